```python
import math
import jax, jax.numpy as jnp
from jax import lax
import numpy as np

D_MODEL = 2048
BATCH = 4
SEQ = 2048
DEPTH = 4

CHUNK = 64
N_MIXERS = 2
N_A = (DEPTH + 1) // 2
N_B = DEPTH // 2
MLA_HEADS = 16
Q_LORA = 512
KV_LORA = 512
QK_NOPE = 128
QK_ROPE = 64
V_HEAD = 128
ROPE_THETA = 10000.0
Q_BLOCK = 128
CA_HEADS = 16
CA_HEAD_DIM = D_MODEL // CA_HEADS
LEFT_CHUNKS = 8
BAND = (LEFT_CHUNKS + 1) * CHUNK
MAX_REL = 256
D_FF = 7 * D_MODEL // 2
N_EXPERTS = 8
TOP_K = 2
EXPERT_BLOCK = 256
LN_EPS = 1e-5
RMS_EPS = 1e-6
DEEPNORM_ALPHA = (2.0 * DEPTH) ** 0.25
DEEPNORM_BETA = (8.0 * DEPTH) ** -0.25
N_MOD = 6

kernel_name = 'hybrid_mla_chunkattn_moe_deepnorm_adaln'


def layer_norm(x, g, b):
    xf = x.astype(jnp.float32)
    mu = jnp.mean(xf, -1, keepdims=True)
    var = jnp.mean(jnp.square(xf - mu), -1, keepdims=True)
    return ((xf - mu) * lax.rsqrt(var + LN_EPS)).astype(x.dtype) * g + b


def rms_norm(x, g):
    xf = x.astype(jnp.float32)
    return (xf * lax.rsqrt(jnp.mean(xf * xf, -1, keepdims=True) + RMS_EPS)).astype(x.dtype) * g


def rope(x, positions):
    half = QK_ROPE // 2
    inv_freq = ROPE_THETA ** (-jnp.arange(half, dtype=jnp.float32) / half)
    ang = positions.astype(jnp.float32)[..., None] * inv_freq
    ang = ang.reshape(ang.shape[:2] + (1,) * (x.ndim - 3) + (half,))
    cos, sin = jnp.cos(ang), jnp.sin(ang)
    x1 = x[..., :half].astype(jnp.float32)
    x2 = x[..., half:].astype(jnp.float32)
    return jnp.concatenate([x1 * cos - x2 * sin, x2 * cos + x1 * sin], -1).astype(x.dtype)


def mla_mixer(u, positions, w_dq, q_norm, w_uq, w_dkv, kv_norm, w_ukv, w_o):
    B, S, _ = u.shape
    H = MLA_HEADS
    c_q = rms_norm(u @ w_dq, q_norm)
    q = (c_q @ w_uq).reshape(B, S, H, QK_NOPE + QK_ROPE)
    q_nope = q[..., :QK_NOPE]
    q_rope = rope(q[..., QK_NOPE:], positions)
    ckv = u @ w_dkv
    c_kv = rms_norm(ckv[..., :KV_LORA], kv_norm)
    k_rope = rope(ckv[..., KV_LORA:], positions)
    kv = (c_kv @ w_ukv).reshape(B, S, H, QK_NOPE + V_HEAD)
    k_nope, v = kv[..., :QK_NOPE], kv[..., QK_NOPE:]
    scale = (QK_NOPE + QK_ROPE) ** -0.5
    key_chunk = jnp.arange(S) // CHUNK

    def attend(i):
        start = i * Q_BLOCK
        qn = lax.dynamic_slice_in_dim(q_nope, start, Q_BLOCK, axis=1)
        qr = lax.dynamic_slice_in_dim(q_rope, start, Q_BLOCK, axis=1)
        s = (jnp.einsum('bqhd,bkhd->bhqk', qn, k_nope)
             + jnp.einsum('bqhr,bkr->bhqk', qr, k_rope)).astype(jnp.float32) * scale
        q_chunk = (start + jnp.arange(Q_BLOCK)) // CHUNK
        mask = key_chunk[None, :] <= q_chunk[:, None]
        s = jnp.where(mask, s, -jnp.inf)
        p = jax.nn.softmax(s, axis=-1).astype(v.dtype)
        return jnp.einsum('bhqk,bkhd->bqhd', p, v)

    o = lax.map(attend, jnp.arange(S // Q_BLOCK))
    o = jnp.moveaxis(o, 0, 1).reshape(B, S, H * V_HEAD)
    return o @ w_o


def chunk_mixer(u, w_qkv, b_qkv, rel_bias, w_o):
    B, S, _ = u.shape
    H, Dh = CA_HEADS, CA_HEAD_DIM
    qkv = (u @ w_qkv + b_qkv).reshape(B, S, 3, H, Dh)
    q, k, v = qkv[:, :, 0], qkv[:, :, 1], qkv[:, :, 2]
    pad = ((0, 0), (LEFT_CHUNKS * CHUNK, 0), (0, 0), (0, 0))
    k_pad, v_pad = jnp.pad(k, pad), jnp.pad(v, pad)
    dist = (np.arange(CHUNK)[:, None] + LEFT_CHUNKS * CHUNK) - np.arange(BAND)[None, :]
    rel_idx = np.clip(dist, -MAX_REL, MAX_REL) + MAX_REL
    bias = rel_bias[:, rel_idx].astype(jnp.float32)
    scale = Dh ** -0.5

    def attend(n):
        qc = lax.dynamic_slice_in_dim(q, n * CHUNK, CHUNK, axis=1)
        kb = lax.dynamic_slice_in_dim(k_pad, n * CHUNK, BAND, axis=1)
        vb = lax.dynamic_slice_in_dim(v_pad, n * CHUNK, BAND, axis=1)
        s = jnp.einsum('bqhd,bkhd->bhqk', qc, kb).astype(jnp.float32) * scale + bias
        valid = (n * CHUNK + jnp.arange(BAND)) >= LEFT_CHUNKS * CHUNK
        s = jnp.where(valid, s, -jnp.inf)
        p = jax.nn.softmax(s, axis=-1).astype(vb.dtype)
        return jnp.einsum('bhqk,bkhd->bqhd', p, vb)

    o = lax.map(attend, jnp.arange(S // CHUNK))
    o = jnp.moveaxis(o, 0, 1).reshape(B, S, H * Dh)
    return o @ w_o


def dense_swiglu(u, w_gate, w_up, w_down):
    return (jax.nn.silu(u @ w_gate) * (u @ w_up)) @ w_down


def moe_swiglu(u, router_w, router_b, w_gate, w_up, w_down):
    B, S, D = u.shape
    xt = u.reshape(-1, D)
    T = xt.shape[0]
    logits = (xt @ router_w).astype(jnp.float32) + router_b.astype(jnp.float32)
    top_val, top_idx = lax.top_k(logits, TOP_K)
    gates = jax.nn.softmax(top_val, axis=-1).astype(u.dtype)
    A = T * TOP_K
    e_flat = top_idx.reshape(-1).astype(jnp.int32)
    g_flat = gates.reshape(-1)
    tok_flat = jnp.arange(A, dtype=jnp.int32) // TOP_K
    order = jnp.argsort(e_flat)
    e_sorted = e_flat[order]
    counts = jnp.bincount(e_flat, length=N_EXPERTS).astype(jnp.int32)
    starts = jnp.cumsum(counts) - counts
    padded = (counts + EXPERT_BLOCK - 1) // EXPERT_BLOCK * EXPERT_BLOCK
    padded_end = jnp.cumsum(padded)
    padded_starts = padded_end - padded
    dest = padded_starts[e_sorted] + jnp.arange(A, dtype=jnp.int32) - starts[e_sorted]
    n_rows = A + N_EXPERTS * EXPERT_BLOCK
    row_tok = jnp.full((n_rows,), T, jnp.int32).at[dest].set(tok_flat[order])
    row_gate = jnp.zeros((n_rows,), u.dtype).at[dest].set(g_flat[order])
    n_blocks = n_rows // EXPERT_BLOCK
    blk_expert = jnp.minimum(
        jnp.searchsorted(padded_end, jnp.arange(n_blocks, dtype=jnp.int32) * EXPERT_BLOCK, side='right'),
        N_EXPERTS - 1)
    x_rows = jnp.concatenate([xt, jnp.zeros((1, D), xt.dtype)], 0)[row_tok].reshape(n_blocks, EXPERT_BLOCK, D)

    def expert_block(args):
        xb, e = args
        return (jax.nn.silu(xb @ w_gate[e]) * (xb @ w_up[e])) @ w_down[e]

    y = lax.map(expert_block, (x_rows, blk_expert)).reshape(n_rows, D)
    out = jnp.zeros((T + 1, D), u.dtype).at[row_tok].add(y * row_gate[:, None])[:T]
    return out.reshape(B, S, D)


def setup_inputs(seed: int = 0) -> dict:
    key = jax.random.key(seed)
    ks = iter(jax.random.split(key, 40))
    D, F, E, H = D_MODEL, D_FF, N_EXPERTS, MLA_HEADS
    nrm = lambda shape, s: jax.random.normal(next(ks), shape, jnp.float32) * s
    x = jax.random.normal(next(ks), (BATCH, SEQ, D), jnp.float32)
    c = jax.random.normal(next(ks), (BATCH, D), jnp.float32)
    offset = jax.random.randint(next(ks), (BATCH, 1), 0, 4096, dtype=jnp.int32)
    positions = offset + jnp.arange(SEQ, dtype=jnp.int32)[None, :]
    return {
        'x': x, 'c': c, 'positions': positions,
        'ada_w': nrm((D, N_MOD * D), 0.1 * D ** -0.5),
        'ada_b': nrm((N_MOD * D,), 0.02),
        'ada_table': nrm((DEPTH, N_MOD, D), 0.02),
        'ln_g': 1.0 + nrm((DEPTH, 2, D), 0.02),
        'ln_b': nrm((DEPTH, 2, D), 0.02),
        'mla_w_dq': nrm((N_A, D, Q_LORA), D ** -0.5),
        'mla_q_norm': 1.0 + nrm((N_A, Q_LORA), 0.02),
        'mla_w_uq': nrm((N_A, Q_LORA, H * (QK_NOPE + QK_ROPE)), Q_LORA ** -0.5),
        'mla_w_dkv': nrm((N_A, D, KV_LORA + QK_ROPE), D ** -0.5),
        'mla_kv_norm': 1.0 + nrm((N_A, KV_LORA), 0.02),
        'mla_w_ukv': nrm((N_A, KV_LORA, H * (QK_NOPE + V_HEAD)), KV_LORA ** -0.5),
        'mla_w_o': nrm((N_A, H * V_HEAD, D), DEEPNORM_BETA * (H * V_HEAD) ** -0.5),
        'ca_w_qkv': nrm((N_B, D, 3 * CA_HEADS * CA_HEAD_DIM), D ** -0.5),
        'ca_b_qkv': nrm((N_B, 3 * CA_HEADS * CA_HEAD_DIM), 0.02),
        'ca_rel_bias': nrm((N_B, CA_HEADS, 2 * MAX_REL + 1), 0.3),
        'ca_w_o': nrm((N_B, CA_HEADS * CA_HEAD_DIM, D), DEEPNORM_BETA * (CA_HEADS * CA_HEAD_DIM) ** -0.5),
        'ffn_w_gate': nrm((N_A, D, F), D ** -0.5),
        'ffn_w_up': nrm((N_A, D, F), D ** -0.5),
        'ffn_w_down': nrm((N_A, F, D), DEEPNORM_BETA * F ** -0.5),
        'moe_router_w': nrm((N_B, D, E), D ** -0.5),
        'moe_router_b': nrm((N_B, E), 0.01),
        'moe_w_gate': nrm((N_B, E, D, F), D ** -0.5),
        'moe_w_up': nrm((N_B, E, D, F), D ** -0.5),
        'moe_w_down': nrm((N_B, E, F, D), DEEPNORM_BETA * F ** -0.5),
    }


def reference(x, c, positions, ada_w, ada_b, ada_table, ln_g, ln_b,
              mla_w_dq, mla_q_norm, mla_w_uq, mla_w_dkv, mla_kv_norm, mla_w_ukv, mla_w_o,
              ca_w_qkv, ca_b_qkv, ca_rel_bias, ca_w_o,
              ffn_w_gate, ffn_w_up, ffn_w_down,
              moe_router_w, moe_router_b, moe_w_gate, moe_w_up, moe_w_down):
    B, S, D = x.shape
    cond = (jax.nn.silu(c) @ ada_w + ada_b).reshape(B, N_MOD, D)
    h = x
    for i in range(DEPTH):
        mod = cond + ada_table[i]
        shift_m, scale_m, gate_m = mod[:, 0, None], mod[:, 1, None], mod[:, 2, None]
        shift_f, scale_f, gate_f = mod[:, 3, None], mod[:, 4, None], mod[:, 5, None]
        j = i // N_MIXERS
        u = h * (1.0 + scale_m) + shift_m
        if i % N_MIXERS == 0:
            y = mla_mixer(u, positions, mla_w_dq[j], mla_q_norm[j], mla_w_uq[j], mla_w_dkv[j],
                          mla_kv_norm[j], mla_w_ukv[j], mla_w_o[j])
        else:
            y = chunk_mixer(u, ca_w_qkv[j], ca_b_qkv[j], ca_rel_bias[j], ca_w_o[j])
        h = layer_norm(DEEPNORM_ALPHA * h + (1.0 + gate_m) * y, ln_g[i, 0], ln_b[i, 0])
        u = h * (1.0 + scale_f) + shift_f
        if i % 2 == 0:
            y = dense_swiglu(u, ffn_w_gate[j], ffn_w_up[j], ffn_w_down[j])
        else:
            y = moe_swiglu(u, moe_router_w[j], moe_router_b[j], moe_w_gate[j], moe_w_up[j], moe_w_down[j])
        h = layer_norm(DEEPNORM_ALPHA * h + (1.0 + gate_f) * y, ln_g[i, 1], ln_b[i, 1])
    return h
```

```python
import functools

import numpy as np
import jax
import jax.numpy as jnp
from jax import lax
from jax.experimental import pallas as pl
from jax.experimental.pallas import tpu as pltpu

F32 = jnp.float32
BF16 = jnp.bfloat16

CHUNK = 64
MLA_HEADS = 16
QK_NOPE = 128
QK_ROPE = 64
V_HEAD = 128
KV_LORA = 512
ROPE_THETA = 10000.0
CA_HEADS = 16
CA_HEAD_DIM = 128
LEFT_CHUNKS = 8
MAX_REL = 256
N_EXPERTS = 8
TOP_K = 2
N_MOD = 6
LN_EPS = 1e-5
RMS_EPS = 1e-6
DEPTH = 4
DEEPNORM_ALPHA = (2.0 * DEPTH) ** 0.25

LANES = 128
VMEM_LIMIT = 56 * 1024 * 1024

MLA_HEAD_PAD = 2 * LANES
MOE_SLOT_ROWS = 1024
MOE_ROW_BLOCK = 256
CA_QBLOCK = 256
CA_KBLOCKS = 3

_NT = (((1,), (1,)), ((), ()))


def _params(*sem):
    return pltpu.CompilerParams(dimension_semantics=sem, vmem_limit_bytes=VMEM_LIMIT)


def _silu(x):
    return x * jax.nn.sigmoid(x)


def _layer_norm(z, g, b):
    mu = jnp.mean(z, axis=-1, keepdims=True)
    zc = z - mu
    var = jnp.mean(zc * zc, axis=-1, keepdims=True)
    return zc * lax.rsqrt(var + LN_EPS) * g + b


def _modulate(h, mod_ref, k):
    return h * (1.0 + mod_ref[0, k + 1:k + 2, :]) + mod_ref[0, k:k + 1, :]


def _cond_kernel(c_ref, w_ref, b_ref, tab_ref, o_ref):
    s = _silu(c_ref[...]).astype(BF16)
    y = jnp.dot(s, w_ref[...].astype(BF16), preferred_element_type=F32) + b_ref[...]
    o_ref[...] = y[None, :, :] + tab_ref[...][:, None, :]


def _cond(c, ada_w, ada_b, ada_table):
    B, D = c.shape
    N = ada_w.shape[1]
    tn = 1024
    c8 = jnp.zeros((8, D), F32).at[:B].set(c)
    out = pl.pallas_call(
        _cond_kernel,
        grid=(N // tn,),
        in_specs=[
            pl.BlockSpec((8, D), lambda n: (0, 0)),
            pl.BlockSpec((D, tn), lambda n: (0, n)),
            pl.BlockSpec((1, tn), lambda n: (0, n)),
            pl.BlockSpec((DEPTH, tn), lambda n: (0, n)),
        ],
        out_specs=pl.BlockSpec((DEPTH, 8, tn), lambda n: (0, 0, n)),
        out_shape=jax.ShapeDtypeStruct((DEPTH, 8, N), F32),
        compiler_params=_params("arbitrary"),
        name="cond",
    )(c8, ada_w, ada_b.reshape(1, N), ada_table.reshape(DEPTH, N))
    mod = out[:, :B].reshape(DEPTH, B, N_MOD, D)
    return jnp.pad(mod, ((0, 0), (0, 0), (0, 8 - N_MOD), (0, 0)))


def _rope_lanes(x, cos, sin_signed):
    lane = lax.broadcasted_iota(jnp.int32, x.shape, 1)
    partner = jnp.where(lane < QK_ROPE // 2, pltpu.roll(x, LANES - QK_ROPE // 2, 1),
                        pltpu.roll(x, QK_ROPE // 2, 1))
    return x * cos + partner * sin_signed


def _mla_proj_kernel(h_ref, mod_ref, pos_ref, freq_ref, wdq_ref, qn_ref, wdkv_ref, kvn_ref, wkr_ref,
                     wuq_ref, wukv_ref, q_ref, k_ref, v_ref):
    u = _modulate(h_ref[...], mod_ref, 0).astype(BF16)
    ang = pos_ref[...].astype(F32) * freq_ref[...]
    cos = jnp.cos(ang)
    sin = jnp.sin(ang)
    lane = lax.broadcasted_iota(jnp.int32, ang.shape, 1)
    sin_signed = jnp.where(lane < QK_ROPE // 2, -sin, sin)

    def rms(x, g):
        return (x * lax.rsqrt(jnp.mean(x * x, axis=-1, keepdims=True) + RMS_EPS)) * g

    cq = rms(jnp.dot(u, wdq_ref[...], preferred_element_type=F32), qn_ref[...]).astype(BF16)
    ckv = rms(jnp.dot(u, wdkv_ref[...], preferred_element_type=F32), kvn_ref[...]).astype(BF16)
    kr = jnp.dot(u, wkr_ref[...], preferred_element_type=F32)
    kr = _rope_lanes(kr, cos, sin_signed).astype(BF16)
    for hd in range(MLA_HEADS):
        c0 = hd * MLA_HEAD_PAD
        qh = jnp.dot(cq, wuq_ref[:, c0:c0 + MLA_HEAD_PAD], preferred_element_type=F32)
        q_ref[:, c0:c0 + LANES] = qh[:, :LANES].astype(BF16)
        q_ref[:, c0 + LANES:c0 + MLA_HEAD_PAD] = _rope_lanes(qh[:, LANES:], cos, sin_signed).astype(BF16)
        kvh = jnp.dot(ckv, wukv_ref[:, c0:c0 + MLA_HEAD_PAD], preferred_element_type=F32)
        k_ref[:, c0:c0 + LANES] = kvh[:, :LANES].astype(BF16)
        k_ref[:, c0 + LANES:c0 + MLA_HEAD_PAD] = kr
        v_ref[:, hd * V_HEAD:(hd + 1) * V_HEAD] = kvh[:, LANES:].astype(BF16)


def _mla_proj(h, mod_l, pos_col, freq_row, w_dq, q_norm, w_dkv, kv_norm, w_uq, w_ukv, S):
    T, D = h.shape
    H = MLA_HEADS
    tm = 256
    q_lora = w_dq.shape[1]
    wdq = w_dq.astype(BF16)
    wdkv = w_dkv[:, :KV_LORA].astype(BF16)
    wkr = jnp.pad(w_dkv[:, KV_LORA:], ((0, 0), (0, LANES - QK_ROPE))).astype(BF16)
    wuq = jnp.pad(w_uq.reshape(q_lora, H, QK_NOPE + QK_ROPE),
                  ((0, 0), (0, 0), (0, MLA_HEAD_PAD - QK_NOPE - QK_ROPE))).reshape(q_lora, H * MLA_HEAD_PAD)
    wuq = wuq.astype(BF16)
    wukv = w_ukv.astype(BF16)
    const = lambda i: (0, 0)
    row = lambda i: (i, 0)
    return pl.pallas_call(
        _mla_proj_kernel,
        grid=(T // tm,),
        in_specs=[
            pl.BlockSpec((tm, D), row),
            pl.BlockSpec((1, 8, D), lambda i: (i * tm // S, 0, 0)),
            pl.BlockSpec((tm, 1), row),
            pl.BlockSpec((1, LANES), const),
            pl.BlockSpec(wdq.shape, const),
            pl.BlockSpec((1, q_lora), const),
            pl.BlockSpec(wdkv.shape, const),
            pl.BlockSpec((1, KV_LORA), const),
            pl.BlockSpec(wkr.shape, const),
            pl.BlockSpec(wuq.shape, const),
            pl.BlockSpec(wukv.shape, const),
        ],
        out_specs=[
            pl.BlockSpec((tm, H * MLA_HEAD_PAD), row),
            pl.BlockSpec((tm, H * MLA_HEAD_PAD), row),
            pl.BlockSpec((tm, H * V_HEAD), row),
        ],
        out_shape=[
            jax.ShapeDtypeStruct((T, H * MLA_HEAD_PAD), BF16),
            jax.ShapeDtypeStruct((T, H * MLA_HEAD_PAD), BF16),
            jax.ShapeDtypeStruct((T, H * V_HEAD), BF16),
        ],
        compiler_params=_params("arbitrary"),
        name="mla_proj",
    )(h, mod_l, pos_col, freq_row, wdq, q_norm.reshape(1, -1), wdkv, kv_norm.reshape(1, -1), wkr, wuq, wukv)


def _mla_attn_kernel(q_ref, k_ref, v_ref, o_ref, m_sc, l_sc, acc_sc, *, scale):
    i = pl.program_id(2)
    j = pl.program_id(3)

    @pl.when(j == 0)
    def _():
        m_sc[...] = jnp.full(m_sc.shape, -jnp.inf, F32)
        l_sc[...] = jnp.zeros(l_sc.shape, F32)
        acc_sc[...] = jnp.zeros(acc_sc.shape, F32)

    @pl.when(j <= i)
    def _():
        s = lax.dot_general(q_ref[...], k_ref[...], _NT, preferred_element_type=F32) * scale
        qc = lax.broadcasted_iota(jnp.int32, s.shape, 0) // CHUNK
        kc = lax.broadcasted_iota(jnp.int32, s.shape, 1) // CHUNK
        s = jnp.where((kc <= qc) | (j < i), s, -jnp.inf)
        m_prev = m_sc[...]
        m_new = jnp.maximum(m_prev, jnp.max(s, axis=-1, keepdims=True))
        a = jnp.exp(m_prev - m_new)
        p = jnp.exp(s - m_new)
        l_sc[...] = a * l_sc[...] + jnp.sum(p, axis=-1, keepdims=True)
        acc_sc[...] = a * acc_sc[...] + jnp.dot(p.astype(BF16), v_ref[...], preferred_element_type=F32)
        m_sc[...] = m_new

    @pl.when(j == pl.num_programs(3) - 1)
    def _():
        o_ref[...] = (acc_sc[...] / l_sc[...]).astype(o_ref.dtype)


def _mla_attn(q, k, v, B, S):
    H = MLA_HEADS
    tq = 512
    nq = S // tq
    scale = (QK_NOPE + QK_ROPE) ** -0.5
    return pl.pallas_call(
        functools.partial(_mla_attn_kernel, scale=scale),
        grid=(B, H, nq, nq),
        in_specs=[
            pl.BlockSpec((tq, MLA_HEAD_PAD), lambda b, h, i, j: (b * nq + i, h)),
            pl.BlockSpec((tq, MLA_HEAD_PAD), lambda b, h, i, j: (b * nq + jnp.minimum(i, j), h)),
            pl.BlockSpec((tq, V_HEAD), lambda b, h, i, j: (b * nq + jnp.minimum(i, j), h)),
        ],
        out_specs=pl.BlockSpec((tq, V_HEAD), lambda b, h, i, j: (b * nq + i, h)),
        out_shape=jax.ShapeDtypeStruct((B * S, H * V_HEAD), BF16),
        scratch_shapes=[pltpu.VMEM((tq, 1), F32), pltpu.VMEM((tq, 1), F32), pltpu.VMEM((tq, V_HEAD), F32)],
        compiler_params=_params("arbitrary", "arbitrary", "arbitrary", "arbitrary"),
        name="mla_attn",
    )(q, k, v)


def _proj_ln_kernel(o_ref, w_ref, h_ref, mod_ref, g_ref, b_ref, out_ref, *, gate_row):
    y = jnp.dot(o_ref[...], w_ref[...], preferred_element_type=F32)
    z = DEEPNORM_ALPHA * h_ref[...] + (1.0 + mod_ref[0, gate_row:gate_row + 1, :]) * y
    out_ref[...] = _layer_norm(z, g_ref[...], b_ref[...])


def _proj_ln(o, w_o, h, mod_l, ln_g, ln_b, S):
    T, D = h.shape
    tm = 512
    w = w_o.astype(BF16)
    row = lambda i: (i, 0)
    const = lambda i: (0, 0)
    return pl.pallas_call(
        functools.partial(_proj_ln_kernel, gate_row=2),
        grid=(T // tm,),
        in_specs=[
            pl.BlockSpec((tm, o.shape[1]), row),
            pl.BlockSpec(w.shape, const),
            pl.BlockSpec((tm, D), row),
            pl.BlockSpec((1, 8, D), lambda i: (i * tm // S, 0, 0)),
            pl.BlockSpec((1, D), const),
            pl.BlockSpec((1, D), const),
        ],
        out_specs=pl.BlockSpec((tm, D), row),
        out_shape=jax.ShapeDtypeStruct((T, D), F32),
        compiler_params=_params("arbitrary"),
        name="proj_ln",
    )(o, w, h, mod_l, ln_g.reshape(1, D), ln_b.reshape(1, D))


def _ffn_kernel(h_ref, mod_ref, wg_ref, wu_ref, wd_ref, g_ref, b_ref, out_ref, u_sc):
    f = pl.program_id(1)

    @pl.when(f == 0)
    def _():
        u_sc[...] = _modulate(h_ref[...], mod_ref, 3).astype(BF16)
        out_ref[...] = jnp.zeros(out_ref.shape, F32)

    u = u_sc[...]
    g = jnp.dot(u, wg_ref[...].astype(BF16), preferred_element_type=F32)
    up = jnp.dot(u, wu_ref[...].astype(BF16), preferred_element_type=F32)
    hm = (_silu(g) * up).astype(BF16)
    out_ref[...] += jnp.dot(hm, wd_ref[...].astype(BF16), preferred_element_type=F32)

    @pl.when(f == pl.num_programs(1) - 1)
    def _():
        z = DEEPNORM_ALPHA * h_ref[...] + (1.0 + mod_ref[0, 5:6, :]) * out_ref[...]
        out_ref[...] = _layer_norm(z, g_ref[...], b_ref[...])


def _ffn(h, mod_l, w_gate, w_up, w_down, layer, ln_g, ln_b, S):
    T, D = h.shape
    F = w_gate.shape[2]
    tm = min(1024, T)
    tf = 256
    row = lambda i, f: (i, 0)
    const = lambda i, f: (0, 0)
    return pl.pallas_call(
        _ffn_kernel,
        grid=(T // tm, F // tf),
        in_specs=[
            pl.BlockSpec((tm, D), row, pipeline_mode=pl.Buffered(1)),
            pl.BlockSpec((1, 8, D), lambda i, f: (i * tm // S, 0, 0)),
            pl.BlockSpec((None, D, tf), lambda i, f: (layer, 0, f)),
            pl.BlockSpec((None, D, tf), lambda i, f: (layer, 0, f)),
            pl.BlockSpec((None, tf, D), lambda i, f: (layer, f, 0)),
            pl.BlockSpec((1, D), const),
            pl.BlockSpec((1, D), const),
        ],
        out_specs=pl.BlockSpec((tm, D), row),
        out_shape=jax.ShapeDtypeStruct((T, D), F32),
        scratch_shapes=[pltpu.VMEM((tm, D), BF16)],
        compiler_params=_params("arbitrary", "arbitrary"),
        name="ffn",
    )(h, mod_l, w_gate, w_up, w_down, ln_g.reshape(1, D), ln_b.reshape(1, D))


def _qkv_kernel(h_ref, mod_ref, w_ref, b_ref, o_ref, u_sc):
    @pl.when(pl.program_id(1) == 0)
    def _():
        u_sc[...] = _modulate(h_ref[...], mod_ref, 0).astype(BF16)

    y = jnp.dot(u_sc[...], w_ref[...].astype(BF16), preferred_element_type=F32) + b_ref[...]
    o_ref[...] = y.astype(o_ref.dtype)


def _qkv(h, mod_l, w_qkv, layer, b_qkv, S):
    T, D = h.shape
    N = w_qkv.shape[2]
    tm = min(1024, T)
    tn = 512
    return pl.pallas_call(
        _qkv_kernel,
        grid=(T // tm, N // tn),
        in_specs=[
            pl.BlockSpec((tm, D), lambda i, n: (i, 0), pipeline_mode=pl.Buffered(1)),
            pl.BlockSpec((1, 8, D), lambda i, n: (i * tm // S, 0, 0)),
            pl.BlockSpec((None, D, tn), lambda i, n: (layer, 0, n)),
            pl.BlockSpec((1, tn), lambda i, n: (0, n)),
        ],
        out_specs=pl.BlockSpec((tm, tn), lambda i, n: (i, n)),
        out_shape=jax.ShapeDtypeStruct((T, N), BF16),
        scratch_shapes=[pltpu.VMEM((tm, D), BF16)],
        compiler_params=_params("arbitrary", "arbitrary"),
        name="qkv",
    )(h, mod_l, w_qkv, b_qkv.reshape(1, N))


def _ca_attn_kernel(q_ref, k0_ref, k1_ref, k2_ref, v0_ref, v1_ref, v2_ref, bias_ref, o_ref, *, scale):
    i = pl.program_id(2)
    q = q_ref[...]
    tq = q.shape[0]
    k_refs = (k0_ref, k1_ref, k2_ref)
    v_refs = (v0_ref, v1_ref, v2_ref)
    s = []
    for p in range(CA_KBLOCKS):
        sp = lax.dot_general(q, k_refs[p][...], _NT, preferred_element_type=F32) * scale
        sp = sp + bias_ref[0, :, p * tq:(p + 1) * tq]
        first = CA_KBLOCKS - 1 - p
        if first > 0:
            sp = jnp.where(i >= first, sp, -jnp.inf)
        s.append(sp)
    m = functools.reduce(jnp.maximum, [jnp.max(sp, axis=-1, keepdims=True) for sp in s])
    e = [jnp.exp(sp - m) for sp in s]
    l = functools.reduce(jnp.add, [jnp.sum(ep, axis=-1, keepdims=True) for ep in e])
    acc = functools.reduce(jnp.add, [jnp.dot(ep.astype(BF16), v_refs[p][...], preferred_element_type=F32)
                                      for p, ep in enumerate(e)])
    o_ref[...] = (acc / l).astype(o_ref.dtype)


def _ca_bias_table(rel_bias):
    tq = CA_QBLOCK
    r = np.arange(tq)[:, None]
    c = np.arange(CA_KBLOCKS * tq)[None, :]
    dist = r + (CA_KBLOCKS - 1) * tq - c
    rel_idx = np.clip(dist, -MAX_REL, MAX_REL) + MAX_REL
    qc = r // CHUNK + (CA_KBLOCKS - 1) * tq // CHUNK
    kc = c // CHUNK
    band = (kc <= qc) & (kc >= qc - LEFT_CHUNKS)
    return jnp.where(band[None], rel_bias[:, rel_idx].astype(F32), -jnp.inf)


def _ca_attn(qkv, rel_bias, B, S):
    H, Dh = CA_HEADS, CA_HEAD_DIM
    tq = CA_QBLOCK
    nq = S // tq
    bias = _ca_bias_table(rel_bias)

    def kv_spec(which, p):
        back = CA_KBLOCKS - 1 - p
        return pl.BlockSpec((tq, Dh), lambda b, h, i: (b * nq + jnp.maximum(i - back, 0), which * H + h))

    return pl.pallas_call(
        functools.partial(_ca_attn_kernel, scale=Dh ** -0.5),
        grid=(B, H, nq),
        in_specs=[pl.BlockSpec((tq, Dh), lambda b, h, i: (b * nq + i, h))]
        + [kv_spec(1, p) for p in range(CA_KBLOCKS)]
        + [kv_spec(2, p) for p in range(CA_KBLOCKS)]
        + [pl.BlockSpec((1, tq, CA_KBLOCKS * tq), lambda b, h, i: (h, 0, 0))],
        out_specs=pl.BlockSpec((tq, Dh), lambda b, h, i: (b * nq + i, h)),
        out_shape=jax.ShapeDtypeStruct((B * S, H * Dh), BF16),
        compiler_params=_params("arbitrary", "arbitrary", "arbitrary"),
        name="ca_attn",
    )(qkv, qkv, qkv, qkv, qkv, qkv, qkv, bias)


def _router_kernel(h_ref, mod_ref, rw_ref, rb_ref, u_ref, idx_ref, gate_ref, rank_ref, cnt_ref, carry_sc):
    t = pl.program_id(0)

    @pl.when(t == 0)
    def _():
        carry_sc[...] = jnp.zeros(carry_sc.shape, F32)

    u = _modulate(h_ref[...], mod_ref, 3)
    u_ref[...] = u
    tb = u.shape[0]
    logits = lax.dot_general(rw_ref[...], u, _NT, precision=lax.Precision.HIGHEST,
                             preferred_element_type=F32) + rb_ref[...]
    eid = lax.broadcasted_iota(jnp.int32, logits.shape, 0)
    m1 = jnp.max(logits, axis=0, keepdims=True)
    i1 = jnp.min(jnp.where(logits == m1, eid, N_EXPERTS), axis=0, keepdims=True)
    rest = jnp.where(eid == i1, -jnp.inf, logits)
    m2 = jnp.max(rest, axis=0, keepdims=True)
    i2 = jnp.min(jnp.where(rest == m2, eid, N_EXPERTS), axis=0, keepdims=True)
    e2 = jnp.exp(m2 - m1)
    den = 1.0 + e2
    idx_ref[...] = jnp.concatenate([i1, i2], axis=0)
    gate_ref[...] = jnp.concatenate([1.0 / den, e2 / den], axis=0)
    onehot = jnp.where((eid == i1) | (eid == i2), 1.0, 0.0)
    tri = jnp.where(lax.broadcasted_iota(jnp.int32, (tb, tb), 0) <= lax.broadcasted_iota(jnp.int32, (tb, tb), 1),
                    1.0, 0.0).astype(BF16)
    incl = jnp.dot(onehot.astype(BF16), tri, preferred_element_type=F32)
    excl = incl - onehot + carry_sc[:, 0:1]
    r1 = jnp.sum(jnp.where(eid == i1, excl, 0.0), axis=0, keepdims=True)
    r2 = jnp.sum(jnp.where(eid == i2, excl, 0.0), axis=0, keepdims=True)
    rank_ref[...] = jnp.concatenate([r1, r2], axis=0).astype(jnp.int32)
    carry_sc[...] = carry_sc[...] + jnp.sum(onehot, axis=1, keepdims=True)
    cnt_ref[...] = carry_sc[...].astype(jnp.int32)


def _router(h, mod_l, router_w, router_b, S):
    T, D = h.shape
    E = N_EXPERTS
    tb = 512
    return pl.pallas_call(
        _router_kernel,
        grid=(T // tb,),
        in_specs=[
            pl.BlockSpec((tb, D), lambda t: (t, 0)),
            pl.BlockSpec((1, 8, D), lambda t: (t * tb // S, 0, 0)),
            pl.BlockSpec((E, D), lambda t: (0, 0)),
            pl.BlockSpec((E, 1), lambda t: (0, 0)),
        ],
        out_specs=[
            pl.BlockSpec((tb, D), lambda t: (t, 0)),
            pl.BlockSpec((TOP_K, tb), lambda t: (0, t)),
            pl.BlockSpec((TOP_K, tb), lambda t: (0, t)),
            pl.BlockSpec((TOP_K, tb), lambda t: (0, t)),
            pl.BlockSpec((E, LANES), lambda t: (0, 0)),
        ],
        out_shape=[
            jax.ShapeDtypeStruct((T, D), F32),
            jax.ShapeDtypeStruct((TOP_K, T), jnp.int32),
            jax.ShapeDtypeStruct((TOP_K, T), F32),
            jax.ShapeDtypeStruct((TOP_K, T), jnp.int32),
            jax.ShapeDtypeStruct((E, LANES), jnp.int32),
        ],
        scratch_shapes=[pltpu.VMEM((E, LANES), F32)],
        compiler_params=_params("arbitrary"),
        name="router",
    )(h, mod_l, router_w.T, router_b.reshape(E, 1))


def _row_gather_start(src_hbm, dst, sem, rows_ref, base, n):
    def issue(j, carry):
        r = rows_ref[base + j]
        pltpu.make_async_copy(src_hbm.at[pl.ds(r, 1), :], dst.at[pl.ds(j, 1), :], sem).start()
        return carry
    lax.fori_loop(0, n, issue, 0)


def _row_gather_wait(src_hbm, dst, sem, n):
    pltpu.make_async_copy(src_hbm.at[pl.ds(0, n), :], dst, sem).wait()


def _dispatch_kernel(row_tok_ref, nblk_ref, u_hbm, o_ref, buf, sem):
    g = pl.program_id(0)
    per_slot = MOE_SLOT_ROWS // MOE_ROW_BLOCK
    valid = (g % per_slot) < nblk_ref[g // per_slot]

    @pl.when(valid)
    def _():
        _row_gather_start(u_hbm, buf, sem.at[0], row_tok_ref, g * MOE_ROW_BLOCK, MOE_ROW_BLOCK)
        _row_gather_wait(u_hbm, buf, sem.at[0], MOE_ROW_BLOCK)
        o_ref[...] = buf[...].astype(BF16)

    @pl.when(jnp.logical_not(valid))
    def _():
        o_ref[...] = jnp.zeros(o_ref.shape, BF16)


def _dispatch(u, row_tok, nblk, n_rows):
    T, D = u.shape
    rb = MOE_ROW_BLOCK
    return pl.pallas_call(
        _dispatch_kernel,
        grid_spec=pltpu.PrefetchScalarGridSpec(
            num_scalar_prefetch=2,
            grid=(n_rows // rb,),
            in_specs=[pl.BlockSpec(memory_space=pl.ANY)],
            out_specs=pl.BlockSpec((rb, D), lambda g, rt, nb: (g, 0)),
            scratch_shapes=[pltpu.VMEM((rb, D), F32), pltpu.SemaphoreType.DMA((1,))],
        ),
        out_shape=jax.ShapeDtypeStruct((n_rows, D), BF16),
        compiler_params=_params("arbitrary"),
        name="moe_dispatch",
    )(row_tok, nblk, u)


def _moe_ffn_kernel(se_ref, nblk_ref, x_ref, wg_ref, wu_ref, wd_ref, y_ref, wg_sc, wu_sc, wd_sc):
    s = pl.program_id(0)
    f = pl.program_id(1)
    nb = nblk_ref[s]

    @pl.when(f == 0)
    def _():
        y_ref[...] = jnp.zeros(y_ref.shape, F32)

    @pl.when(nb > 0)
    def _():
        wg_sc[...] = wg_ref[...].astype(BF16)
        wu_sc[...] = wu_ref[...].astype(BF16)
        wd_sc[...] = wd_ref[...].astype(BF16)

        def body(r, carry):
            rows = pl.ds(pl.multiple_of(r * MOE_ROW_BLOCK, MOE_ROW_BLOCK), MOE_ROW_BLOCK)
            x = x_ref[rows, :]
            g = jnp.dot(x, wg_sc[...], preferred_element_type=F32)
            up = jnp.dot(x, wu_sc[...], preferred_element_type=F32)
            hm = (_silu(g) * up).astype(BF16)
            y_ref[rows, :] += jnp.dot(hm, wd_sc[...], preferred_element_type=F32)
            return carry

        lax.fori_loop(0, nb, body, 0)


def _moe_ffn(xs, slot_e, nblk, w_gate, w_up, w_down, layer):
    n_rows, D = xs.shape
    F = w_gate.shape[3]
    R = MOE_SLOT_ROWS
    tf = 256
    nf = F // tf

    def f_eff(s, f, nb):
        return jnp.where(nb[s] > 0, f, nf - 1)

    return pl.pallas_call(
        _moe_ffn_kernel,
        grid_spec=pltpu.PrefetchScalarGridSpec(
            num_scalar_prefetch=2,
            grid=(n_rows // R, nf),
            in_specs=[
                pl.BlockSpec((R, D), lambda s, f, se, nb: (s, 0)),
                pl.BlockSpec((None, None, D, tf), lambda s, f, se, nb: (layer, se[s], 0, f_eff(s, f, nb))),
                pl.BlockSpec((None, None, D, tf), lambda s, f, se, nb: (layer, se[s], 0, f_eff(s, f, nb))),
                pl.BlockSpec((None, None, tf, D), lambda s, f, se, nb: (layer, se[s], f_eff(s, f, nb), 0)),
            ],
            out_specs=pl.BlockSpec((R, D), lambda s, f, se, nb: (s, 0)),
            scratch_shapes=[pltpu.VMEM((D, tf), BF16), pltpu.VMEM((D, tf), BF16), pltpu.VMEM((tf, D), BF16)],
        ),
        out_shape=jax.ShapeDtypeStruct((n_rows, D), F32),
        compiler_params=_params("arbitrary", "arbitrary"),
        name="moe_ffn",
    )(slot_e, nblk, xs, w_gate, w_up, w_down)


def _combine_kernel(dest_ref, y_hbm, gate_ref, h_ref, mod_ref, g_ref, b_ref, out_ref, buf0, buf1, sem, *, n_tok):
    i = pl.program_id(0)
    tb = out_ref.shape[0]
    _row_gather_start(y_hbm, buf0, sem.at[0], dest_ref, i * tb, tb)
    _row_gather_start(y_hbm, buf1, sem.at[1], dest_ref, n_tok + i * tb, tb)
    _row_gather_wait(y_hbm, buf0, sem.at[0], tb)
    _row_gather_wait(y_hbm, buf1, sem.at[1], tb)
    y = gate_ref[:, 0:1] * buf0[...] + gate_ref[:, 1:2] * buf1[...]
    z = DEEPNORM_ALPHA * h_ref[...] + (1.0 + mod_ref[0, 5:6, :]) * y
    out_ref[...] = _layer_norm(z, g_ref[...], b_ref[...])


def _combine(y, dest, gates_tk, h, mod_l, ln_g, ln_b, S):
    T, D = h.shape
    tb = 256
    return pl.pallas_call(
        functools.partial(_combine_kernel, n_tok=T),
        grid_spec=pltpu.PrefetchScalarGridSpec(
            num_scalar_prefetch=1,
            grid=(T // tb,),
            in_specs=[
                pl.BlockSpec(memory_space=pl.ANY),
                pl.BlockSpec((tb, TOP_K), lambda i, d: (i, 0)),
                pl.BlockSpec((tb, D), lambda i, d: (i, 0)),
                pl.BlockSpec((1, 8, D), lambda i, d: (i * tb // S, 0, 0)),
                pl.BlockSpec((1, D), lambda i, d: (0, 0)),
                pl.BlockSpec((1, D), lambda i, d: (0, 0)),
            ],
            out_specs=pl.BlockSpec((tb, D), lambda i, d: (i, 0)),
            scratch_shapes=[pltpu.VMEM((tb, D), F32), pltpu.VMEM((tb, D), F32), pltpu.SemaphoreType.DMA((2,))],
        ),
        out_shape=jax.ShapeDtypeStruct((T, D), F32),
        compiler_params=_params("arbitrary"),
        name="moe_combine",
    )(dest, y, gates_tk, h, mod_l, ln_g.reshape(1, D), ln_b.reshape(1, D))


def _moe_layer(h, mod_l, router_w, router_b, w_gate, w_up, w_down, layer, ln_g, ln_b, S):
    T, D = h.shape
    E, R, rb = N_EXPERTS, MOE_SLOT_ROWS, MOE_ROW_BLOCK
    n_slots = T * TOP_K // R + E
    n_rows = n_slots * R
    u, idx, gates, rank, cnt = _router(h, mod_l, router_w, router_b, S)
    cnt = cnt[:, 0]
    slots_e = (cnt + R - 1) // R
    slot_end = jnp.cumsum(slots_e)
    slot_start = slot_end - slots_e
    dest = (slot_start * R)[idx] + rank
    tok = jnp.tile(jnp.arange(T, dtype=jnp.int32), TOP_K)
    row_tok = jnp.zeros((n_rows,), jnp.int32).at[dest.reshape(-1)].set(tok)
    sid = jnp.arange(n_slots, dtype=jnp.int32)
    n_used = slot_end[-1]
    last_used = jnp.maximum(n_used - 1, 0)
    se = jnp.minimum(jnp.searchsorted(slot_end, jnp.minimum(sid, last_used), side='right'), E - 1).astype(jnp.int32)
    rows_in = jnp.clip(cnt[se] - (sid - slot_start[se]) * R, 0, R)
    rows_in = jnp.where(sid < n_used, rows_in, 0)
    nblk = ((rows_in + rb - 1) // rb).astype(jnp.int32)
    xs = _dispatch(u, row_tok, nblk, n_rows)
    y = _moe_ffn(xs, se, nblk, w_gate, w_up, w_down, layer)
    return _combine(y, dest.reshape(-1).astype(jnp.int32), gates.T, h, mod_l, ln_g, ln_b, S)


def kernel(x, c, positions, ada_w, ada_b, ada_table, ln_g, ln_b, mla_w_dq, mla_q_norm, mla_w_uq, mla_w_dkv, mla_kv_norm, mla_w_ukv, mla_w_o, ca_w_qkv, ca_b_qkv, ca_rel_bias, ca_w_o, ffn_w_gate, ffn_w_up, ffn_w_down, moe_router_w, moe_router_b, moe_w_gate, moe_w_up, moe_w_down):
    B, S, D = x.shape
    T = B * S
    mod = _cond(c, ada_w, ada_b, ada_table)
    half = QK_ROPE // 2
    inv_freq = ROPE_THETA ** (-jnp.arange(half, dtype=F32) / half)
    freq_row = jnp.concatenate([inv_freq, inv_freq, jnp.zeros((LANES - QK_ROPE,), F32)]).reshape(1, LANES)
    pos_col = positions.reshape(T, 1).astype(jnp.int32)
    h = x.reshape(T, D)
    for i in range(DEPTH):
        j = i // 2
        mod_l = mod[i]
        if i % 2 == 0:
            q, k, v = _mla_proj(h, mod_l, pos_col, freq_row, mla_w_dq[j], mla_q_norm[j], mla_w_dkv[j],
                                mla_kv_norm[j], mla_w_uq[j], mla_w_ukv[j], S)
            o = _mla_attn(q, k, v, B, S)
            h = _proj_ln(o, mla_w_o[j], h, mod_l, ln_g[i, 0], ln_b[i, 0], S)
            h = _ffn(h, mod_l, ffn_w_gate, ffn_w_up, ffn_w_down, j, ln_g[i, 1], ln_b[i, 1], S)
        else:
            qkv = _qkv(h, mod_l, ca_w_qkv, j, ca_b_qkv[j], S)
            o = _ca_attn(qkv, ca_rel_bias[j], B, S)
            h = _proj_ln(o, ca_w_o[j], h, mod_l, ln_g[i, 0], ln_b[i, 0], S)
            h = _moe_layer(h, mod_l, moe_router_w[j], moe_router_b[j], moe_w_gate, moe_w_up,
                           moe_w_down, j, ln_g[i, 1], ln_b[i, 1], S)
    return h.reshape(B, S, D)
```

```python
import functools

import numpy as np
import jax
import jax.numpy as jnp
from jax import lax
from jax.experimental import pallas as pl
from jax.experimental.pallas import tpu as pltpu

F32 = jnp.float32
BF16 = jnp.bfloat16

CHUNK = 64
MLA_HEADS = 16
QK_NOPE = 128
QK_ROPE = 64
V_HEAD = 128
KV_LORA = 512
ROPE_THETA = 10000.0
CA_HEADS = 16
CA_HEAD_DIM = 128
LEFT_CHUNKS = 8
MAX_REL = 256
N_EXPERTS = 8
TOP_K = 2
N_MOD = 6
LN_EPS = 1e-5
RMS_EPS = 1e-6
DEPTH = 4
DEEPNORM_ALPHA = (2.0 * DEPTH) ** 0.25

LANES = 128
VMEM_LIMIT = 56 * 1024 * 1024

MLA_HEAD_PAD = 2 * LANES
MOE_SLOT_ROWS = 2560
MOE_ROW_BLOCK = 256
CA_QBLOCK = 256
CA_KBLOCKS = 3
ATTN_HEADS_PER_STEP = 4

_NT = (((1,), (1,)), ((), ()))


def _params(*sem):
    return pltpu.CompilerParams(dimension_semantics=sem, vmem_limit_bytes=VMEM_LIMIT)


def _silu(x):
    return x * jax.nn.sigmoid(x)


def _layer_norm(z, g, b):
    mu = jnp.mean(z, axis=-1, keepdims=True)
    zc = z - mu
    var = jnp.mean(zc * zc, axis=-1, keepdims=True)
    return zc * lax.rsqrt(var + LN_EPS) * g + b


def _modulate(h, mod_ref, k):
    return h * (1.0 + mod_ref[0, k + 1:k + 2, :]) + mod_ref[0, k:k + 1, :]


def _cond_kernel(c_ref, w_ref, b_ref, tab_ref, o_ref):
    s = _silu(c_ref[...]).astype(BF16)
    y = jnp.dot(s, w_ref[...].astype(BF16), preferred_element_type=F32) + b_ref[...]
    o_ref[...] = y[None, :, :] + tab_ref[...][:, None, :]


def _cond(c, ada_w, ada_b, ada_table):
    B, D = c.shape
    N = ada_w.shape[1]
    tn = 1024
    c8 = jnp.zeros((8, D), F32).at[:B].set(c)
    out = pl.pallas_call(
        _cond_kernel,
        grid=(N // tn,),
        in_specs=[
            pl.BlockSpec((8, D), lambda n: (0, 0)),
            pl.BlockSpec((D, tn), lambda n: (0, n)),
            pl.BlockSpec((1, tn), lambda n: (0, n)),
            pl.BlockSpec((DEPTH, tn), lambda n: (0, n)),
        ],
        out_specs=pl.BlockSpec((DEPTH, 8, tn), lambda n: (0, 0, n)),
        out_shape=jax.ShapeDtypeStruct((DEPTH, 8, N), F32),
        compiler_params=_params("arbitrary"),
        name="cond",
    )(c8, ada_w, ada_b.reshape(1, N), ada_table.reshape(DEPTH, N))
    mod = out[:, :B].reshape(DEPTH, B, N_MOD, D)
    return jnp.pad(mod, ((0, 0), (0, 0), (0, 8 - N_MOD), (0, 0)))


def _rope_lanes(x, cos, sin_signed):
    lane = lax.broadcasted_iota(jnp.int32, x.shape, 1)
    partner = jnp.where(lane < QK_ROPE // 2, pltpu.roll(x, LANES - QK_ROPE // 2, 1),
                        pltpu.roll(x, QK_ROPE // 2, 1))
    return x * cos + partner * sin_signed


def _mla_proj_kernel(h_ref, mod_ref, pos_ref, freq_ref, wdq_ref, qn_ref, wdkv_ref, kvn_ref, wkr_ref,
                     wuq_ref, wukv_ref, q_ref, k_ref, v_ref):
    u = _modulate(h_ref[...], mod_ref, 0).astype(BF16)
    ang = pos_ref[...].astype(F32) * freq_ref[...]
    cos = jnp.cos(ang)
    sin = jnp.sin(ang)
    lane = lax.broadcasted_iota(jnp.int32, ang.shape, 1)
    sin_signed = jnp.where(lane < QK_ROPE // 2, -sin, sin)

    def rms(x, g):
        return (x * lax.rsqrt(jnp.mean(x * x, axis=-1, keepdims=True) + RMS_EPS)) * g

    cq = rms(jnp.dot(u, wdq_ref[...], preferred_element_type=F32), qn_ref[...]).astype(BF16)
    ckv = rms(jnp.dot(u, wdkv_ref[...], preferred_element_type=F32), kvn_ref[...]).astype(BF16)
    kr = jnp.dot(u, wkr_ref[...], preferred_element_type=F32)
    kr = _rope_lanes(kr, cos, sin_signed).astype(BF16)
    for hd in range(MLA_HEADS):
        c0 = hd * MLA_HEAD_PAD
        qh = jnp.dot(cq, wuq_ref[:, c0:c0 + MLA_HEAD_PAD], preferred_element_type=F32)
        q_ref[:, c0:c0 + LANES] = qh[:, :LANES].astype(BF16)
        q_ref[:, c0 + LANES:c0 + MLA_HEAD_PAD] = _rope_lanes(qh[:, LANES:], cos, sin_signed).astype(BF16)
        kvh = jnp.dot(ckv, wukv_ref[:, c0:c0 + MLA_HEAD_PAD], preferred_element_type=F32)
        k_ref[:, c0:c0 + LANES] = kvh[:, :LANES].astype(BF16)
        k_ref[:, c0 + LANES:c0 + MLA_HEAD_PAD] = kr
        v_ref[:, hd * V_HEAD:(hd + 1) * V_HEAD] = kvh[:, LANES:].astype(BF16)


def _mla_proj(h, mod_l, pos_col, freq_row, w_dq, q_norm, w_dkv, kv_norm, w_uq, w_ukv, S):
    T, D = h.shape
    H = MLA_HEADS
    tm = 256
    q_lora = w_dq.shape[1]
    wdq = w_dq.astype(BF16)
    wdkv = w_dkv[:, :KV_LORA].astype(BF16)
    wkr = jnp.pad(w_dkv[:, KV_LORA:], ((0, 0), (0, LANES - QK_ROPE))).astype(BF16)
    wuq = jnp.pad(w_uq.reshape(q_lora, H, QK_NOPE + QK_ROPE),
                  ((0, 0), (0, 0), (0, MLA_HEAD_PAD - QK_NOPE - QK_ROPE))).reshape(q_lora, H * MLA_HEAD_PAD)
    wuq = wuq.astype(BF16)
    wukv = w_ukv.astype(BF16)
    const = lambda i: (0, 0)
    row = lambda i: (i, 0)
    return pl.pallas_call(
        _mla_proj_kernel,
        grid=(T // tm,),
        in_specs=[
            pl.BlockSpec((tm, D), row),
            pl.BlockSpec((1, 8, D), lambda i: (i * tm // S, 0, 0)),
            pl.BlockSpec((tm, 1), row),
            pl.BlockSpec((1, LANES), const),
            pl.BlockSpec(wdq.shape, const),
            pl.BlockSpec((1, q_lora), const),
            pl.BlockSpec(wdkv.shape, const),
            pl.BlockSpec((1, KV_LORA), const),
            pl.BlockSpec(wkr.shape, const),
            pl.BlockSpec(wuq.shape, const),
            pl.BlockSpec(wukv.shape, const),
        ],
        out_specs=[
            pl.BlockSpec((tm, H * MLA_HEAD_PAD), row),
            pl.BlockSpec((tm, H * MLA_HEAD_PAD), row),
            pl.BlockSpec((tm, H * V_HEAD), row),
        ],
        out_shape=[
            jax.ShapeDtypeStruct((T, H * MLA_HEAD_PAD), BF16),
            jax.ShapeDtypeStruct((T, H * MLA_HEAD_PAD), BF16),
            jax.ShapeDtypeStruct((T, H * V_HEAD), BF16),
        ],
        compiler_params=_params("arbitrary"),
        name="mla_proj",
    )(h, mod_l, pos_col, freq_row, wdq, q_norm.reshape(1, -1), wdkv, kv_norm.reshape(1, -1), wkr, wuq, wukv)


def _with_ones(v):
    return jnp.concatenate([v, jnp.ones_like(v)], axis=1)


def _mla_attn_kernel(pi_ref, pj_ref, q_ref, k_ref, v_ref, o_ref, m_sc, acc_sc, *, scale):
    p = pl.program_id(2)
    i = pi_ref[p]
    j = pj_ref[p]

    @pl.when(j == 0)
    def _():
        m_sc[...] = jnp.full(m_sc.shape, -jnp.inf, F32)
        acc_sc[...] = jnp.zeros(acc_sc.shape, F32)

    def accumulate(diagonal):
        for hd in range(ATTN_HEADS_PER_STEP):
            qk = slice(hd * MLA_HEAD_PAD, (hd + 1) * MLA_HEAD_PAD)
            s = lax.dot_general(q_ref[:, qk], k_ref[:, qk], _NT, preferred_element_type=F32) * scale
            if diagonal:
                qc = lax.broadcasted_iota(jnp.int32, s.shape, 0) // CHUNK
                kc = lax.broadcasted_iota(jnp.int32, s.shape, 1) // CHUNK
                s = jnp.where(kc <= qc, s, -jnp.inf)
            m_prev = m_sc[hd]
            m_new = jnp.maximum(m_prev, jnp.max(s, axis=-1, keepdims=True))
            a = jnp.exp(m_prev - m_new)
            e = jnp.exp(s - m_new).astype(BF16)
            v = _with_ones(v_ref[:, hd * V_HEAD:(hd + 1) * V_HEAD])
            acc_sc[hd] = a * acc_sc[hd] + jnp.dot(e, v, preferred_element_type=F32)
            m_sc[hd] = m_new

    @pl.when(j < i)
    def _():
        accumulate(False)

    @pl.when(j == i)
    def _():
        accumulate(True)
        for hd in range(ATTN_HEADS_PER_STEP):
            acc = acc_sc[hd]
            o_ref[:, hd * V_HEAD:(hd + 1) * V_HEAD] = (acc[:, :V_HEAD] / acc[:, V_HEAD:]).astype(o_ref.dtype)


def _mla_attn(q, k, v, B, S):
    H = MLA_HEADS
    hp = ATTN_HEADS_PER_STEP
    tq = 512
    nq = S // tq
    scale = (QK_NOPE + QK_ROPE) ** -0.5
    pairs = [(i, j) for i in range(nq) for j in range(i + 1)]
    pi = jnp.asarray([p[0] for p in pairs], jnp.int32)
    pj = jnp.asarray([p[1] for p in pairs], jnp.int32)
    return pl.pallas_call(
        functools.partial(_mla_attn_kernel, scale=scale),
        grid_spec=pltpu.PrefetchScalarGridSpec(
            num_scalar_prefetch=2,
            grid=(B, H // hp, len(pairs)),
            in_specs=[
                pl.BlockSpec((tq, hp * MLA_HEAD_PAD), lambda b, g, p, pi, pj: (b * nq + pi[p], g)),
                pl.BlockSpec((tq, hp * MLA_HEAD_PAD), lambda b, g, p, pi, pj: (b * nq + pj[p], g)),
                pl.BlockSpec((tq, hp * V_HEAD), lambda b, g, p, pi, pj: (b * nq + pj[p], g)),
            ],
            out_specs=pl.BlockSpec((tq, hp * V_HEAD), lambda b, g, p, pi, pj: (b * nq + pi[p], g)),
            scratch_shapes=[pltpu.VMEM((hp, tq, 1), F32), pltpu.VMEM((hp, tq, 2 * V_HEAD), F32)],
        ),
        out_shape=jax.ShapeDtypeStruct((B * S, H * V_HEAD), BF16),
        compiler_params=_params("arbitrary", "arbitrary", "arbitrary"),
        name="mla_attn",
    )(pi, pj, q, k, v)


def _proj_ln_kernel(o_ref, w_ref, h_ref, mod_ref, g_ref, b_ref, out_ref, *, gate_row):
    y = jnp.dot(o_ref[...], w_ref[...], preferred_element_type=F32)
    z = DEEPNORM_ALPHA * h_ref[...] + (1.0 + mod_ref[0, gate_row:gate_row + 1, :]) * y
    out_ref[...] = _layer_norm(z, g_ref[...], b_ref[...])


def _proj_ln(o, w_o, h, mod_l, ln_g, ln_b, S):
    T, D = h.shape
    tm = 512
    w = w_o.astype(BF16)
    row = lambda i: (i, 0)
    const = lambda i: (0, 0)
    return pl.pallas_call(
        functools.partial(_proj_ln_kernel, gate_row=2),
        grid=(T // tm,),
        in_specs=[
            pl.BlockSpec((tm, o.shape[1]), row),
            pl.BlockSpec(w.shape, const),
            pl.BlockSpec((tm, D), row),
            pl.BlockSpec((1, 8, D), lambda i: (i * tm // S, 0, 0)),
            pl.BlockSpec((1, D), const),
            pl.BlockSpec((1, D), const),
        ],
        out_specs=pl.BlockSpec((tm, D), row),
        out_shape=jax.ShapeDtypeStruct((T, D), F32),
        compiler_params=_params("arbitrary"),
        name="proj_ln",
    )(o, w, h, mod_l, ln_g.reshape(1, D), ln_b.reshape(1, D))


def _ffn_kernel(h_ref, mod_ref, wg_ref, wu_ref, wd_ref, g_ref, b_ref, out_ref, u_sc):
    f = pl.program_id(1)

    @pl.when(f == 0)
    def _():
        u_sc[...] = _modulate(h_ref[...], mod_ref, 3).astype(BF16)
        out_ref[...] = jnp.zeros(out_ref.shape, F32)

    u = u_sc[...]
    g = jnp.dot(u, wg_ref[...].astype(BF16), preferred_element_type=F32)
    up = jnp.dot(u, wu_ref[...].astype(BF16), preferred_element_type=F32)
    hm = (_silu(g) * up).astype(BF16)
    out_ref[...] += jnp.dot(hm, wd_ref[...].astype(BF16), preferred_element_type=F32)

    @pl.when(f == pl.num_programs(1) - 1)
    def _():
        z = DEEPNORM_ALPHA * h_ref[...] + (1.0 + mod_ref[0, 5:6, :]) * out_ref[...]
        out_ref[...] = _layer_norm(z, g_ref[...], b_ref[...])


def _ffn(h, mod_l, w_gate, w_up, w_down, layer, ln_g, ln_b, S):
    T, D = h.shape
    F = w_gate.shape[2]
    tm = min(1024, T)
    tf = 256
    row = lambda i, f: (i, 0)
    const = lambda i, f: (0, 0)
    return pl.pallas_call(
        _ffn_kernel,
        grid=(T // tm, F // tf),
        in_specs=[
            pl.BlockSpec((tm, D), row, pipeline_mode=pl.Buffered(1)),
            pl.BlockSpec((1, 8, D), lambda i, f: (i * tm // S, 0, 0)),
            pl.BlockSpec((None, D, tf), lambda i, f: (layer, 0, f)),
            pl.BlockSpec((None, D, tf), lambda i, f: (layer, 0, f)),
            pl.BlockSpec((None, tf, D), lambda i, f: (layer, f, 0)),
            pl.BlockSpec((1, D), const),
            pl.BlockSpec((1, D), const),
        ],
        out_specs=pl.BlockSpec((tm, D), row),
        out_shape=jax.ShapeDtypeStruct((T, D), F32),
        scratch_shapes=[pltpu.VMEM((tm, D), BF16)],
        compiler_params=_params("arbitrary", "arbitrary"),
        name="ffn",
    )(h, mod_l, w_gate, w_up, w_down, ln_g.reshape(1, D), ln_b.reshape(1, D))


def _qkv_kernel(h_ref, mod_ref, w_ref, b_ref, o_ref, u_sc):
    @pl.when(pl.program_id(1) == 0)
    def _():
        u_sc[...] = _modulate(h_ref[...], mod_ref, 0).astype(BF16)

    y = jnp.dot(u_sc[...], w_ref[...].astype(BF16), preferred_element_type=F32) + b_ref[...]
    o_ref[...] = y.astype(o_ref.dtype)


def _qkv(h, mod_l, w_qkv, layer, b_qkv, S):
    T, D = h.shape
    N = w_qkv.shape[2]
    tm = min(1024, T)
    tn = 512
    return pl.pallas_call(
        _qkv_kernel,
        grid=(T // tm, N // tn),
        in_specs=[
            pl.BlockSpec((tm, D), lambda i, n: (i, 0), pipeline_mode=pl.Buffered(1)),
            pl.BlockSpec((1, 8, D), lambda i, n: (i * tm // S, 0, 0)),
            pl.BlockSpec((None, D, tn), lambda i, n: (layer, 0, n)),
            pl.BlockSpec((1, tn), lambda i, n: (0, n)),
        ],
        out_specs=pl.BlockSpec((tm, tn), lambda i, n: (i, n)),
        out_shape=jax.ShapeDtypeStruct((T, N), BF16),
        scratch_shapes=[pltpu.VMEM((tm, D), BF16)],
        compiler_params=_params("arbitrary", "arbitrary"),
        name="qkv",
    )(h, mod_l, w_qkv, b_qkv.reshape(1, N))


def _ca_attn_kernel(q_ref, k0_ref, k1_ref, k2_ref, v0_ref, v1_ref, v2_ref, bias_ref, o_ref, *, scale):
    i = pl.program_id(2)
    tq = q_ref.shape[0]
    k_refs = (k0_ref, k1_ref, k2_ref)
    v_refs = (v0_ref, v1_ref, v2_ref)
    for hd in range(ATTN_HEADS_PER_STEP):
        cols = slice(hd * CA_HEAD_DIM, (hd + 1) * CA_HEAD_DIM)
        q = q_ref[:, cols]
        s = []
        for p in range(CA_KBLOCKS):
            sp = lax.dot_general(q, k_refs[p][:, cols], _NT, preferred_element_type=F32) * scale
            sp = sp + bias_ref[hd, :, p * tq:(p + 1) * tq]
            first = CA_KBLOCKS - 1 - p
            if first > 0:
                sp = jnp.where(i >= first, sp, -jnp.inf)
            s.append(sp)
        m = functools.reduce(jnp.maximum, [jnp.max(sp, axis=-1, keepdims=True) for sp in s])
        acc = functools.reduce(jnp.add, [
            jnp.dot(jnp.exp(sp - m).astype(BF16), _with_ones(v_refs[p][:, cols]), preferred_element_type=F32)
            for p, sp in enumerate(s)])
        o_ref[:, cols] = (acc[:, :CA_HEAD_DIM] / acc[:, CA_HEAD_DIM:]).astype(o_ref.dtype)


def _ca_bias_table(rel_bias):
    H = rel_bias.shape[0]
    tq = CA_QBLOCK
    nk = CA_KBLOCKS * tq
    L = tq + nk - 1
    d = np.arange(L) - (tq - 1)
    rel_idx = np.clip((nk - tq) - d, -MAX_REL, MAX_REL) + MAX_REL
    wpad = jnp.pad(rel_bias[:, rel_idx].astype(F32), ((0, 0), (0, 1)))
    skew = jnp.tile(wpad, (1, tq))[:, :tq * L].reshape(H, tq, L)
    table = skew[:, :, tq - 1:tq - 1 + nk]
    r = np.arange(tq)[:, None]
    c = np.arange(nk)[None, :]
    qc = r // CHUNK + (nk - tq) // CHUNK
    kc = c // CHUNK
    band = (kc <= qc) & (kc >= qc - LEFT_CHUNKS)
    return jnp.where(band[None], table, -jnp.inf)


def _ca_attn(qkv, rel_bias, B, S):
    H, Dh = CA_HEADS, CA_HEAD_DIM
    hp = ATTN_HEADS_PER_STEP
    ng = H // hp
    tq = CA_QBLOCK
    nq = S // tq
    bias = _ca_bias_table(rel_bias)

    def kv_spec(which, p):
        back = CA_KBLOCKS - 1 - p
        return pl.BlockSpec((tq, hp * Dh), lambda b, g, i: (b * nq + jnp.maximum(i - back, 0), which * ng + g))

    return pl.pallas_call(
        functools.partial(_ca_attn_kernel, scale=Dh ** -0.5),
        grid=(B, ng, nq),
        in_specs=[pl.BlockSpec((tq, hp * Dh), lambda b, g, i: (b * nq + i, g))]
        + [kv_spec(1, p) for p in range(CA_KBLOCKS)]
        + [kv_spec(2, p) for p in range(CA_KBLOCKS)]
        + [pl.BlockSpec((hp, tq, CA_KBLOCKS * tq), lambda b, g, i: (g, 0, 0))],
        out_specs=pl.BlockSpec((tq, hp * Dh), lambda b, g, i: (b * nq + i, g)),
        out_shape=jax.ShapeDtypeStruct((B * S, H * Dh), BF16),
        compiler_params=_params("arbitrary", "arbitrary", "arbitrary"),
        name="ca_attn",
    )(qkv, qkv, qkv, qkv, qkv, qkv, qkv, bias)


def _router_kernel(h_ref, mod_ref, rw_ref, rb_ref, u_ref, idx_ref, gate_ref, rank_ref, cnt_ref, carry_sc):
    t = pl.program_id(0)

    @pl.when(t == 0)
    def _():
        carry_sc[...] = jnp.zeros(carry_sc.shape, F32)

    u = _modulate(h_ref[...], mod_ref, 3)
    u_ref[...] = u
    tb = u.shape[0]
    logits = lax.dot_general(rw_ref[...], u, _NT, precision=lax.Precision.HIGHEST,
                             preferred_element_type=F32) + rb_ref[...]
    eid = lax.broadcasted_iota(jnp.int32, logits.shape, 0)
    m1 = jnp.max(logits, axis=0, keepdims=True)
    i1 = jnp.min(jnp.where(logits == m1, eid, N_EXPERTS), axis=0, keepdims=True)
    rest = jnp.where(eid == i1, -jnp.inf, logits)
    m2 = jnp.max(rest, axis=0, keepdims=True)
    i2 = jnp.min(jnp.where(rest == m2, eid, N_EXPERTS), axis=0, keepdims=True)
    e2 = jnp.exp(m2 - m1)
    den = 1.0 + e2
    idx_ref[...] = jnp.concatenate([i1, i2], axis=0)
    gate_ref[...] = jnp.concatenate([1.0 / den, e2 / den], axis=0)
    onehot = jnp.where((eid == i1) | (eid == i2), 1.0, 0.0)
    tri = jnp.where(lax.broadcasted_iota(jnp.int32, (tb, tb), 0) <= lax.broadcasted_iota(jnp.int32, (tb, tb), 1),
                    1.0, 0.0).astype(BF16)
    incl = jnp.dot(onehot.astype(BF16), tri, preferred_element_type=F32)
    excl = incl - onehot + carry_sc[:, 0:1]
    r1 = jnp.sum(jnp.where(eid == i1, excl, 0.0), axis=0, keepdims=True)
    r2 = jnp.sum(jnp.where(eid == i2, excl, 0.0), axis=0, keepdims=True)
    rank_ref[...] = jnp.concatenate([r1, r2], axis=0).astype(jnp.int32)
    carry_sc[...] = carry_sc[...] + jnp.sum(onehot, axis=1, keepdims=True)
    cnt_ref[...] = carry_sc[...].astype(jnp.int32)


def _router(h, mod_l, router_w, router_b, S):
    T, D = h.shape
    E = N_EXPERTS
    tb = 512
    return pl.pallas_call(
        _router_kernel,
        grid=(T // tb,),
        in_specs=[
            pl.BlockSpec((tb, D), lambda t: (t, 0)),
            pl.BlockSpec((1, 8, D), lambda t: (t * tb // S, 0, 0)),
            pl.BlockSpec((E, D), lambda t: (0, 0)),
            pl.BlockSpec((E, 1), lambda t: (0, 0)),
        ],
        out_specs=[
            pl.BlockSpec((tb, D), lambda t: (t, 0)),
            pl.BlockSpec((TOP_K, tb), lambda t: (0, t)),
            pl.BlockSpec((TOP_K, tb), lambda t: (0, t)),
            pl.BlockSpec((TOP_K, tb), lambda t: (0, t)),
            pl.BlockSpec((E, LANES), lambda t: (0, 0)),
        ],
        out_shape=[
            jax.ShapeDtypeStruct((T, D), F32),
            jax.ShapeDtypeStruct((TOP_K, T), jnp.int32),
            jax.ShapeDtypeStruct((TOP_K, T), F32),
            jax.ShapeDtypeStruct((TOP_K, T), jnp.int32),
            jax.ShapeDtypeStruct((E, LANES), jnp.int32),
        ],
        scratch_shapes=[pltpu.VMEM((E, LANES), F32)],
        compiler_params=_params("arbitrary"),
        name="router",
    )(h, mod_l, router_w.T, router_b.reshape(E, 1))


def _row_gather_start(src_hbm, dst, sem, rows_ref, base, n):
    def issue(j, carry):
        r = rows_ref[base + j]
        pltpu.make_async_copy(src_hbm.at[pl.ds(r, 1), :], dst.at[pl.ds(j, 1), :], sem).start()
        return carry
    lax.fori_loop(0, n, issue, 0)


def _row_gather_wait(src_hbm, dst, sem, n):
    pltpu.make_async_copy(src_hbm.at[pl.ds(0, n), :], dst, sem).wait()


def _dispatch_kernel(row_tok_ref, blk_ref, nv_ref, u_hbm, o_ref, buf, sem):
    c = pl.program_id(0)
    nv = nv_ref[0]
    rb = MOE_ROW_BLOCK
    slot = c % 2

    def start(step, sl):
        _row_gather_start(u_hbm, buf.at[sl], sem.at[sl], row_tok_ref, blk_ref[step] * rb, rb)

    @pl.when((c == 0) & (nv > 0))
    def _():
        start(0, 0)

    @pl.when(c + 1 < nv)
    def _():
        start(c + 1, 1 - slot)

    @pl.when(c < nv)
    def _():
        _row_gather_wait(u_hbm, buf.at[slot], sem.at[slot], rb)
        o_ref[...] = buf[slot].astype(BF16)

    @pl.when(c >= nv)
    def _():
        o_ref[...] = jnp.zeros(o_ref.shape, BF16)


def _dispatch(u, row_tok, blk_ids, n_valid, n_rows):
    T, D = u.shape
    rb = MOE_ROW_BLOCK
    return pl.pallas_call(
        _dispatch_kernel,
        grid_spec=pltpu.PrefetchScalarGridSpec(
            num_scalar_prefetch=3,
            grid=(blk_ids.shape[0],),
            in_specs=[pl.BlockSpec(memory_space=pl.ANY)],
            out_specs=pl.BlockSpec((rb, D), lambda c, rt, blk, nv: (blk[c], 0)),
            scratch_shapes=[pltpu.VMEM((2, rb, D), F32), pltpu.SemaphoreType.DMA((2,))],
        ),
        out_shape=jax.ShapeDtypeStruct((n_rows + rb, D), BF16),
        compiler_params=_params("arbitrary"),
        name="moe_dispatch",
    )(row_tok, blk_ids, n_valid, u)


def _moe_ffn_kernel(se_ref, nblk_ref, x_ref, wg_ref, wu_ref, wd_ref, y_hbm, acc, wg_sc, wu_sc, wd_sc, sem):
    s = pl.program_id(0)
    f = pl.program_id(1)
    nb = nblk_ref[s]
    rb = MOE_ROW_BLOCK

    @pl.when(nb > 0)
    def _():
        @pl.when(f == 0)
        def _():
            acc[...] = jnp.zeros(acc.shape, F32)

        wg_sc[...] = wg_ref[...].astype(BF16)
        wu_sc[...] = wu_ref[...].astype(BF16)
        wd_sc[...] = wd_ref[...].astype(BF16)

        def rows_at(start, size):
            rows = pl.ds(pl.multiple_of(start, rb), size)
            x = x_ref[rows, :]
            g = jnp.dot(x, wg_sc[...], preferred_element_type=F32)
            up = jnp.dot(x, wu_sc[...], preferred_element_type=F32)
            hm = (_silu(g) * up).astype(BF16)
            acc[rows, :] += jnp.dot(hm, wd_sc[...], preferred_element_type=F32)

        def pair(r, carry):
            rows_at(r * (2 * rb), 2 * rb)
            return carry

        lax.fori_loop(0, nb // 2, pair, 0)

        @pl.when(nb % 2 == 1)
        def _():
            rows_at((nb - 1) * rb, rb)

        @pl.when(f == pl.num_programs(1) - 1)
        def _():
            def out_copy(r):
                return pltpu.make_async_copy(acc.at[pl.ds(pl.multiple_of(r * rb, rb), rb), :],
                                             y_hbm.at[pl.ds(pl.multiple_of(s * MOE_SLOT_ROWS + r * rb, rb), rb), :],
                                             sem.at[0])

            def start(r, carry):
                out_copy(r).start()
                return carry

            def wait(r, carry):
                out_copy(r).wait()
                return carry

            lax.fori_loop(0, nb, start, 0)
            lax.fori_loop(0, nb, wait, 0)


def _moe_ffn(xs, slot_e, nblk, w_gate, w_up, w_down, layer):
    D = xs.shape[1]
    F = w_gate.shape[3]
    R = MOE_SLOT_ROWS
    n_slots = slot_e.shape[0]
    tf = 256
    nf = F // tf

    def f_eff(s, f, nb):
        return jnp.where(nb[s] > 0, f, nf - 1)

    return pl.pallas_call(
        _moe_ffn_kernel,
        grid_spec=pltpu.PrefetchScalarGridSpec(
            num_scalar_prefetch=2,
            grid=(n_slots, nf),
            in_specs=[
                pl.BlockSpec((R, D), lambda s, f, se, nb: (s, 0), pipeline_mode=pl.Buffered(1)),
                pl.BlockSpec((None, None, D, tf), lambda s, f, se, nb: (layer, se[s], 0, f_eff(s, f, nb))),
                pl.BlockSpec((None, None, D, tf), lambda s, f, se, nb: (layer, se[s], 0, f_eff(s, f, nb))),
                pl.BlockSpec((None, None, tf, D), lambda s, f, se, nb: (layer, se[s], f_eff(s, f, nb), 0)),
            ],
            out_specs=pl.BlockSpec(memory_space=pl.ANY),
            scratch_shapes=[pltpu.VMEM((R, D), F32), pltpu.VMEM((D, tf), BF16), pltpu.VMEM((D, tf), BF16),
                            pltpu.VMEM((tf, D), BF16), pltpu.SemaphoreType.DMA((1,))],
        ),
        out_shape=jax.ShapeDtypeStruct((n_slots * R, D), F32),
        compiler_params=_params("arbitrary", "arbitrary"),
        name="moe_ffn",
    )(slot_e, nblk, xs, w_gate, w_up, w_down)


def _combine_kernel(dest_ref, y_hbm, gate_ref, h_ref, mod_ref, g_ref, b_ref, out_ref, buf0, buf1, sem, *, n_tok):
    i = pl.program_id(0)
    tb = out_ref.shape[0]
    slot = i % 2

    def start(step, sl):
        _row_gather_start(y_hbm, buf0.at[sl], sem.at[0, sl], dest_ref, step * tb, tb)
        _row_gather_start(y_hbm, buf1.at[sl], sem.at[1, sl], dest_ref, n_tok + step * tb, tb)

    @pl.when(i == 0)
    def _():
        start(0, 0)

    @pl.when(i + 1 < pl.num_programs(0))
    def _():
        start(i + 1, 1 - slot)

    _row_gather_wait(y_hbm, buf0.at[slot], sem.at[0, slot], tb)
    _row_gather_wait(y_hbm, buf1.at[slot], sem.at[1, slot], tb)
    y = gate_ref[:, 0:1] * buf0[slot] + gate_ref[:, 1:2] * buf1[slot]
    z = DEEPNORM_ALPHA * h_ref[...] + (1.0 + mod_ref[0, 5:6, :]) * y
    out_ref[...] = _layer_norm(z, g_ref[...], b_ref[...])


def _combine(y, dest, gates_tk, h, mod_l, ln_g, ln_b, S):
    T, D = h.shape
    tb = 256
    return pl.pallas_call(
        functools.partial(_combine_kernel, n_tok=T),
        grid_spec=pltpu.PrefetchScalarGridSpec(
            num_scalar_prefetch=1,
            grid=(T // tb,),
            in_specs=[
                pl.BlockSpec(memory_space=pl.ANY),
                pl.BlockSpec((tb, TOP_K), lambda i, d: (i, 0)),
                pl.BlockSpec((tb, D), lambda i, d: (i, 0)),
                pl.BlockSpec((1, 8, D), lambda i, d: (i * tb // S, 0, 0)),
                pl.BlockSpec((1, D), lambda i, d: (0, 0)),
                pl.BlockSpec((1, D), lambda i, d: (0, 0)),
            ],
            out_specs=pl.BlockSpec((tb, D), lambda i, d: (i, 0)),
            scratch_shapes=[pltpu.VMEM((2, tb, D), F32), pltpu.VMEM((2, tb, D), F32),
                            pltpu.SemaphoreType.DMA((2, 2))],
        ),
        out_shape=jax.ShapeDtypeStruct((T, D), F32),
        compiler_params=_params("arbitrary"),
        name="moe_combine",
    )(dest, y, gates_tk, h, mod_l, ln_g.reshape(1, D), ln_b.reshape(1, D))


def _moe_layer(h, mod_l, router_w, router_b, w_gate, w_up, w_down, layer, ln_g, ln_b, S):
    T, D = h.shape
    E, R, rb = N_EXPERTS, MOE_SLOT_ROWS, MOE_ROW_BLOCK
    n_slots = -(-T * TOP_K // R) + E
    n_rows = n_slots * R
    per_slot = R // rb
    max_blocks = T * TOP_K // rb + E
    u, idx, gates, rank, cnt = _router(h, mod_l, router_w, router_b, S)
    cnt = cnt[:, 0]
    slots_e = (cnt + R - 1) // R
    slot_end = jnp.cumsum(slots_e)
    slot_start = slot_end - slots_e
    dest = (slot_start * R)[idx] + rank
    tok = jnp.tile(jnp.arange(T, dtype=jnp.int32), TOP_K)
    row_tok = jnp.zeros((n_rows,), jnp.int32).at[dest.reshape(-1)].set(tok)
    sid = jnp.arange(n_slots, dtype=jnp.int32)
    n_used = slot_end[-1]
    last_used = jnp.maximum(n_used - 1, 0)
    se = jnp.minimum(jnp.searchsorted(slot_end, jnp.minimum(sid, last_used), side='right'), E - 1).astype(jnp.int32)
    rows_in = jnp.clip(cnt[se] - (sid - slot_start[se]) * R, 0, R)
    rows_in = jnp.where(sid < n_used, rows_in, 0)
    nblk = ((rows_in + rb - 1) // rb).astype(jnp.int32)
    blk = jnp.arange(n_rows // rb, dtype=jnp.int32)
    blk_valid = (blk % per_slot) < nblk[blk // per_slot]
    order = jnp.argsort(jnp.logical_not(blk_valid), stable=True).astype(jnp.int32)
    n_valid = jnp.sum(blk_valid.astype(jnp.int32))
    blk_ids = jnp.where(jnp.arange(max_blocks) < n_valid, order[:max_blocks], n_rows // rb).astype(jnp.int32)
    xs = _dispatch(u, row_tok, blk_ids, n_valid.reshape(1), n_rows)
    y = _moe_ffn(xs, se, nblk, w_gate, w_up, w_down, layer)
    return _combine(y, dest.reshape(-1).astype(jnp.int32), gates.T, h, mod_l, ln_g, ln_b, S)


def kernel(x, c, positions, ada_w, ada_b, ada_table, ln_g, ln_b, mla_w_dq, mla_q_norm, mla_w_uq, mla_w_dkv, mla_kv_norm, mla_w_ukv, mla_w_o, ca_w_qkv, ca_b_qkv, ca_rel_bias, ca_w_o, ffn_w_gate, ffn_w_up, ffn_w_down, moe_router_w, moe_router_b, moe_w_gate, moe_w_up, moe_w_down):
    B, S, D = x.shape
    T = B * S
    mod = _cond(c, ada_w, ada_b, ada_table)
    half = QK_ROPE // 2
    inv_freq = ROPE_THETA ** (-jnp.arange(half, dtype=F32) / half)
    freq_row = jnp.concatenate([inv_freq, inv_freq, jnp.zeros((LANES - QK_ROPE,), F32)]).reshape(1, LANES)
    pos_col = positions.reshape(T, 1).astype(jnp.int32)
    h = x.reshape(T, D)
    for i in range(DEPTH):
        j = i // 2
        mod_l = mod[i]
        if i % 2 == 0:
            q, k, v = _mla_proj(h, mod_l, pos_col, freq_row, mla_w_dq[j], mla_q_norm[j], mla_w_dkv[j],
                                mla_kv_norm[j], mla_w_uq[j], mla_w_ukv[j], S)
            o = _mla_attn(q, k, v, B, S)
            h = _proj_ln(o, mla_w_o[j], h, mod_l, ln_g[i, 0], ln_b[i, 0], S)
            h = _ffn(h, mod_l, ffn_w_gate, ffn_w_up, ffn_w_down, j, ln_g[i, 1], ln_b[i, 1], S)
        else:
            qkv = _qkv(h, mod_l, ca_w_qkv, j, ca_b_qkv[j], S)
            o = _ca_attn(qkv, ca_rel_bias[j], B, S)
            h = _proj_ln(o, ca_w_o[j], h, mod_l, ln_g[i, 0], ln_b[i, 0], S)
            h = _moe_layer(h, mod_l, moe_router_w[j], moe_router_b[j], moe_w_gate, moe_w_up,
                           moe_w_down, j, ln_g[i, 1], ln_b[i, 1], S)
    return h.reshape(B, S, D)
```

```python
import functools

import numpy as np
import jax
import jax.numpy as jnp
from jax import lax
from jax.experimental import pallas as pl
from jax.experimental.pallas import tpu as pltpu

F32 = jnp.float32
BF16 = jnp.bfloat16

CHUNK = 64
MLA_HEADS = 16
QK_NOPE = 128
QK_ROPE = 64
V_HEAD = 128
KV_LORA = 512
ROPE_THETA = 10000.0
CA_HEADS = 16
CA_HEAD_DIM = 128
LEFT_CHUNKS = 8
MAX_REL = 256
N_EXPERTS = 8
TOP_K = 2
N_MOD = 6
LN_EPS = 1e-5
RMS_EPS = 1e-6
DEPTH = 4
DEEPNORM_ALPHA = (2.0 * DEPTH) ** 0.25

LANES = 128
VMEM_LIMIT = 56 * 1024 * 1024

MLA_HEAD_PAD = 2 * LANES
MOE_SLOT_ROWS = 2560
MOE_ROW_BLOCK = 256
CA_QBLOCK = 256
CA_KBLOCKS = 3
ATTN_HEADS_PER_STEP = 4

_NT = (((1,), (1,)), ((), ()))


def _params(*sem):
    return pltpu.CompilerParams(dimension_semantics=sem, vmem_limit_bytes=VMEM_LIMIT)


def _silu(x):
    return x * jax.nn.sigmoid(x)


def _layer_norm(z, g, b):
    mu = jnp.mean(z, axis=-1, keepdims=True)
    zc = z - mu
    var = jnp.mean(zc * zc, axis=-1, keepdims=True)
    return zc * lax.rsqrt(var + LN_EPS) * g + b


def _modulate(h, mod_ref, k):
    return h * (1.0 + mod_ref[0, k + 1:k + 2, :]) + mod_ref[0, k:k + 1, :]


def _cond_kernel(c_ref, w_ref, b_ref, tab_ref, o_ref):
    s = _silu(c_ref[...]).astype(BF16)
    y = jnp.dot(s, w_ref[...].astype(BF16), preferred_element_type=F32) + b_ref[...]
    o_ref[...] = y[None, :, :] + tab_ref[...][:, None, :]


def _cond(c, ada_w, ada_b, ada_table):
    B, D = c.shape
    N = ada_w.shape[1]
    tn = 1024
    c8 = jnp.zeros((8, D), F32).at[:B].set(c)
    out = pl.pallas_call(
        _cond_kernel,
        grid=(N // tn,),
        in_specs=[
            pl.BlockSpec((8, D), lambda n: (0, 0)),
            pl.BlockSpec((D, tn), lambda n: (0, n)),
            pl.BlockSpec((1, tn), lambda n: (0, n)),
            pl.BlockSpec((DEPTH, tn), lambda n: (0, n)),
        ],
        out_specs=pl.BlockSpec((DEPTH, 8, tn), lambda n: (0, 0, n)),
        out_shape=jax.ShapeDtypeStruct((DEPTH, 8, N), F32),
        compiler_params=_params("arbitrary"),
        name="cond",
    )(c8, ada_w, ada_b.reshape(1, N), ada_table.reshape(DEPTH, N))
    mod = out[:, :B].reshape(DEPTH, B, N_MOD, D)
    return jnp.pad(mod, ((0, 0), (0, 0), (0, 8 - N_MOD), (0, 0)))


def _rope_lanes(x, cos, sin_signed):
    lane = lax.broadcasted_iota(jnp.int32, x.shape, 1)
    partner = jnp.where(lane < QK_ROPE // 2, pltpu.roll(x, LANES - QK_ROPE // 2, 1),
                        pltpu.roll(x, QK_ROPE // 2, 1))
    return x * cos + partner * sin_signed


def _mla_proj_kernel(h_ref, mod_ref, pos_ref, freq_ref, wdq_ref, qn_ref, wdkv_ref, kvn_ref, wkr_ref,
                     wuq_ref, wukv_ref, q_ref, k_ref, v_ref):
    u = _modulate(h_ref[...], mod_ref, 0).astype(BF16)
    ang = pos_ref[...].astype(F32) * freq_ref[...]
    cos = jnp.cos(ang)
    sin = jnp.sin(ang)
    lane = lax.broadcasted_iota(jnp.int32, ang.shape, 1)
    sin_signed = jnp.where(lane < QK_ROPE // 2, -sin, sin)

    def rms(x, g):
        return (x * lax.rsqrt(jnp.mean(x * x, axis=-1, keepdims=True) + RMS_EPS)) * g

    cq = rms(jnp.dot(u, wdq_ref[...], preferred_element_type=F32), qn_ref[...]).astype(BF16)
    ckv = rms(jnp.dot(u, wdkv_ref[...], preferred_element_type=F32), kvn_ref[...]).astype(BF16)
    kr = jnp.dot(u, wkr_ref[...], preferred_element_type=F32)
    kr = _rope_lanes(kr, cos, sin_signed).astype(BF16)
    for hd in range(MLA_HEADS):
        c0 = hd * MLA_HEAD_PAD
        qh = jnp.dot(cq, wuq_ref[:, c0:c0 + MLA_HEAD_PAD], preferred_element_type=F32)
        q_ref[:, c0:c0 + LANES] = qh[:, :LANES].astype(BF16)
        q_ref[:, c0 + LANES:c0 + MLA_HEAD_PAD] = _rope_lanes(qh[:, LANES:], cos, sin_signed).astype(BF16)
        kvh = jnp.dot(ckv, wukv_ref[:, c0:c0 + MLA_HEAD_PAD], preferred_element_type=F32)
        k_ref[:, c0:c0 + LANES] = kvh[:, :LANES].astype(BF16)
        k_ref[:, c0 + LANES:c0 + MLA_HEAD_PAD] = kr
        v_ref[:, hd * V_HEAD:(hd + 1) * V_HEAD] = kvh[:, LANES:].astype(BF16)


def _mla_proj(h, mod_l, pos_col, freq_row, w_dq, q_norm, w_dkv, kv_norm, w_uq, w_ukv, S):
    T, D = h.shape
    H = MLA_HEADS
    tm = 256
    q_lora = w_dq.shape[1]
    wdq = w_dq.astype(BF16)
    wdkv = w_dkv[:, :KV_LORA].astype(BF16)
    wkr = jnp.pad(w_dkv[:, KV_LORA:], ((0, 0), (0, LANES - QK_ROPE))).astype(BF16)
    wuq = jnp.pad(w_uq.reshape(q_lora, H, QK_NOPE + QK_ROPE),
                  ((0, 0), (0, 0), (0, MLA_HEAD_PAD - QK_NOPE - QK_ROPE))).reshape(q_lora, H * MLA_HEAD_PAD)
    wuq = wuq.astype(BF16)
    wukv = w_ukv.astype(BF16)
    const = lambda i: (0, 0)
    row = lambda i: (i, 0)
    return pl.pallas_call(
        _mla_proj_kernel,
        grid=(T // tm,),
        in_specs=[
            pl.BlockSpec((tm, D), row),
            pl.BlockSpec((1, 8, D), lambda i: (i * tm // S, 0, 0)),
            pl.BlockSpec((tm, 1), row),
            pl.BlockSpec((1, LANES), const),
            pl.BlockSpec(wdq.shape, const),
            pl.BlockSpec((1, q_lora), const),
            pl.BlockSpec(wdkv.shape, const),
            pl.BlockSpec((1, KV_LORA), const),
            pl.BlockSpec(wkr.shape, const),
            pl.BlockSpec(wuq.shape, const),
            pl.BlockSpec(wukv.shape, const),
        ],
        out_specs=[
            pl.BlockSpec((tm, H * MLA_HEAD_PAD), row),
            pl.BlockSpec((tm, H * MLA_HEAD_PAD), row),
            pl.BlockSpec((tm, H * V_HEAD), row),
        ],
        out_shape=[
            jax.ShapeDtypeStruct((T, H * MLA_HEAD_PAD), BF16),
            jax.ShapeDtypeStruct((T, H * MLA_HEAD_PAD), BF16),
            jax.ShapeDtypeStruct((T, H * V_HEAD), BF16),
        ],
        compiler_params=_params("arbitrary"),
        name="mla_proj",
    )(h, mod_l, pos_col, freq_row, wdq, q_norm.reshape(1, -1), wdkv, kv_norm.reshape(1, -1), wkr, wuq, wukv)


def _with_ones(v):
    return jnp.concatenate([v, jnp.ones_like(v)], axis=1)


def _mla_attn_kernel(pi_ref, pj_ref, q_ref, k_ref, v_ref, o_ref, m_sc, acc_sc, *, scale):
    p = pl.program_id(2)
    i = pi_ref[p]
    j = pj_ref[p]

    @pl.when(j == 0)
    def _():
        m_sc[...] = jnp.full(m_sc.shape, -jnp.inf, F32)
        acc_sc[...] = jnp.zeros(acc_sc.shape, F32)

    def accumulate(diagonal):
        for hd in range(ATTN_HEADS_PER_STEP):
            qk = slice(hd * MLA_HEAD_PAD, (hd + 1) * MLA_HEAD_PAD)
            s = lax.dot_general(q_ref[:, qk], k_ref[:, qk], _NT, preferred_element_type=F32) * scale
            if diagonal:
                qc = lax.broadcasted_iota(jnp.int32, s.shape, 0) // CHUNK
                kc = lax.broadcasted_iota(jnp.int32, s.shape, 1) // CHUNK
                s = jnp.where(kc <= qc, s, -jnp.inf)
            m_prev = m_sc[hd]
            m_new = jnp.maximum(m_prev, jnp.max(s, axis=-1, keepdims=True))
            a = jnp.exp(m_prev - m_new)
            e = jnp.exp(s - m_new).astype(BF16)
            v = _with_ones(v_ref[:, hd * V_HEAD:(hd + 1) * V_HEAD])
            acc_sc[hd] = a * acc_sc[hd] + jnp.dot(e, v, preferred_element_type=F32)
            m_sc[hd] = m_new

    @pl.when(j < i)
    def _():
        accumulate(False)

    @pl.when(j == i)
    def _():
        accumulate(True)
        for hd in range(ATTN_HEADS_PER_STEP):
            acc = acc_sc[hd]
            o_ref[:, hd * V_HEAD:(hd + 1) * V_HEAD] = (acc[:, :V_HEAD] / acc[:, V_HEAD:]).astype(o_ref.dtype)


def _mla_attn(q, k, v, B, S):
    H = MLA_HEADS
    hp = ATTN_HEADS_PER_STEP
    tq = 512
    nq = S // tq
    scale = (QK_NOPE + QK_ROPE) ** -0.5
    pairs = [(i, j) for i in range(nq) for j in range(i + 1)]
    pi = jnp.asarray([p[0] for p in pairs], jnp.int32)
    pj = jnp.asarray([p[1] for p in pairs], jnp.int32)
    return pl.pallas_call(
        functools.partial(_mla_attn_kernel, scale=scale),
        grid_spec=pltpu.PrefetchScalarGridSpec(
            num_scalar_prefetch=2,
            grid=(B, H // hp, len(pairs)),
            in_specs=[
                pl.BlockSpec((tq, hp * MLA_HEAD_PAD), lambda b, g, p, pi, pj: (b * nq + pi[p], g)),
                pl.BlockSpec((tq, hp * MLA_HEAD_PAD), lambda b, g, p, pi, pj: (b * nq + pj[p], g)),
                pl.BlockSpec((tq, hp * V_HEAD), lambda b, g, p, pi, pj: (b * nq + pj[p], g)),
            ],
            out_specs=pl.BlockSpec((tq, hp * V_HEAD), lambda b, g, p, pi, pj: (b * nq + pi[p], g)),
            scratch_shapes=[pltpu.VMEM((hp, tq, 1), F32), pltpu.VMEM((hp, tq, 2 * V_HEAD), F32)],
        ),
        out_shape=jax.ShapeDtypeStruct((B * S, H * V_HEAD), BF16),
        compiler_params=_params("arbitrary", "arbitrary", "arbitrary"),
        name="mla_attn",
    )(pi, pj, q, k, v)


def _proj_ln_kernel(o_ref, w_ref, h_ref, mod_ref, g_ref, b_ref, out_ref, *, gate_row):
    y = jnp.dot(o_ref[...], w_ref[...], preferred_element_type=F32)
    z = DEEPNORM_ALPHA * h_ref[...] + (1.0 + mod_ref[0, gate_row:gate_row + 1, :]) * y
    out_ref[...] = _layer_norm(z, g_ref[...], b_ref[...])


def _proj_ln(o, w_o, h, mod_l, ln_g, ln_b, S):
    T, D = h.shape
    tm = 512
    w = w_o.astype(BF16)
    row = lambda i: (i, 0)
    const = lambda i: (0, 0)
    return pl.pallas_call(
        functools.partial(_proj_ln_kernel, gate_row=2),
        grid=(T // tm,),
        in_specs=[
            pl.BlockSpec((tm, o.shape[1]), row),
            pl.BlockSpec(w.shape, const),
            pl.BlockSpec((tm, D), row),
            pl.BlockSpec((1, 8, D), lambda i: (i * tm // S, 0, 0)),
            pl.BlockSpec((1, D), const),
            pl.BlockSpec((1, D), const),
        ],
        out_specs=pl.BlockSpec((tm, D), row),
        out_shape=jax.ShapeDtypeStruct((T, D), F32),
        compiler_params=_params("arbitrary"),
        name="proj_ln",
    )(o, w, h, mod_l, ln_g.reshape(1, D), ln_b.reshape(1, D))


def _ffn_kernel(h_ref, mod_ref, wg_ref, wu_ref, wd_ref, g_ref, b_ref, out_ref, u_sc):
    f = pl.program_id(1)

    @pl.when(f == 0)
    def _():
        u_sc[...] = _modulate(h_ref[...], mod_ref, 3).astype(BF16)
        out_ref[...] = jnp.zeros(out_ref.shape, F32)

    u = u_sc[...]
    g = jnp.dot(u, wg_ref[...].astype(BF16), preferred_element_type=F32)
    up = jnp.dot(u, wu_ref[...].astype(BF16), preferred_element_type=F32)
    hm = (_silu(g) * up).astype(BF16)
    out_ref[...] += jnp.dot(hm, wd_ref[...].astype(BF16), preferred_element_type=F32)

    @pl.when(f == pl.num_programs(1) - 1)
    def _():
        z = DEEPNORM_ALPHA * h_ref[...] + (1.0 + mod_ref[0, 5:6, :]) * out_ref[...]
        out_ref[...] = _layer_norm(z, g_ref[...], b_ref[...])


def _ffn(h, mod_l, w_gate, w_up, w_down, layer, ln_g, ln_b, S):
    T, D = h.shape
    F = w_gate.shape[2]
    tm = min(1024, T)
    tf = 256
    row = lambda i, f: (i, 0)
    const = lambda i, f: (0, 0)
    return pl.pallas_call(
        _ffn_kernel,
        grid=(T // tm, F // tf),
        in_specs=[
            pl.BlockSpec((tm, D), row, pipeline_mode=pl.Buffered(1)),
            pl.BlockSpec((1, 8, D), lambda i, f: (i * tm // S, 0, 0)),
            pl.BlockSpec((None, D, tf), lambda i, f: (layer, 0, f)),
            pl.BlockSpec((None, D, tf), lambda i, f: (layer, 0, f)),
            pl.BlockSpec((None, tf, D), lambda i, f: (layer, f, 0)),
            pl.BlockSpec((1, D), const),
            pl.BlockSpec((1, D), const),
        ],
        out_specs=pl.BlockSpec((tm, D), row),
        out_shape=jax.ShapeDtypeStruct((T, D), F32),
        scratch_shapes=[pltpu.VMEM((tm, D), BF16)],
        compiler_params=_params("arbitrary", "arbitrary"),
        name="ffn",
    )(h, mod_l, w_gate, w_up, w_down, ln_g.reshape(1, D), ln_b.reshape(1, D))


def _qkv_kernel(h_ref, mod_ref, w_ref, b_ref, o_ref, u_sc):
    @pl.when(pl.program_id(1) == 0)
    def _():
        u_sc[...] = _modulate(h_ref[...], mod_ref, 0).astype(BF16)

    y = jnp.dot(u_sc[...], w_ref[...].astype(BF16), preferred_element_type=F32) + b_ref[...]
    o_ref[...] = y.astype(o_ref.dtype)


def _qkv(h, mod_l, w_qkv, layer, b_qkv, S):
    T, D = h.shape
    N = w_qkv.shape[2]
    tm = min(1024, T)
    tn = 512
    return pl.pallas_call(
        _qkv_kernel,
        grid=(T // tm, N // tn),
        in_specs=[
            pl.BlockSpec((tm, D), lambda i, n: (i, 0), pipeline_mode=pl.Buffered(1)),
            pl.BlockSpec((1, 8, D), lambda i, n: (i * tm // S, 0, 0)),
            pl.BlockSpec((None, D, tn), lambda i, n: (layer, 0, n)),
            pl.BlockSpec((1, tn), lambda i, n: (0, n)),
        ],
        out_specs=pl.BlockSpec((tm, tn), lambda i, n: (i, n)),
        out_shape=jax.ShapeDtypeStruct((T, N), BF16),
        scratch_shapes=[pltpu.VMEM((tm, D), BF16)],
        compiler_params=_params("arbitrary", "arbitrary"),
        name="qkv",
    )(h, mod_l, w_qkv, b_qkv.reshape(1, N))


def _ca_attn_kernel(q_ref, k0_ref, k1_ref, k2_ref, v0_ref, v1_ref, v2_ref, bias_ref, o_ref, *, scale):
    i = pl.program_id(2)
    tq = q_ref.shape[0]
    k_refs = (k0_ref, k1_ref, k2_ref)
    v_refs = (v0_ref, v1_ref, v2_ref)
    for hd in range(ATTN_HEADS_PER_STEP):
        cols = slice(hd * CA_HEAD_DIM, (hd + 1) * CA_HEAD_DIM)
        q = q_ref[:, cols]
        s = []
        for p in range(CA_KBLOCKS):
            sp = lax.dot_general(q, k_refs[p][:, cols], _NT, preferred_element_type=F32) * scale
            sp = sp + bias_ref[hd, :, p * tq:(p + 1) * tq]
            first = CA_KBLOCKS - 1 - p
            if first > 0:
                sp = jnp.where(i >= first, sp, -jnp.inf)
            s.append(sp)
        m = functools.reduce(jnp.maximum, [jnp.max(sp, axis=-1, keepdims=True) for sp in s])
        acc = functools.reduce(jnp.add, [
            jnp.dot(jnp.exp(sp - m).astype(BF16), _with_ones(v_refs[p][:, cols]), preferred_element_type=F32)
            for p, sp in enumerate(s)])
        o_ref[:, cols] = (acc[:, :CA_HEAD_DIM] / acc[:, CA_HEAD_DIM:]).astype(o_ref.dtype)


def _ca_bias_table(rel_bias):
    H = rel_bias.shape[0]
    tq = CA_QBLOCK
    nk = CA_KBLOCKS * tq
    L = tq + nk - 1
    d = np.arange(L) - (tq - 1)
    rel_idx = np.clip((nk - tq) - d, -MAX_REL, MAX_REL) + MAX_REL
    wpad = jnp.pad(rel_bias[:, rel_idx].astype(F32), ((0, 0), (0, 1)))
    skew = jnp.tile(wpad, (1, tq))[:, :tq * L].reshape(H, tq, L)
    table = skew[:, :, tq - 1:tq - 1 + nk]
    r = np.arange(tq)[:, None]
    c = np.arange(nk)[None, :]
    qc = r // CHUNK + (nk - tq) // CHUNK
    kc = c // CHUNK
    band = (kc <= qc) & (kc >= qc - LEFT_CHUNKS)
    return jnp.where(band[None], table, -jnp.inf)


def _ca_attn(qkv, rel_bias, B, S):
    H, Dh = CA_HEADS, CA_HEAD_DIM
    hp = ATTN_HEADS_PER_STEP
    ng = H // hp
    tq = CA_QBLOCK
    nq = S // tq
    bias = _ca_bias_table(rel_bias)

    def kv_spec(which, p):
        back = CA_KBLOCKS - 1 - p
        return pl.BlockSpec((tq, hp * Dh), lambda b, g, i: (b * nq + jnp.maximum(i - back, 0), which * ng + g))

    return pl.pallas_call(
        functools.partial(_ca_attn_kernel, scale=Dh ** -0.5),
        grid=(B, ng, nq),
        in_specs=[pl.BlockSpec((tq, hp * Dh), lambda b, g, i: (b * nq + i, g))]
        + [kv_spec(1, p) for p in range(CA_KBLOCKS)]
        + [kv_spec(2, p) for p in range(CA_KBLOCKS)]
        + [pl.BlockSpec((hp, tq, CA_KBLOCKS * tq), lambda b, g, i: (g, 0, 0))],
        out_specs=pl.BlockSpec((tq, hp * Dh), lambda b, g, i: (b * nq + i, g)),
        out_shape=jax.ShapeDtypeStruct((B * S, H * Dh), BF16),
        compiler_params=_params("arbitrary", "arbitrary", "arbitrary"),
        name="ca_attn",
    )(qkv, qkv, qkv, qkv, qkv, qkv, qkv, bias)


def _router_kernel(h_ref, mod_ref, rw_ref, rb_ref, u_ref, idx_ref, gate_ref, rank_ref, cnt_ref, carry_sc):
    t = pl.program_id(0)

    @pl.when(t == 0)
    def _():
        carry_sc[...] = jnp.zeros(carry_sc.shape, F32)

    u = _modulate(h_ref[...], mod_ref, 3)
    u_ref[...] = u
    tb = u.shape[0]
    logits = lax.dot_general(rw_ref[...], u, _NT, precision=lax.Precision.HIGHEST,
                             preferred_element_type=F32) + rb_ref[...]
    eid = lax.broadcasted_iota(jnp.int32, logits.shape, 0)
    m1 = jnp.max(logits, axis=0, keepdims=True)
    i1 = jnp.min(jnp.where(logits == m1, eid, N_EXPERTS), axis=0, keepdims=True)
    rest = jnp.where(eid == i1, -jnp.inf, logits)
    m2 = jnp.max(rest, axis=0, keepdims=True)
    i2 = jnp.min(jnp.where(rest == m2, eid, N_EXPERTS), axis=0, keepdims=True)
    e2 = jnp.exp(m2 - m1)
    den = 1.0 + e2
    idx_ref[...] = jnp.concatenate([i1, i2], axis=0)
    gate_ref[...] = jnp.concatenate([1.0 / den, e2 / den], axis=0)
    onehot = jnp.where((eid == i1) | (eid == i2), 1.0, 0.0)
    tri = jnp.where(lax.broadcasted_iota(jnp.int32, (tb, tb), 0) <= lax.broadcasted_iota(jnp.int32, (tb, tb), 1),
                    1.0, 0.0).astype(BF16)
    incl = jnp.dot(onehot.astype(BF16), tri, preferred_element_type=F32)
    excl = incl - onehot + carry_sc[:, 0:1]
    r1 = jnp.sum(jnp.where(eid == i1, excl, 0.0), axis=0, keepdims=True)
    r2 = jnp.sum(jnp.where(eid == i2, excl, 0.0), axis=0, keepdims=True)
    rank_ref[...] = jnp.concatenate([r1, r2], axis=0).astype(jnp.int32)
    carry_sc[...] = carry_sc[...] + jnp.sum(onehot, axis=1, keepdims=True)
    cnt_ref[...] = carry_sc[...].astype(jnp.int32)


def _router(h, mod_l, router_w, router_b, S):
    T, D = h.shape
    E = N_EXPERTS
    tb = 512
    return pl.pallas_call(
        _router_kernel,
        grid=(T // tb,),
        in_specs=[
            pl.BlockSpec((tb, D), lambda t: (t, 0)),
            pl.BlockSpec((1, 8, D), lambda t: (t * tb // S, 0, 0)),
            pl.BlockSpec((E, D), lambda t: (0, 0)),
            pl.BlockSpec((E, 1), lambda t: (0, 0)),
        ],
        out_specs=[
            pl.BlockSpec((tb, D), lambda t: (t, 0)),
            pl.BlockSpec((TOP_K, tb), lambda t: (0, t)),
            pl.BlockSpec((TOP_K, tb), lambda t: (0, t)),
            pl.BlockSpec((TOP_K, tb), lambda t: (0, t)),
            pl.BlockSpec((E, LANES), lambda t: (0, 0)),
        ],
        out_shape=[
            jax.ShapeDtypeStruct((T, D), F32),
            jax.ShapeDtypeStruct((TOP_K, T), jnp.int32),
            jax.ShapeDtypeStruct((TOP_K, T), F32),
            jax.ShapeDtypeStruct((TOP_K, T), jnp.int32),
            jax.ShapeDtypeStruct((E, LANES), jnp.int32),
        ],
        scratch_shapes=[pltpu.VMEM((E, LANES), F32)],
        compiler_params=_params("arbitrary"),
        name="router",
    )(h, mod_l, router_w.T, router_b.reshape(E, 1))


def _row_gather_start(src_hbm, dst, sem, rows_ref, base, n):
    def issue(j2, carry):
        for prio in range(2):
            j = 2 * j2 + prio
            r = rows_ref[base + j]
            pltpu.make_async_copy(src_hbm.at[pl.ds(r, 1), :], dst.at[pl.ds(j, 1), :], sem).start(priority=prio)
        return carry
    lax.fori_loop(0, n // 2, issue, 0)


def _row_gather_wait(src_hbm, dst, sem, n):
    pltpu.make_async_copy(src_hbm.at[pl.ds(0, n), :], dst, sem).wait()


def _dispatch_kernel(row_tok_ref, blk_ref, nv_ref, u_hbm, o_ref, buf, sem):
    c = pl.program_id(0)
    nv = nv_ref[0]
    rb = MOE_ROW_BLOCK
    slot = c % 2

    def start(step, sl):
        _row_gather_start(u_hbm, buf.at[sl], sem.at[sl], row_tok_ref, blk_ref[step] * rb, rb)

    @pl.when((c == 0) & (nv > 0))
    def _():
        start(0, 0)

    @pl.when(c + 1 < nv)
    def _():
        start(c + 1, 1 - slot)

    @pl.when(c < nv)
    def _():
        _row_gather_wait(u_hbm, buf.at[slot], sem.at[slot], rb)
        o_ref[...] = buf[slot].astype(BF16)

    @pl.when(c >= nv)
    def _():
        o_ref[...] = jnp.zeros(o_ref.shape, BF16)


def _dispatch(u, row_tok, blk_ids, n_valid, n_rows):
    T, D = u.shape
    rb = MOE_ROW_BLOCK
    return pl.pallas_call(
        _dispatch_kernel,
        grid_spec=pltpu.PrefetchScalarGridSpec(
            num_scalar_prefetch=3,
            grid=(blk_ids.shape[0],),
            in_specs=[pl.BlockSpec(memory_space=pl.ANY)],
            out_specs=pl.BlockSpec((rb, D), lambda c, rt, blk, nv: (blk[c], 0)),
            scratch_shapes=[pltpu.VMEM((2, rb, D), F32), pltpu.SemaphoreType.DMA((2,))],
        ),
        out_shape=jax.ShapeDtypeStruct((n_rows + rb, D), BF16),
        compiler_params=_params("arbitrary"),
        name="moe_dispatch",
    )(row_tok, blk_ids, n_valid, u)


def _moe_ffn_kernel(se_ref, nblk_ref, x_ref, wg_ref, wu_ref, wd_ref, y_hbm, acc, wg_sc, wu_sc, wd_sc, sem):
    s = pl.program_id(0)
    f = pl.program_id(1)
    nb = nblk_ref[s]
    rb = MOE_ROW_BLOCK

    @pl.when(nb > 0)
    def _():
        @pl.when(f == 0)
        def _():
            acc[...] = jnp.zeros(acc.shape, F32)

        wg_sc[...] = wg_ref[...].astype(BF16)
        wu_sc[...] = wu_ref[...].astype(BF16)
        wd_sc[...] = wd_ref[...].astype(BF16)

        def rows_at(start, size):
            rows = pl.ds(pl.multiple_of(start, rb), size)
            x = x_ref[rows, :]
            g = jnp.dot(x, wg_sc[...], preferred_element_type=F32)
            up = jnp.dot(x, wu_sc[...], preferred_element_type=F32)
            hm = (_silu(g) * up).astype(BF16)
            acc[rows, :] += jnp.dot(hm, wd_sc[...], preferred_element_type=F32)

        def pair(r, carry):
            rows_at(r * (2 * rb), 2 * rb)
            return carry

        lax.fori_loop(0, nb // 2, pair, 0)

        @pl.when(nb % 2 == 1)
        def _():
            rows_at((nb - 1) * rb, rb)

        @pl.when(f == pl.num_programs(1) - 1)
        def _():
            def out_copy(r):
                return pltpu.make_async_copy(acc.at[pl.ds(pl.multiple_of(r * rb, rb), rb), :],
                                             y_hbm.at[pl.ds(pl.multiple_of(s * MOE_SLOT_ROWS + r * rb, rb), rb), :],
                                             sem.at[0])

            def start(r, carry):
                out_copy(r).start()
                return carry

            def wait(r, carry):
                out_copy(r).wait()
                return carry

            lax.fori_loop(0, nb, start, 0)
            lax.fori_loop(0, nb, wait, 0)


def _moe_ffn(xs, slot_e, nblk, w_gate, w_up, w_down, layer):
    D = xs.shape[1]
    F = w_gate.shape[3]
    R = MOE_SLOT_ROWS
    n_slots = slot_e.shape[0]
    tf = 256
    nf = F // tf

    def f_eff(s, f, nb):
        return jnp.where(nb[s] > 0, f, nf - 1)

    return pl.pallas_call(
        _moe_ffn_kernel,
        grid_spec=pltpu.PrefetchScalarGridSpec(
            num_scalar_prefetch=2,
            grid=(n_slots, nf),
            in_specs=[
                pl.BlockSpec((R, D), lambda s, f, se, nb: (s, 0), pipeline_mode=pl.Buffered(1)),
                pl.BlockSpec((None, None, D, tf), lambda s, f, se, nb: (layer, se[s], 0, f_eff(s, f, nb))),
                pl.BlockSpec((None, None, D, tf), lambda s, f, se, nb: (layer, se[s], 0, f_eff(s, f, nb))),
                pl.BlockSpec((None, None, tf, D), lambda s, f, se, nb: (layer, se[s], f_eff(s, f, nb), 0)),
            ],
            out_specs=pl.BlockSpec(memory_space=pl.ANY),
            scratch_shapes=[pltpu.VMEM((R, D), F32), pltpu.VMEM((D, tf), BF16), pltpu.VMEM((D, tf), BF16),
                            pltpu.VMEM((tf, D), BF16), pltpu.SemaphoreType.DMA((1,))],
        ),
        out_shape=jax.ShapeDtypeStruct((n_slots * R, D), F32),
        compiler_params=_params("arbitrary", "arbitrary"),
        name="moe_ffn",
    )(slot_e, nblk, xs, w_gate, w_up, w_down)


def _combine_kernel(dest_ref, y_hbm, gate_ref, h_ref, mod_ref, g_ref, b_ref, out_ref, buf0, buf1, sem, *, n_tok):
    i = pl.program_id(0)
    tb = out_ref.shape[0]
    slot = i % 2

    def start(step, sl):
        _row_gather_start(y_hbm, buf0.at[sl], sem.at[0, sl], dest_ref, step * tb, tb)
        _row_gather_start(y_hbm, buf1.at[sl], sem.at[1, sl], dest_ref, n_tok + step * tb, tb)

    @pl.when(i == 0)
    def _():
        start(0, 0)

    @pl.when(i + 1 < pl.num_programs(0))
    def _():
        start(i + 1, 1 - slot)

    _row_gather_wait(y_hbm, buf0.at[slot], sem.at[0, slot], tb)
    _row_gather_wait(y_hbm, buf1.at[slot], sem.at[1, slot], tb)
    y = gate_ref[:, 0:1] * buf0[slot] + gate_ref[:, 1:2] * buf1[slot]
    z = DEEPNORM_ALPHA * h_ref[...] + (1.0 + mod_ref[0, 5:6, :]) * y
    out_ref[...] = _layer_norm(z, g_ref[...], b_ref[...])


def _combine(y, dest, gates_tk, h, mod_l, ln_g, ln_b, S):
    T, D = h.shape
    tb = 256
    return pl.pallas_call(
        functools.partial(_combine_kernel, n_tok=T),
        grid_spec=pltpu.PrefetchScalarGridSpec(
            num_scalar_prefetch=1,
            grid=(T // tb,),
            in_specs=[
                pl.BlockSpec(memory_space=pl.ANY),
                pl.BlockSpec((tb, TOP_K), lambda i, d: (i, 0)),
                pl.BlockSpec((tb, D), lambda i, d: (i, 0)),
                pl.BlockSpec((1, 8, D), lambda i, d: (i * tb // S, 0, 0)),
                pl.BlockSpec((1, D), lambda i, d: (0, 0)),
                pl.BlockSpec((1, D), lambda i, d: (0, 0)),
            ],
            out_specs=pl.BlockSpec((tb, D), lambda i, d: (i, 0)),
            scratch_shapes=[pltpu.VMEM((2, tb, D), F32), pltpu.VMEM((2, tb, D), F32),
                            pltpu.SemaphoreType.DMA((2, 2))],
        ),
        out_shape=jax.ShapeDtypeStruct((T, D), F32),
        compiler_params=_params("arbitrary"),
        name="moe_combine",
    )(dest, y, gates_tk, h, mod_l, ln_g.reshape(1, D), ln_b.reshape(1, D))


def _moe_layer(h, mod_l, router_w, router_b, w_gate, w_up, w_down, layer, ln_g, ln_b, S):
    T, D = h.shape
    E, R, rb = N_EXPERTS, MOE_SLOT_ROWS, MOE_ROW_BLOCK
    n_slots = -(-T * TOP_K // R) + E
    n_rows = n_slots * R
    per_slot = R // rb
    max_blocks = T * TOP_K // rb + E
    u, idx, gates, rank, cnt = _router(h, mod_l, router_w, router_b, S)
    cnt = cnt[:, 0]
    slots_e = (cnt + R - 1) // R
    slot_end = jnp.cumsum(slots_e)
    slot_start = slot_end - slots_e
    row_start = slot_start * R
    dest = rank
    for e in range(E):
        dest = dest + jnp.where(idx == e, row_start[e], 0)
    tok = jnp.tile(jnp.arange(T, dtype=jnp.int32), TOP_K)
    row_tok = jnp.zeros((n_rows,), jnp.int32).at[dest.reshape(-1)].set(tok)
    sid = jnp.arange(n_slots, dtype=jnp.int32)
    n_used = slot_end[-1]
    last_used = jnp.maximum(n_used - 1, 0)
    se = jnp.minimum(jnp.searchsorted(slot_end, jnp.minimum(sid, last_used), side='right'), E - 1).astype(jnp.int32)
    rows_in = jnp.clip(cnt[se] - (sid - slot_start[se]) * R, 0, R)
    rows_in = jnp.where(sid < n_used, rows_in, 0)
    nblk = ((rows_in + rb - 1) // rb).astype(jnp.int32)
    blk = jnp.arange(n_rows // rb, dtype=jnp.int32)
    blk_valid = (blk % per_slot) < nblk[blk // per_slot]
    order = jnp.argsort(jnp.logical_not(blk_valid), stable=True).astype(jnp.int32)
    n_valid = jnp.sum(blk_valid.astype(jnp.int32))
    blk_ids = jnp.where(jnp.arange(max_blocks) < n_valid, order[:max_blocks], n_rows // rb).astype(jnp.int32)
    xs = _dispatch(u, row_tok, blk_ids, n_valid.reshape(1), n_rows)
    y = _moe_ffn(xs, se, nblk, w_gate, w_up, w_down, layer)
    return _combine(y, dest.reshape(-1).astype(jnp.int32), gates.T, h, mod_l, ln_g, ln_b, S)


def kernel(x, c, positions, ada_w, ada_b, ada_table, ln_g, ln_b, mla_w_dq, mla_q_norm, mla_w_uq, mla_w_dkv, mla_kv_norm, mla_w_ukv, mla_w_o, ca_w_qkv, ca_b_qkv, ca_rel_bias, ca_w_o, ffn_w_gate, ffn_w_up, ffn_w_down, moe_router_w, moe_router_b, moe_w_gate, moe_w_up, moe_w_down):
    B, S, D = x.shape
    T = B * S
    mod = _cond(c, ada_w, ada_b, ada_table)
    half = QK_ROPE // 2
    inv_freq = ROPE_THETA ** (-jnp.arange(half, dtype=F32) / half)
    freq_row = jnp.concatenate([inv_freq, inv_freq, jnp.zeros((LANES - QK_ROPE,), F32)]).reshape(1, LANES)
    pos_col = positions.reshape(T, 1).astype(jnp.int32)
    h = x.reshape(T, D)
    for i in range(DEPTH):
        j = i // 2
        mod_l = mod[i]
        if i % 2 == 0:
            q, k, v = _mla_proj(h, mod_l, pos_col, freq_row, mla_w_dq[j], mla_q_norm[j], mla_w_dkv[j],
                                mla_kv_norm[j], mla_w_uq[j], mla_w_ukv[j], S)
            o = _mla_attn(q, k, v, B, S)
            h = _proj_ln(o, mla_w_o[j], h, mod_l, ln_g[i, 0], ln_b[i, 0], S)
            h = _ffn(h, mod_l, ffn_w_gate, ffn_w_up, ffn_w_down, j, ln_g[i, 1], ln_b[i, 1], S)
        else:
            qkv = _qkv(h, mod_l, ca_w_qkv, j, ca_b_qkv[j], S)
            o = _ca_attn(qkv, ca_rel_bias[j], B, S)
            h = _proj_ln(o, ca_w_o[j], h, mod_l, ln_g[i, 0], ln_b[i, 0], S)
            h = _moe_layer(h, mod_l, moe_router_w[j], moe_router_b[j], moe_w_gate, moe_w_up,
                           moe_w_down, j, ln_g[i, 1], ln_b[i, 1], S)
    return h.reshape(B, S, D)
```

```python
import functools

import numpy as np
import jax
import jax.numpy as jnp
from jax import lax
from jax.experimental import pallas as pl
from jax.experimental.pallas import tpu as pltpu

F32 = jnp.float32
BF16 = jnp.bfloat16

CHUNK = 64
MLA_HEADS = 16
QK_NOPE = 128
QK_ROPE = 64
V_HEAD = 128
KV_LORA = 512
ROPE_THETA = 10000.0
CA_HEADS = 16
CA_HEAD_DIM = 128
LEFT_CHUNKS = 8
MAX_REL = 256
N_EXPERTS = 8
TOP_K = 2
N_MOD = 6
LN_EPS = 1e-5
RMS_EPS = 1e-6
DEPTH = 4
DEEPNORM_ALPHA = (2.0 * DEPTH) ** 0.25

LANES = 128
VMEM_LIMIT = 56 * 1024 * 1024

MLA_HEAD_PAD = 2 * LANES
MOE_SLOT_ROWS = 2560
MOE_ROW_BLOCK = 256
CA_QBLOCK = 256
CA_KBLOCKS = 3
ATTN_HEADS_PER_STEP = 4

_NT = (((1,), (1,)), ((), ()))


def _params(*sem):
    return pltpu.CompilerParams(dimension_semantics=sem, vmem_limit_bytes=VMEM_LIMIT)


def _silu(x):
    return x * jax.nn.sigmoid(x)


def _layer_norm(z, g, b):
    mu = jnp.mean(z, axis=-1, keepdims=True)
    zc = z - mu
    var = jnp.mean(zc * zc, axis=-1, keepdims=True)
    return zc * lax.rsqrt(var + LN_EPS) * g + b


def _modulate(h, mod_ref, k):
    return h * (1.0 + mod_ref[0, k + 1:k + 2, :]) + mod_ref[0, k:k + 1, :]


def _cond_kernel(c_ref, w_ref, b_ref, tab_ref, o_ref):
    s = _silu(c_ref[...]).astype(BF16)
    y = jnp.dot(s, w_ref[...].astype(BF16), preferred_element_type=F32) + b_ref[...]
    o_ref[...] = y[None, :, :] + tab_ref[...][:, None, :]


def _cond(c, ada_w, ada_b, ada_table):
    B, D = c.shape
    N = ada_w.shape[1]
    tn = 1024
    c8 = jnp.zeros((8, D), F32).at[:B].set(c)
    out = pl.pallas_call(
        _cond_kernel,
        grid=(N // tn,),
        in_specs=[
            pl.BlockSpec((8, D), lambda n: (0, 0)),
            pl.BlockSpec((D, tn), lambda n: (0, n)),
            pl.BlockSpec((1, tn), lambda n: (0, n)),
            pl.BlockSpec((DEPTH, tn), lambda n: (0, n)),
        ],
        out_specs=pl.BlockSpec((DEPTH, 8, tn), lambda n: (0, 0, n)),
        out_shape=jax.ShapeDtypeStruct((DEPTH, 8, N), F32),
        compiler_params=_params("arbitrary"),
        name="cond",
    )(c8, ada_w, ada_b.reshape(1, N), ada_table.reshape(DEPTH, N))
    mod = out[:, :B].reshape(DEPTH, B, N_MOD, D)
    return jnp.pad(mod, ((0, 0), (0, 0), (0, 8 - N_MOD), (0, 0)))


def _rope_lanes(x, cos, sin_signed):
    lane = lax.broadcasted_iota(jnp.int32, x.shape, 1)
    partner = jnp.where(lane < QK_ROPE // 2, pltpu.roll(x, LANES - QK_ROPE // 2, 1),
                        pltpu.roll(x, QK_ROPE // 2, 1))
    return x * cos + partner * sin_signed


def _mla_proj_kernel(h_ref, mod_ref, pos_ref, freq_ref, wdq_ref, qn_ref, wdkv_ref, kvn_ref, wkr_ref,
                     wuq_ref, wukv_ref, q_ref, k_ref, v_ref):
    u = _modulate(h_ref[...], mod_ref, 0).astype(BF16)
    ang = pos_ref[...].astype(F32) * freq_ref[...]
    cos = jnp.cos(ang)
    sin = jnp.sin(ang)
    lane = lax.broadcasted_iota(jnp.int32, ang.shape, 1)
    sin_signed = jnp.where(lane < QK_ROPE // 2, -sin, sin)

    def rms(x, g):
        return (x * lax.rsqrt(jnp.mean(x * x, axis=-1, keepdims=True) + RMS_EPS)) * g

    cq = rms(jnp.dot(u, wdq_ref[...], preferred_element_type=F32), qn_ref[...]).astype(BF16)
    ckv = rms(jnp.dot(u, wdkv_ref[...], preferred_element_type=F32), kvn_ref[...]).astype(BF16)
    kr = jnp.dot(u, wkr_ref[...], preferred_element_type=F32)
    kr = _rope_lanes(kr, cos, sin_signed).astype(BF16)
    for hd in range(MLA_HEADS):
        c0 = hd * MLA_HEAD_PAD
        qh = jnp.dot(cq, wuq_ref[:, c0:c0 + MLA_HEAD_PAD], preferred_element_type=F32)
        q_ref[:, c0:c0 + LANES] = qh[:, :LANES].astype(BF16)
        q_ref[:, c0 + LANES:c0 + MLA_HEAD_PAD] = _rope_lanes(qh[:, LANES:], cos, sin_signed).astype(BF16)
        kvh = jnp.dot(ckv, wukv_ref[:, c0:c0 + MLA_HEAD_PAD], preferred_element_type=F32)
        k_ref[:, c0:c0 + LANES] = kvh[:, :LANES].astype(BF16)
        k_ref[:, c0 + LANES:c0 + MLA_HEAD_PAD] = kr
        v_ref[:, hd * V_HEAD:(hd + 1) * V_HEAD] = kvh[:, LANES:].astype(BF16)


def _mla_proj(h, mod_l, pos_col, freq_row, w_dq, q_norm, w_dkv, kv_norm, w_uq, w_ukv, S):
    T, D = h.shape
    H = MLA_HEADS
    tm = 256
    q_lora = w_dq.shape[1]
    wdq = w_dq.astype(BF16)
    wdkv = w_dkv[:, :KV_LORA].astype(BF16)
    wkr = jnp.pad(w_dkv[:, KV_LORA:], ((0, 0), (0, LANES - QK_ROPE))).astype(BF16)
    wuq = jnp.pad(w_uq.reshape(q_lora, H, QK_NOPE + QK_ROPE),
                  ((0, 0), (0, 0), (0, MLA_HEAD_PAD - QK_NOPE - QK_ROPE))).reshape(q_lora, H * MLA_HEAD_PAD)
    wuq = wuq.astype(BF16)
    wukv = w_ukv.astype(BF16)
    const = lambda i: (0, 0)
    row = lambda i: (i, 0)
    return pl.pallas_call(
        _mla_proj_kernel,
        grid=(T // tm,),
        in_specs=[
            pl.BlockSpec((tm, D), row),
            pl.BlockSpec((1, 8, D), lambda i: (i * tm // S, 0, 0)),
            pl.BlockSpec((tm, 1), row),
            pl.BlockSpec((1, LANES), const),
            pl.BlockSpec(wdq.shape, const),
            pl.BlockSpec((1, q_lora), const),
            pl.BlockSpec(wdkv.shape, const),
            pl.BlockSpec((1, KV_LORA), const),
            pl.BlockSpec(wkr.shape, const),
            pl.BlockSpec(wuq.shape, const),
            pl.BlockSpec(wukv.shape, const),
        ],
        out_specs=[
            pl.BlockSpec((tm, H * MLA_HEAD_PAD), row),
            pl.BlockSpec((tm, H * MLA_HEAD_PAD), row),
            pl.BlockSpec((tm, H * V_HEAD), row),
        ],
        out_shape=[
            jax.ShapeDtypeStruct((T, H * MLA_HEAD_PAD), BF16),
            jax.ShapeDtypeStruct((T, H * MLA_HEAD_PAD), BF16),
            jax.ShapeDtypeStruct((T, H * V_HEAD), BF16),
        ],
        compiler_params=_params("arbitrary"),
        name="mla_proj",
    )(h, mod_l, pos_col, freq_row, wdq, q_norm.reshape(1, -1), wdkv, kv_norm.reshape(1, -1), wkr, wuq, wukv)


def _with_ones(v):
    return jnp.concatenate([v, jnp.ones_like(v)], axis=1)


def _mla_attn_kernel(pi_ref, pj_ref, q_ref, k_ref, v_ref, o_ref, m_sc, acc_sc, *, scale):
    p = pl.program_id(2)
    i = pi_ref[p]
    j = pj_ref[p]

    @pl.when(j == 0)
    def _():
        m_sc[...] = jnp.full(m_sc.shape, -jnp.inf, F32)
        acc_sc[...] = jnp.zeros(acc_sc.shape, F32)

    def accumulate(diagonal):
        for hd in range(ATTN_HEADS_PER_STEP):
            qk = slice(hd * MLA_HEAD_PAD, (hd + 1) * MLA_HEAD_PAD)
            s = lax.dot_general(q_ref[:, qk], k_ref[:, qk], _NT, preferred_element_type=F32) * scale
            if diagonal:
                qc = lax.broadcasted_iota(jnp.int32, s.shape, 0) // CHUNK
                kc = lax.broadcasted_iota(jnp.int32, s.shape, 1) // CHUNK
                s = jnp.where(kc <= qc, s, -jnp.inf)
            m_prev = m_sc[hd]
            m_new = jnp.maximum(m_prev, jnp.max(s, axis=-1, keepdims=True))
            a = jnp.exp(m_prev - m_new)
            e = jnp.exp(s - m_new).astype(BF16)
            v = _with_ones(v_ref[:, hd * V_HEAD:(hd + 1) * V_HEAD])
            acc_sc[hd] = a * acc_sc[hd] + jnp.dot(e, v, preferred_element_type=F32)
            m_sc[hd] = m_new

    @pl.when(j < i)
    def _():
        accumulate(False)

    @pl.when(j == i)
    def _():
        accumulate(True)
        for hd in range(ATTN_HEADS_PER_STEP):
            acc = acc_sc[hd]
            o_ref[:, hd * V_HEAD:(hd + 1) * V_HEAD] = (acc[:, :V_HEAD] / acc[:, V_HEAD:]).astype(o_ref.dtype)


def _mla_attn(q, k, v, B, S):
    H = MLA_HEADS
    hp = ATTN_HEADS_PER_STEP
    tq = 512
    nq = S // tq
    scale = (QK_NOPE + QK_ROPE) ** -0.5
    pairs = [(i, j) for i in range(nq) for j in range(i + 1)]
    pi = jnp.asarray([p[0] for p in pairs], jnp.int32)
    pj = jnp.asarray([p[1] for p in pairs], jnp.int32)
    return pl.pallas_call(
        functools.partial(_mla_attn_kernel, scale=scale),
        grid_spec=pltpu.PrefetchScalarGridSpec(
            num_scalar_prefetch=2,
            grid=(B, H // hp, len(pairs)),
            in_specs=[
                pl.BlockSpec((tq, hp * MLA_HEAD_PAD), lambda b, g, p, pi, pj: (b * nq + pi[p], g)),
                pl.BlockSpec((tq, hp * MLA_HEAD_PAD), lambda b, g, p, pi, pj: (b * nq + pj[p], g)),
                pl.BlockSpec((tq, hp * V_HEAD), lambda b, g, p, pi, pj: (b * nq + pj[p], g)),
            ],
            out_specs=pl.BlockSpec((tq, hp * V_HEAD), lambda b, g, p, pi, pj: (b * nq + pi[p], g)),
            scratch_shapes=[pltpu.VMEM((hp, tq, 1), F32), pltpu.VMEM((hp, tq, 2 * V_HEAD), F32)],
        ),
        out_shape=jax.ShapeDtypeStruct((B * S, H * V_HEAD), BF16),
        compiler_params=_params("arbitrary", "arbitrary", "arbitrary"),
        name="mla_attn",
    )(pi, pj, q, k, v)


def _proj_ln_kernel(o_ref, w_ref, h_ref, mod_ref, g_ref, b_ref, out_ref, *, gate_row):
    y = jnp.dot(o_ref[...], w_ref[...], preferred_element_type=F32)
    z = DEEPNORM_ALPHA * h_ref[...] + (1.0 + mod_ref[0, gate_row:gate_row + 1, :]) * y
    out_ref[...] = _layer_norm(z, g_ref[...], b_ref[...])


def _proj_ln(o, w_o, h, mod_l, ln_g, ln_b, S):
    T, D = h.shape
    tm = 512
    w = w_o.astype(BF16)
    row = lambda i: (i, 0)
    const = lambda i: (0, 0)
    return pl.pallas_call(
        functools.partial(_proj_ln_kernel, gate_row=2),
        grid=(T // tm,),
        in_specs=[
            pl.BlockSpec((tm, o.shape[1]), row),
            pl.BlockSpec(w.shape, const),
            pl.BlockSpec((tm, D), row),
            pl.BlockSpec((1, 8, D), lambda i: (i * tm // S, 0, 0)),
            pl.BlockSpec((1, D), const),
            pl.BlockSpec((1, D), const),
        ],
        out_specs=pl.BlockSpec((tm, D), row),
        out_shape=jax.ShapeDtypeStruct((T, D), F32),
        compiler_params=_params("arbitrary"),
        name="proj_ln",
    )(o, w, h, mod_l, ln_g.reshape(1, D), ln_b.reshape(1, D))


def _ffn_kernel(h_ref, mod_ref, wg_ref, wu_ref, wd_ref, g_ref, b_ref, out_ref, u_sc):
    f = pl.program_id(1)

    @pl.when(f == 0)
    def _():
        u_sc[...] = _modulate(h_ref[...], mod_ref, 3).astype(BF16)
        out_ref[...] = jnp.zeros(out_ref.shape, F32)

    u = u_sc[...]
    g = jnp.dot(u, wg_ref[...].astype(BF16), preferred_element_type=F32)
    up = jnp.dot(u, wu_ref[...].astype(BF16), preferred_element_type=F32)
    hm = (_silu(g) * up).astype(BF16)
    out_ref[...] += jnp.dot(hm, wd_ref[...].astype(BF16), preferred_element_type=F32)

    @pl.when(f == pl.num_programs(1) - 1)
    def _():
        z = DEEPNORM_ALPHA * h_ref[...] + (1.0 + mod_ref[0, 5:6, :]) * out_ref[...]
        out_ref[...] = _layer_norm(z, g_ref[...], b_ref[...])


def _ffn(h, mod_l, w_gate, w_up, w_down, layer, ln_g, ln_b, S):
    T, D = h.shape
    F = w_gate.shape[2]
    tm = min(1024, T)
    tf = 256
    row = lambda i, f: (i, 0)
    const = lambda i, f: (0, 0)
    return pl.pallas_call(
        _ffn_kernel,
        grid=(T // tm, F // tf),
        in_specs=[
            pl.BlockSpec((tm, D), row, pipeline_mode=pl.Buffered(1)),
            pl.BlockSpec((1, 8, D), lambda i, f: (i * tm // S, 0, 0)),
            pl.BlockSpec((None, D, tf), lambda i, f: (layer, 0, f)),
            pl.BlockSpec((None, D, tf), lambda i, f: (layer, 0, f)),
            pl.BlockSpec((None, tf, D), lambda i, f: (layer, f, 0)),
            pl.BlockSpec((1, D), const),
            pl.BlockSpec((1, D), const),
        ],
        out_specs=pl.BlockSpec((tm, D), row),
        out_shape=jax.ShapeDtypeStruct((T, D), F32),
        scratch_shapes=[pltpu.VMEM((tm, D), BF16)],
        compiler_params=_params("arbitrary", "arbitrary"),
        name="ffn",
    )(h, mod_l, w_gate, w_up, w_down, ln_g.reshape(1, D), ln_b.reshape(1, D))


def _qkv_kernel(h_ref, mod_ref, w_ref, b_ref, o_ref, u_sc):
    @pl.when(pl.program_id(1) == 0)
    def _():
        u_sc[...] = _modulate(h_ref[...], mod_ref, 0).astype(BF16)

    y = jnp.dot(u_sc[...], w_ref[...], preferred_element_type=F32) + b_ref[...]
    o_ref[...] = y.astype(o_ref.dtype)


def _qkv(h, mod_l, w_qkv, layer, b_qkv, S):
    T, D = h.shape
    N = w_qkv.shape[2]
    tm = min(1024, T)
    tn = 1024
    w = w_qkv[layer].astype(BF16)
    return pl.pallas_call(
        _qkv_kernel,
        grid=(T // tm, N // tn),
        in_specs=[
            pl.BlockSpec((tm, D), lambda i, n: (i, 0), pipeline_mode=pl.Buffered(1)),
            pl.BlockSpec((1, 8, D), lambda i, n: (i * tm // S, 0, 0)),
            pl.BlockSpec((D, tn), lambda i, n: (0, n)),
            pl.BlockSpec((1, tn), lambda i, n: (0, n)),
        ],
        out_specs=pl.BlockSpec((tm, tn), lambda i, n: (i, n)),
        out_shape=jax.ShapeDtypeStruct((T, N), BF16),
        scratch_shapes=[pltpu.VMEM((tm, D), BF16)],
        compiler_params=_params("arbitrary", "arbitrary"),
        name="qkv",
    )(h, mod_l, w, b_qkv.reshape(1, N))


def _ca_attn_kernel(q_ref, k0_ref, k1_ref, k2_ref, v0_ref, v1_ref, v2_ref, bias_ref, o_ref, *, scale):
    i = pl.program_id(2)
    tq = q_ref.shape[0]
    k_refs = (k0_ref, k1_ref, k2_ref)
    v_refs = (v0_ref, v1_ref, v2_ref)
    for hd in range(ATTN_HEADS_PER_STEP):
        cols = slice(hd * CA_HEAD_DIM, (hd + 1) * CA_HEAD_DIM)
        q = q_ref[:, cols]
        s = []
        for p in range(CA_KBLOCKS):
            sp = lax.dot_general(q, k_refs[p][:, cols], _NT, preferred_element_type=F32) * scale
            sp = sp + bias_ref[hd, :, p * tq:(p + 1) * tq]
            first = CA_KBLOCKS - 1 - p
            if first > 0:
                sp = jnp.where(i >= first, sp, -jnp.inf)
            s.append(sp)
        m = functools.reduce(jnp.maximum, [jnp.max(sp, axis=-1, keepdims=True) for sp in s])
        acc = functools.reduce(jnp.add, [
            jnp.dot(jnp.exp(sp - m).astype(BF16), _with_ones(v_refs[p][:, cols]), preferred_element_type=F32)
            for p, sp in enumerate(s)])
        o_ref[:, cols] = (acc[:, :CA_HEAD_DIM] / acc[:, CA_HEAD_DIM:]).astype(o_ref.dtype)


def _ca_bias_table(rel_bias):
    H = rel_bias.shape[0]
    tq = CA_QBLOCK
    nk = CA_KBLOCKS * tq
    L = tq + nk - 1
    d = np.arange(L) - (tq - 1)
    rel_idx = np.clip((nk - tq) - d, -MAX_REL, MAX_REL) + MAX_REL
    wpad = jnp.pad(rel_bias[:, rel_idx].astype(F32), ((0, 0), (0, 1)))
    skew = jnp.tile(wpad, (1, tq))[:, :tq * L].reshape(H, tq, L)
    table = skew[:, :, tq - 1:tq - 1 + nk]
    r = np.arange(tq)[:, None]
    c = np.arange(nk)[None, :]
    qc = r // CHUNK + (nk - tq) // CHUNK
    kc = c // CHUNK
    band = (kc <= qc) & (kc >= qc - LEFT_CHUNKS)
    return jnp.where(band[None], table, -jnp.inf)


def _ca_attn(qkv, rel_bias, B, S):
    H, Dh = CA_HEADS, CA_HEAD_DIM
    hp = ATTN_HEADS_PER_STEP
    ng = H // hp
    tq = CA_QBLOCK
    nq = S // tq
    bias = _ca_bias_table(rel_bias)

    def kv_spec(which, p):
        back = CA_KBLOCKS - 1 - p
        return pl.BlockSpec((tq, hp * Dh), lambda b, g, i: (b * nq + jnp.maximum(i - back, 0), which * ng + g))

    return pl.pallas_call(
        functools.partial(_ca_attn_kernel, scale=Dh ** -0.5),
        grid=(B, ng, nq),
        in_specs=[pl.BlockSpec((tq, hp * Dh), lambda b, g, i: (b * nq + i, g))]
        + [kv_spec(1, p) for p in range(CA_KBLOCKS)]
        + [kv_spec(2, p) for p in range(CA_KBLOCKS)]
        + [pl.BlockSpec((hp, tq, CA_KBLOCKS * tq), lambda b, g, i: (g, 0, 0))],
        out_specs=pl.BlockSpec((tq, hp * Dh), lambda b, g, i: (b * nq + i, g)),
        out_shape=jax.ShapeDtypeStruct((B * S, H * Dh), BF16),
        compiler_params=_params("arbitrary", "arbitrary", "arbitrary"),
        name="ca_attn",
    )(qkv, qkv, qkv, qkv, qkv, qkv, qkv, bias)


def _router_kernel(h_ref, mod_ref, rw_ref, rb_ref, u_ref, idx_ref, gate_ref, rank_ref, cnt_ref, carry_sc):
    t = pl.program_id(0)

    @pl.when(t == 0)
    def _():
        carry_sc[...] = jnp.zeros(carry_sc.shape, F32)

    u = _modulate(h_ref[...], mod_ref, 3)
    u_ref[...] = u
    tb = u.shape[0]
    logits = lax.dot_general(rw_ref[...], u, _NT, precision=lax.Precision.HIGHEST,
                             preferred_element_type=F32) + rb_ref[...]
    eid = lax.broadcasted_iota(jnp.int32, logits.shape, 0)
    m1 = jnp.max(logits, axis=0, keepdims=True)
    i1 = jnp.min(jnp.where(logits == m1, eid, N_EXPERTS), axis=0, keepdims=True)
    rest = jnp.where(eid == i1, -jnp.inf, logits)
    m2 = jnp.max(rest, axis=0, keepdims=True)
    i2 = jnp.min(jnp.where(rest == m2, eid, N_EXPERTS), axis=0, keepdims=True)
    e2 = jnp.exp(m2 - m1)
    den = 1.0 + e2
    idx_ref[...] = jnp.concatenate([i1, i2], axis=0)
    gate_ref[...] = jnp.concatenate([1.0 / den, e2 / den], axis=0)
    onehot = jnp.where((eid == i1) | (eid == i2), 1.0, 0.0)
    tri = jnp.where(lax.broadcasted_iota(jnp.int32, (tb, tb), 0) <= lax.broadcasted_iota(jnp.int32, (tb, tb), 1),
                    1.0, 0.0).astype(BF16)
    incl = jnp.dot(onehot.astype(BF16), tri, preferred_element_type=F32)
    excl = incl - onehot + carry_sc[:, 0:1]
    r1 = jnp.sum(jnp.where(eid == i1, excl, 0.0), axis=0, keepdims=True)
    r2 = jnp.sum(jnp.where(eid == i2, excl, 0.0), axis=0, keepdims=True)
    rank_ref[...] = jnp.concatenate([r1, r2], axis=0).astype(jnp.int32)
    carry_sc[...] = carry_sc[...] + jnp.sum(onehot, axis=1, keepdims=True)
    cnt_ref[...] = carry_sc[...].astype(jnp.int32)


def _router(h, mod_l, router_w, router_b, S):
    T, D = h.shape
    E = N_EXPERTS
    tb = 512
    return pl.pallas_call(
        _router_kernel,
        grid=(T // tb,),
        in_specs=[
            pl.BlockSpec((tb, D), lambda t: (t, 0)),
            pl.BlockSpec((1, 8, D), lambda t: (t * tb // S, 0, 0)),
            pl.BlockSpec((E, D), lambda t: (0, 0)),
            pl.BlockSpec((E, 1), lambda t: (0, 0)),
        ],
        out_specs=[
            pl.BlockSpec((tb, D), lambda t: (t, 0)),
            pl.BlockSpec((TOP_K, tb), lambda t: (0, t)),
            pl.BlockSpec((TOP_K, tb), lambda t: (0, t)),
            pl.BlockSpec((TOP_K, tb), lambda t: (0, t)),
            pl.BlockSpec((E, LANES), lambda t: (0, 0)),
        ],
        out_shape=[
            jax.ShapeDtypeStruct((T, D), F32),
            jax.ShapeDtypeStruct((TOP_K, T), jnp.int32),
            jax.ShapeDtypeStruct((TOP_K, T), F32),
            jax.ShapeDtypeStruct((TOP_K, T), jnp.int32),
            jax.ShapeDtypeStruct((E, LANES), jnp.int32),
        ],
        scratch_shapes=[pltpu.VMEM((E, LANES), F32)],
        compiler_params=_params("arbitrary"),
        name="router",
    )(h, mod_l, router_w.T, router_b.reshape(E, 1))


SUBLANES = 8


def _row_gather_start(src_hbm, dst, sem, rows_ref, base):
    def issue(g, carry):
        for k in range(SUBLANES):
            r = rows_ref[base + g * SUBLANES + k]
            pltpu.make_async_copy(src_hbm.at[pl.ds(r, 1), :], dst.at[g, pl.ds(k, 1), :], sem).start(priority=k % 2)
        return carry
    lax.fori_loop(0, dst.shape[0], issue, 0)


def _row_gather_wait(dst, sem):
    pltpu.make_async_copy(dst, dst, sem).wait()


def _gathered_rows(buf):
    return buf.reshape(buf.shape[0] * SUBLANES, buf.shape[2])


def _dispatch_kernel(row_tok_ref, blk_ref, nv_ref, u_hbm, o_ref, buf, sem):
    c = pl.program_id(0)
    nv = nv_ref[0]
    rb = MOE_ROW_BLOCK
    slot = c % 2

    def start(step, sl):
        _row_gather_start(u_hbm, buf.at[sl], sem.at[sl], row_tok_ref, blk_ref[step] * rb)

    @pl.when((c == 0) & (nv > 0))
    def _():
        start(0, 0)

    @pl.when(c + 1 < nv)
    def _():
        start(c + 1, 1 - slot)

    @pl.when(c < nv)
    def _():
        _row_gather_wait(buf.at[slot], sem.at[slot])
        o_ref[...] = _gathered_rows(buf[slot]).astype(BF16)

    @pl.when(c >= nv)
    def _():
        o_ref[...] = jnp.zeros(o_ref.shape, BF16)


def _dispatch(u, row_tok, blk_ids, n_valid, n_rows):
    T, D = u.shape
    rb = MOE_ROW_BLOCK
    return pl.pallas_call(
        _dispatch_kernel,
        grid_spec=pltpu.PrefetchScalarGridSpec(
            num_scalar_prefetch=3,
            grid=(blk_ids.shape[0],),
            in_specs=[pl.BlockSpec(memory_space=pl.ANY)],
            out_specs=pl.BlockSpec((rb, D), lambda c, rt, blk, nv: (blk[c], 0)),
            scratch_shapes=[pltpu.VMEM((2, rb // SUBLANES, SUBLANES, D), F32), pltpu.SemaphoreType.DMA((2,))],
        ),
        out_shape=jax.ShapeDtypeStruct((n_rows + rb, D), BF16),
        compiler_params=_params("arbitrary"),
        name="moe_dispatch",
    )(row_tok, blk_ids, n_valid, u)


def _moe_ffn_kernel(se_ref, nblk_ref, x_ref, wg_ref, wu_ref, wd_ref, y_hbm, acc, wg_sc, wu_sc, wd_sc, sem):
    s = pl.program_id(0)
    f = pl.program_id(1)
    nb = nblk_ref[s]
    rb = MOE_ROW_BLOCK

    @pl.when(nb > 0)
    def _():
        @pl.when(f == 0)
        def _():
            acc[...] = jnp.zeros(acc.shape, F32)

        wg_sc[...] = wg_ref[...].astype(BF16)
        wu_sc[...] = wu_ref[...].astype(BF16)
        wd_sc[...] = wd_ref[...].astype(BF16)

        def rows_at(start, size):
            rows = pl.ds(pl.multiple_of(start, rb), size)
            x = x_ref[rows, :]
            g = jnp.dot(x, wg_sc[...], preferred_element_type=F32)
            up = jnp.dot(x, wu_sc[...], preferred_element_type=F32)
            hm = (_silu(g) * up).astype(BF16)
            acc[rows, :] += jnp.dot(hm, wd_sc[...], preferred_element_type=F32)

        def pair(r, carry):
            rows_at(r * (2 * rb), 2 * rb)
            return carry

        lax.fori_loop(0, nb // 2, pair, 0)

        @pl.when(nb % 2 == 1)
        def _():
            rows_at((nb - 1) * rb, rb)

        @pl.when(f == pl.num_programs(1) - 1)
        def _():
            def out_copy(r):
                return pltpu.make_async_copy(acc.at[pl.ds(pl.multiple_of(r * rb, rb), rb), :],
                                             y_hbm.at[pl.ds(pl.multiple_of(s * MOE_SLOT_ROWS + r * rb, rb), rb), :],
                                             sem.at[0])

            def start(r, carry):
                out_copy(r).start()
                return carry

            def wait(r, carry):
                out_copy(r).wait()
                return carry

            lax.fori_loop(0, nb, start, 0)
            lax.fori_loop(0, nb, wait, 0)


def _moe_ffn(xs, slot_e, nblk, w_gate, w_up, w_down, layer):
    D = xs.shape[1]
    F = w_gate.shape[3]
    R = MOE_SLOT_ROWS
    n_slots = slot_e.shape[0]
    tf = 256
    nf = F // tf

    def f_eff(s, f, nb):
        return jnp.where(nb[s] > 0, f, nf - 1)

    return pl.pallas_call(
        _moe_ffn_kernel,
        grid_spec=pltpu.PrefetchScalarGridSpec(
            num_scalar_prefetch=2,
            grid=(n_slots, nf),
            in_specs=[
                pl.BlockSpec((R, D), lambda s, f, se, nb: (s, 0), pipeline_mode=pl.Buffered(1)),
                pl.BlockSpec((None, None, D, tf), lambda s, f, se, nb: (layer, se[s], 0, f_eff(s, f, nb))),
                pl.BlockSpec((None, None, D, tf), lambda s, f, se, nb: (layer, se[s], 0, f_eff(s, f, nb))),
                pl.BlockSpec((None, None, tf, D), lambda s, f, se, nb: (layer, se[s], f_eff(s, f, nb), 0)),
            ],
            out_specs=pl.BlockSpec(memory_space=pl.ANY),
            scratch_shapes=[pltpu.VMEM((R, D), F32), pltpu.VMEM((D, tf), BF16), pltpu.VMEM((D, tf), BF16),
                            pltpu.VMEM((tf, D), BF16), pltpu.SemaphoreType.DMA((1,))],
        ),
        out_shape=jax.ShapeDtypeStruct((n_slots * R, D), F32),
        compiler_params=_params("arbitrary", "arbitrary"),
        name="moe_ffn",
    )(slot_e, nblk, xs, w_gate, w_up, w_down)


def _combine_kernel(dest_ref, y_hbm, gate_ref, h_ref, mod_ref, g_ref, b_ref, out_ref, buf0, buf1, sem, *, n_tok):
    i = pl.program_id(0)
    tb = out_ref.shape[0]
    slot = i % 2

    def start(step, sl):
        _row_gather_start(y_hbm, buf0.at[sl], sem.at[0, sl], dest_ref, step * tb)
        _row_gather_start(y_hbm, buf1.at[sl], sem.at[1, sl], dest_ref, n_tok + step * tb)

    @pl.when(i == 0)
    def _():
        start(0, 0)

    @pl.when(i + 1 < pl.num_programs(0))
    def _():
        start(i + 1, 1 - slot)

    _row_gather_wait(buf0.at[slot], sem.at[0, slot])
    _row_gather_wait(buf1.at[slot], sem.at[1, slot])
    y = gate_ref[:, 0:1] * _gathered_rows(buf0[slot]) + gate_ref[:, 1:2] * _gathered_rows(buf1[slot])
    z = DEEPNORM_ALPHA * h_ref[...] + (1.0 + mod_ref[0, 5:6, :]) * y
    out_ref[...] = _layer_norm(z, g_ref[...], b_ref[...])


def _combine(y, dest, gates_tk, h, mod_l, ln_g, ln_b, S):
    T, D = h.shape
    tb = 256
    return pl.pallas_call(
        functools.partial(_combine_kernel, n_tok=T),
        grid_spec=pltpu.PrefetchScalarGridSpec(
            num_scalar_prefetch=1,
            grid=(T // tb,),
            in_specs=[
                pl.BlockSpec(memory_space=pl.ANY),
                pl.BlockSpec((tb, TOP_K), lambda i, d: (i, 0)),
                pl.BlockSpec((tb, D), lambda i, d: (i, 0)),
                pl.BlockSpec((1, 8, D), lambda i, d: (i * tb // S, 0, 0)),
                pl.BlockSpec((1, D), lambda i, d: (0, 0)),
                pl.BlockSpec((1, D), lambda i, d: (0, 0)),
            ],
            out_specs=pl.BlockSpec((tb, D), lambda i, d: (i, 0)),
            scratch_shapes=[pltpu.VMEM((2, tb // SUBLANES, SUBLANES, D), F32),
                            pltpu.VMEM((2, tb // SUBLANES, SUBLANES, D), F32),
                            pltpu.SemaphoreType.DMA((2, 2))],
        ),
        out_shape=jax.ShapeDtypeStruct((T, D), F32),
        compiler_params=_params("arbitrary"),
        name="moe_combine",
    )(dest, y, gates_tk, h, mod_l, ln_g.reshape(1, D), ln_b.reshape(1, D))


def _moe_layer(h, mod_l, router_w, router_b, w_gate, w_up, w_down, layer, ln_g, ln_b, S):
    T, D = h.shape
    E, R, rb = N_EXPERTS, MOE_SLOT_ROWS, MOE_ROW_BLOCK
    n_slots = -(-T * TOP_K // R) + E
    n_rows = n_slots * R
    per_slot = R // rb
    max_blocks = T * TOP_K // rb + E
    u, idx, gates, rank, cnt = _router(h, mod_l, router_w, router_b, S)
    cnt = cnt[:, 0]
    slots_e = (cnt + R - 1) // R
    slot_end = jnp.cumsum(slots_e)
    slot_start = slot_end - slots_e
    row_start = slot_start * R
    dest = rank
    for e in range(E):
        dest = dest + jnp.where(idx == e, row_start[e], 0)
    tok = jnp.tile(jnp.arange(T, dtype=jnp.int32), TOP_K)
    row_tok = jnp.zeros((n_rows,), jnp.int32).at[dest.reshape(-1)].set(tok)
    sid = jnp.arange(n_slots, dtype=jnp.int32)
    n_used = slot_end[-1]
    last_used = jnp.maximum(n_used - 1, 0)
    se = jnp.minimum(jnp.searchsorted(slot_end, jnp.minimum(sid, last_used), side='right'), E - 1).astype(jnp.int32)
    rows_in = jnp.clip(cnt[se] - (sid - slot_start[se]) * R, 0, R)
    rows_in = jnp.where(sid < n_used, rows_in, 0)
    nblk = ((rows_in + rb - 1) // rb).astype(jnp.int32)
    blk = jnp.arange(n_rows // rb, dtype=jnp.int32)
    blk_valid = (blk % per_slot) < nblk[blk // per_slot]
    order = jnp.argsort(jnp.logical_not(blk_valid), stable=True).astype(jnp.int32)
    n_valid = jnp.sum(blk_valid.astype(jnp.int32))
    blk_ids = jnp.where(jnp.arange(max_blocks) < n_valid, order[:max_blocks], n_rows // rb).astype(jnp.int32)
    xs = _dispatch(u, row_tok, blk_ids, n_valid.reshape(1), n_rows)
    y = _moe_ffn(xs, se, nblk, w_gate, w_up, w_down, layer)
    return _combine(y, dest.reshape(-1).astype(jnp.int32), gates.T, h, mod_l, ln_g, ln_b, S)


def kernel(x, c, positions, ada_w, ada_b, ada_table, ln_g, ln_b, mla_w_dq, mla_q_norm, mla_w_uq, mla_w_dkv, mla_kv_norm, mla_w_ukv, mla_w_o, ca_w_qkv, ca_b_qkv, ca_rel_bias, ca_w_o, ffn_w_gate, ffn_w_up, ffn_w_down, moe_router_w, moe_router_b, moe_w_gate, moe_w_up, moe_w_down):
    B, S, D = x.shape
    T = B * S
    mod = _cond(c, ada_w, ada_b, ada_table)
    half = QK_ROPE // 2
    inv_freq = ROPE_THETA ** (-jnp.arange(half, dtype=F32) / half)
    freq_row = jnp.concatenate([inv_freq, inv_freq, jnp.zeros((LANES - QK_ROPE,), F32)]).reshape(1, LANES)
    pos_col = positions.reshape(T, 1).astype(jnp.int32)
    h = x.reshape(T, D)
    for i in range(DEPTH):
        j = i // 2
        mod_l = mod[i]
        if i % 2 == 0:
            q, k, v = _mla_proj(h, mod_l, pos_col, freq_row, mla_w_dq[j], mla_q_norm[j], mla_w_dkv[j],
                                mla_kv_norm[j], mla_w_uq[j], mla_w_ukv[j], S)
            o = _mla_attn(q, k, v, B, S)
            h = _proj_ln(o, mla_w_o[j], h, mod_l, ln_g[i, 0], ln_b[i, 0], S)
            h = _ffn(h, mod_l, ffn_w_gate, ffn_w_up, ffn_w_down, j, ln_g[i, 1], ln_b[i, 1], S)
        else:
            qkv = _qkv(h, mod_l, ca_w_qkv, j, ca_b_qkv[j], S)
            o = _ca_attn(qkv, ca_rel_bias[j], B, S)
            h = _proj_ln(o, ca_w_o[j], h, mod_l, ln_g[i, 0], ln_b[i, 0], S)
            h = _moe_layer(h, mod_l, moe_router_w[j], moe_router_b[j], moe_w_gate, moe_w_up,
                           moe_w_down, j, ln_g[i, 1], ln_b[i, 1], S)
    return h.reshape(B, S, D)
```

```python
import functools

import numpy as np
import jax
import jax.numpy as jnp
from jax import lax
from jax.experimental import pallas as pl
from jax.experimental.pallas import tpu as pltpu

F32 = jnp.float32
BF16 = jnp.bfloat16

CHUNK = 64
MLA_HEADS = 16
QK_NOPE = 128
QK_ROPE = 64
V_HEAD = 128
KV_LORA = 512
ROPE_THETA = 10000.0
CA_HEADS = 16
CA_HEAD_DIM = 128
LEFT_CHUNKS = 8
MAX_REL = 256
N_EXPERTS = 8
TOP_K = 2
N_MOD = 6
LN_EPS = 1e-5
RMS_EPS = 1e-6
DEPTH = 4
DEEPNORM_ALPHA = (2.0 * DEPTH) ** 0.25

LANES = 128
VMEM_LIMIT = 56 * 1024 * 1024

MLA_HEAD_PAD = 2 * LANES
MOE_SLOT_ROWS = 2560
MOE_ROW_BLOCK = 256
CA_QBLOCK = 256
CA_KBLOCKS = 3
ATTN_HEADS_PER_STEP = 4

_NT = (((1,), (1,)), ((), ()))


def _params(*sem):
    return pltpu.CompilerParams(dimension_semantics=sem, vmem_limit_bytes=VMEM_LIMIT)


def _silu(x):
    return x * jax.nn.sigmoid(x)


def _layer_norm(z, g, b):
    mu = jnp.mean(z, axis=-1, keepdims=True)
    zc = z - mu
    var = jnp.mean(zc * zc, axis=-1, keepdims=True)
    return zc * lax.rsqrt(var + LN_EPS) * g + b


def _modulate(h, mod_ref, k):
    return h * (1.0 + mod_ref[0, k + 1:k + 2, :]) + mod_ref[0, k:k + 1, :]


def _cond_kernel(c_ref, w_ref, b_ref, tab_ref, o_ref):
    s = _silu(c_ref[...]).astype(BF16)
    y = jnp.dot(s, w_ref[...].astype(BF16), preferred_element_type=F32) + b_ref[...]
    o_ref[...] = y[None, :, :] + tab_ref[...][:, None, :]


def _cond(c, ada_w, ada_b, ada_table):
    B, D = c.shape
    N = ada_w.shape[1]
    tn = 1024
    c8 = jnp.zeros((8, D), F32).at[:B].set(c)
    out = pl.pallas_call(
        _cond_kernel,
        grid=(N // tn,),
        in_specs=[
            pl.BlockSpec((8, D), lambda n: (0, 0)),
            pl.BlockSpec((D, tn), lambda n: (0, n)),
            pl.BlockSpec((1, tn), lambda n: (0, n)),
            pl.BlockSpec((DEPTH, tn), lambda n: (0, n)),
        ],
        out_specs=pl.BlockSpec((DEPTH, 8, tn), lambda n: (0, 0, n)),
        out_shape=jax.ShapeDtypeStruct((DEPTH, 8, N), F32),
        compiler_params=_params("arbitrary"),
        name="cond",
    )(c8, ada_w, ada_b.reshape(1, N), ada_table.reshape(DEPTH, N))
    mod = out[:, :B].reshape(DEPTH, B, N_MOD, D)
    return jnp.pad(mod, ((0, 0), (0, 0), (0, 8 - N_MOD), (0, 0)))


def _rope_lanes(x, cos, sin_signed):
    lane = lax.broadcasted_iota(jnp.int32, x.shape, 1)
    partner = jnp.where(lane < QK_ROPE // 2, pltpu.roll(x, LANES - QK_ROPE // 2, 1),
                        pltpu.roll(x, QK_ROPE // 2, 1))
    return x * cos + partner * sin_signed


def _mla_proj_kernel(h_ref, mod_ref, pos_ref, freq_ref, wdq_ref, qn_ref, wdkv_ref, kvn_ref, wkr_ref,
                     wuq_ref, wukv_ref, q_ref, k_ref, v_ref):
    u = _modulate(h_ref[...], mod_ref, 0).astype(BF16)
    ang = pos_ref[...].astype(F32) * freq_ref[...]
    cos = jnp.cos(ang)
    sin = jnp.sin(ang)
    lane = lax.broadcasted_iota(jnp.int32, ang.shape, 1)
    sin_signed = jnp.where(lane < QK_ROPE // 2, -sin, sin)

    def rms(x, g):
        return (x * lax.rsqrt(jnp.mean(x * x, axis=-1, keepdims=True) + RMS_EPS)) * g

    cq = rms(jnp.dot(u, wdq_ref[...], preferred_element_type=F32), qn_ref[...]).astype(BF16)
    ckv = rms(jnp.dot(u, wdkv_ref[...], preferred_element_type=F32), kvn_ref[...]).astype(BF16)
    kr = jnp.dot(u, wkr_ref[...], preferred_element_type=F32)
    kr = _rope_lanes(kr, cos, sin_signed).astype(BF16)
    for hd in range(MLA_HEADS):
        c0 = hd * MLA_HEAD_PAD
        qh = jnp.dot(cq, wuq_ref[:, c0:c0 + MLA_HEAD_PAD], preferred_element_type=F32)
        q_ref[:, c0:c0 + LANES] = qh[:, :LANES].astype(BF16)
        q_ref[:, c0 + LANES:c0 + MLA_HEAD_PAD] = _rope_lanes(qh[:, LANES:], cos, sin_signed).astype(BF16)
        kvh = jnp.dot(ckv, wukv_ref[:, c0:c0 + MLA_HEAD_PAD], preferred_element_type=F32)
        k_ref[:, c0:c0 + LANES] = kvh[:, :LANES].astype(BF16)
        k_ref[:, c0 + LANES:c0 + MLA_HEAD_PAD] = kr
        v_ref[:, hd * V_HEAD:(hd + 1) * V_HEAD] = kvh[:, LANES:].astype(BF16)


def _mla_proj(h, mod_l, pos_col, freq_row, w_dq, q_norm, w_dkv, kv_norm, w_uq, w_ukv, S):
    T, D = h.shape
    H = MLA_HEADS
    tm = 256
    q_lora = w_dq.shape[1]
    wdq = w_dq.astype(BF16)
    wdkv = w_dkv[:, :KV_LORA].astype(BF16)
    wkr = jnp.pad(w_dkv[:, KV_LORA:], ((0, 0), (0, LANES - QK_ROPE))).astype(BF16)
    wuq = jnp.pad(w_uq.reshape(q_lora, H, QK_NOPE + QK_ROPE),
                  ((0, 0), (0, 0), (0, MLA_HEAD_PAD - QK_NOPE - QK_ROPE))).reshape(q_lora, H * MLA_HEAD_PAD)
    wuq = wuq.astype(BF16)
    wukv = w_ukv.astype(BF16)
    const = lambda i: (0, 0)
    row = lambda i: (i, 0)
    return pl.pallas_call(
        _mla_proj_kernel,
        grid=(T // tm,),
        in_specs=[
            pl.BlockSpec((tm, D), row),
            pl.BlockSpec((1, 8, D), lambda i: (i * tm // S, 0, 0)),
            pl.BlockSpec((tm, 1), row),
            pl.BlockSpec((1, LANES), const),
            pl.BlockSpec(wdq.shape, const),
            pl.BlockSpec((1, q_lora), const),
            pl.BlockSpec(wdkv.shape, const),
            pl.BlockSpec((1, KV_LORA), const),
            pl.BlockSpec(wkr.shape, const),
            pl.BlockSpec(wuq.shape, const),
            pl.BlockSpec(wukv.shape, const),
        ],
        out_specs=[
            pl.BlockSpec((tm, H * MLA_HEAD_PAD), row),
            pl.BlockSpec((tm, H * MLA_HEAD_PAD), row),
            pl.BlockSpec((tm, H * V_HEAD), row),
        ],
        out_shape=[
            jax.ShapeDtypeStruct((T, H * MLA_HEAD_PAD), BF16),
            jax.ShapeDtypeStruct((T, H * MLA_HEAD_PAD), BF16),
            jax.ShapeDtypeStruct((T, H * V_HEAD), BF16),
        ],
        compiler_params=_params("arbitrary"),
        name="mla_proj",
    )(h, mod_l, pos_col, freq_row, wdq, q_norm.reshape(1, -1), wdkv, kv_norm.reshape(1, -1), wkr, wuq, wukv)


def _with_ones(v):
    return jnp.concatenate([v, jnp.ones_like(v)], axis=1)


def _mla_attn_kernel(pi_ref, pj_ref, q_ref, k_ref, v_ref, o_ref, m_sc, acc_sc, *, scale):
    p = pl.program_id(2)
    i = pi_ref[p]
    j = pj_ref[p]

    @pl.when(j == 0)
    def _():
        m_sc[...] = jnp.full(m_sc.shape, -jnp.inf, F32)
        acc_sc[...] = jnp.zeros(acc_sc.shape, F32)

    def accumulate(diagonal):
        for hd in range(ATTN_HEADS_PER_STEP):
            qk = slice(hd * MLA_HEAD_PAD, (hd + 1) * MLA_HEAD_PAD)
            s = lax.dot_general(q_ref[:, qk], k_ref[:, qk], _NT, preferred_element_type=F32) * scale
            if diagonal:
                qc = lax.broadcasted_iota(jnp.int32, s.shape, 0) // CHUNK
                kc = lax.broadcasted_iota(jnp.int32, s.shape, 1) // CHUNK
                s = jnp.where(kc <= qc, s, -jnp.inf)
            m_prev = m_sc[hd]
            m_new = jnp.maximum(m_prev, jnp.max(s, axis=-1, keepdims=True))
            a = jnp.exp(m_prev - m_new)
            e = jnp.exp(s - m_new).astype(BF16)
            v = _with_ones(v_ref[:, hd * V_HEAD:(hd + 1) * V_HEAD])
            acc_sc[hd] = a * acc_sc[hd] + jnp.dot(e, v, preferred_element_type=F32)
            m_sc[hd] = m_new

    @pl.when(j < i)
    def _():
        accumulate(False)

    @pl.when(j == i)
    def _():
        accumulate(True)
        for hd in range(ATTN_HEADS_PER_STEP):
            acc = acc_sc[hd]
            o_ref[:, hd * V_HEAD:(hd + 1) * V_HEAD] = (acc[:, :V_HEAD] / acc[:, V_HEAD:]).astype(o_ref.dtype)


def _mla_attn(q, k, v, B, S):
    H = MLA_HEADS
    hp = ATTN_HEADS_PER_STEP
    tq = 512
    nq = S // tq
    scale = (QK_NOPE + QK_ROPE) ** -0.5
    pairs = [(i, j) for i in range(nq) for j in range(i + 1)]
    pi = jnp.asarray([p[0] for p in pairs], jnp.int32)
    pj = jnp.asarray([p[1] for p in pairs], jnp.int32)
    return pl.pallas_call(
        functools.partial(_mla_attn_kernel, scale=scale),
        grid_spec=pltpu.PrefetchScalarGridSpec(
            num_scalar_prefetch=2,
            grid=(B, H // hp, len(pairs)),
            in_specs=[
                pl.BlockSpec((tq, hp * MLA_HEAD_PAD), lambda b, g, p, pi, pj: (b * nq + pi[p], g)),
                pl.BlockSpec((tq, hp * MLA_HEAD_PAD), lambda b, g, p, pi, pj: (b * nq + pj[p], g)),
                pl.BlockSpec((tq, hp * V_HEAD), lambda b, g, p, pi, pj: (b * nq + pj[p], g)),
            ],
            out_specs=pl.BlockSpec((tq, hp * V_HEAD), lambda b, g, p, pi, pj: (b * nq + pi[p], g)),
            scratch_shapes=[pltpu.VMEM((hp, tq, 1), F32), pltpu.VMEM((hp, tq, 2 * V_HEAD), F32)],
        ),
        out_shape=jax.ShapeDtypeStruct((B * S, H * V_HEAD), BF16),
        compiler_params=_params("arbitrary", "arbitrary", "arbitrary"),
        name="mla_attn",
    )(pi, pj, q, k, v)


def _proj_ln_kernel(o_ref, w_ref, h_ref, mod_ref, g_ref, b_ref, out_ref, w_sc, *, gate_row):
    @pl.when(pl.program_id(0) == 0)
    def _():
        w_sc[...] = w_ref[...].astype(BF16)

    y = jnp.dot(o_ref[...], w_sc[...], preferred_element_type=F32)
    z = DEEPNORM_ALPHA * h_ref[...] + (1.0 + mod_ref[0, gate_row:gate_row + 1, :]) * y
    out_ref[...] = _layer_norm(z, g_ref[...], b_ref[...])


def _proj_ln(o, w_o, layer, h, mod_l, ln_g, ln_b, S):
    T, D = h.shape
    tm = 512
    K = w_o.shape[1]
    row = lambda i: (i, 0)
    const = lambda i: (0, 0)
    return pl.pallas_call(
        functools.partial(_proj_ln_kernel, gate_row=2),
        grid=(T // tm,),
        in_specs=[
            pl.BlockSpec((tm, K), row),
            pl.BlockSpec((None, K, D), lambda i: (layer, 0, 0), pipeline_mode=pl.Buffered(1)),
            pl.BlockSpec((tm, D), row),
            pl.BlockSpec((1, 8, D), lambda i: (i * tm // S, 0, 0)),
            pl.BlockSpec((1, D), const),
            pl.BlockSpec((1, D), const),
        ],
        out_specs=pl.BlockSpec((tm, D), row),
        out_shape=jax.ShapeDtypeStruct((T, D), F32),
        scratch_shapes=[pltpu.VMEM((K, D), BF16)],
        compiler_params=_params("arbitrary"),
        name="proj_ln",
    )(o, w_o, h, mod_l, ln_g.reshape(1, D), ln_b.reshape(1, D))


def _ffn_kernel(h_ref, mod_ref, wg_ref, wu_ref, wd_ref, g_ref, b_ref, out_ref, u_sc):
    f = pl.program_id(1)

    @pl.when(f == 0)
    def _():
        u_sc[...] = _modulate(h_ref[...], mod_ref, 3).astype(BF16)
        out_ref[...] = jnp.zeros(out_ref.shape, F32)

    u = u_sc[...]
    g = jnp.dot(u, wg_ref[...].astype(BF16), preferred_element_type=F32)
    up = jnp.dot(u, wu_ref[...].astype(BF16), preferred_element_type=F32)
    hm = (_silu(g) * up).astype(BF16)
    out_ref[...] += jnp.dot(hm, wd_ref[...].astype(BF16), preferred_element_type=F32)

    @pl.when(f == pl.num_programs(1) - 1)
    def _():
        z = DEEPNORM_ALPHA * h_ref[...] + (1.0 + mod_ref[0, 5:6, :]) * out_ref[...]
        out_ref[...] = _layer_norm(z, g_ref[...], b_ref[...])


def _ffn(h, mod_l, w_gate, w_up, w_down, layer, ln_g, ln_b, S):
    T, D = h.shape
    F = w_gate.shape[2]
    tm = min(1024, T)
    tf = 256
    row = lambda i, f: (i, 0)
    const = lambda i, f: (0, 0)
    return pl.pallas_call(
        _ffn_kernel,
        grid=(T // tm, F // tf),
        in_specs=[
            pl.BlockSpec((tm, D), row, pipeline_mode=pl.Buffered(1)),
            pl.BlockSpec((1, 8, D), lambda i, f: (i * tm // S, 0, 0)),
            pl.BlockSpec((None, D, tf), lambda i, f: (layer, 0, f)),
            pl.BlockSpec((None, D, tf), lambda i, f: (layer, 0, f)),
            pl.BlockSpec((None, tf, D), lambda i, f: (layer, f, 0)),
            pl.BlockSpec((1, D), const),
            pl.BlockSpec((1, D), const),
        ],
        out_specs=pl.BlockSpec((tm, D), row),
        out_shape=jax.ShapeDtypeStruct((T, D), F32),
        scratch_shapes=[pltpu.VMEM((tm, D), BF16)],
        compiler_params=_params("arbitrary", "arbitrary"),
        name="ffn",
    )(h, mod_l, w_gate, w_up, w_down, ln_g.reshape(1, D), ln_b.reshape(1, D))


def _qkv_kernel(h_ref, mod_ref, w_ref, b_ref, o_ref, u_sc):
    @pl.when(pl.program_id(1) == 0)
    def _():
        u_sc[...] = _modulate(h_ref[...], mod_ref, 0).astype(BF16)

    y = jnp.dot(u_sc[...], w_ref[...], preferred_element_type=F32) + b_ref[...]
    o_ref[...] = y.astype(o_ref.dtype)


def _qkv(h, mod_l, w_qkv, layer, b_qkv, S):
    T, D = h.shape
    N = w_qkv.shape[2]
    tm = min(1024, T)
    tn = 1024
    w = w_qkv[layer].astype(BF16)
    return pl.pallas_call(
        _qkv_kernel,
        grid=(T // tm, N // tn),
        in_specs=[
            pl.BlockSpec((tm, D), lambda i, n: (i, 0), pipeline_mode=pl.Buffered(1)),
            pl.BlockSpec((1, 8, D), lambda i, n: (i * tm // S, 0, 0)),
            pl.BlockSpec((D, tn), lambda i, n: (0, n)),
            pl.BlockSpec((1, tn), lambda i, n: (0, n)),
        ],
        out_specs=pl.BlockSpec((tm, tn), lambda i, n: (i, n)),
        out_shape=jax.ShapeDtypeStruct((T, N), BF16),
        scratch_shapes=[pltpu.VMEM((tm, D), BF16)],
        compiler_params=_params("arbitrary", "arbitrary"),
        name="qkv",
    )(h, mod_l, w, b_qkv.reshape(1, N))


def _ca_attn_kernel(q_ref, k0_ref, k1_ref, k2_ref, v0_ref, v1_ref, v2_ref, bias_ref, o_ref, *, scale):
    i = pl.program_id(2)
    tq = q_ref.shape[0]
    k_refs = (k0_ref, k1_ref, k2_ref)
    v_refs = (v0_ref, v1_ref, v2_ref)
    for hd in range(ATTN_HEADS_PER_STEP):
        cols = slice(hd * CA_HEAD_DIM, (hd + 1) * CA_HEAD_DIM)
        q = q_ref[:, cols]
        s = []
        for p in range(CA_KBLOCKS):
            sp = lax.dot_general(q, k_refs[p][:, cols], _NT, preferred_element_type=F32) * scale
            sp = sp + bias_ref[hd, :, p * tq:(p + 1) * tq]
            first = CA_KBLOCKS - 1 - p
            if first > 0:
                sp = jnp.where(i >= first, sp, -jnp.inf)
            s.append(sp)
        m = functools.reduce(jnp.maximum, [jnp.max(sp, axis=-1, keepdims=True) for sp in s])
        acc = functools.reduce(jnp.add, [
            jnp.dot(jnp.exp(sp - m).astype(BF16), _with_ones(v_refs[p][:, cols]), preferred_element_type=F32)
            for p, sp in enumerate(s)])
        o_ref[:, cols] = (acc[:, :CA_HEAD_DIM] / acc[:, CA_HEAD_DIM:]).astype(o_ref.dtype)


def _ca_bias_kernel(w_ref, o_ref):
    tq, nk = o_ref.shape[1], o_ref.shape[2]
    W = w_ref.shape[2]
    x = jnp.broadcast_to(w_ref[0], (tq, W))
    t = pltpu.roll(x, W - (tq - 1), 1, stride=1, stride_axis=0)[:, :nk]
    qc = lax.broadcasted_iota(jnp.int32, (tq, nk), 0) // CHUNK + (nk - tq) // CHUNK
    kc = lax.broadcasted_iota(jnp.int32, (tq, nk), 1) // CHUNK
    o_ref[0] = jnp.where((kc <= qc) & (kc >= qc - LEFT_CHUNKS), t, -jnp.inf)


def _ca_bias_table(rel_bias):
    H = rel_bias.shape[0]
    tq = CA_QBLOCK
    nk = CA_KBLOCKS * tq
    W = tq + nk
    d = np.arange(W) - (tq - 1)
    rel_idx = np.clip((nk - tq) - d, -MAX_REL, MAX_REL) + MAX_REL
    w = rel_bias[:, rel_idx].astype(F32).reshape(H, 1, W)
    return pl.pallas_call(
        _ca_bias_kernel,
        grid=(H,),
        in_specs=[pl.BlockSpec((1, 1, W), lambda h: (h, 0, 0))],
        out_specs=pl.BlockSpec((1, tq, nk), lambda h: (h, 0, 0)),
        out_shape=jax.ShapeDtypeStruct((H, tq, nk), F32),
        compiler_params=_params("arbitrary"),
        name="ca_bias",
    )(w)


def _ca_attn(qkv, rel_bias, B, S):
    H, Dh = CA_HEADS, CA_HEAD_DIM
    hp = ATTN_HEADS_PER_STEP
    ng = H // hp
    tq = CA_QBLOCK
    nq = S // tq
    bias = _ca_bias_table(rel_bias)

    def kv_spec(which, p):
        back = CA_KBLOCKS - 1 - p
        return pl.BlockSpec((tq, hp * Dh), lambda b, g, i: (b * nq + jnp.maximum(i - back, 0), which * ng + g))

    return pl.pallas_call(
        functools.partial(_ca_attn_kernel, scale=Dh ** -0.5),
        grid=(B, ng, nq),
        in_specs=[pl.BlockSpec((tq, hp * Dh), lambda b, g, i: (b * nq + i, g))]
        + [kv_spec(1, p) for p in range(CA_KBLOCKS)]
        + [kv_spec(2, p) for p in range(CA_KBLOCKS)]
        + [pl.BlockSpec((hp, tq, CA_KBLOCKS * tq), lambda b, g, i: (g, 0, 0))],
        out_specs=pl.BlockSpec((tq, hp * Dh), lambda b, g, i: (b * nq + i, g)),
        out_shape=jax.ShapeDtypeStruct((B * S, H * Dh), BF16),
        compiler_params=_params("arbitrary", "arbitrary", "arbitrary"),
        name="ca_attn",
    )(qkv, qkv, qkv, qkv, qkv, qkv, qkv, bias)


def _router_kernel(h_ref, mod_ref, rw_ref, rb_ref, u_ref, idx_ref, gate_ref, rank_ref, cnt_ref, carry_sc):
    t = pl.program_id(0)

    @pl.when(t == 0)
    def _():
        carry_sc[...] = jnp.zeros(carry_sc.shape, F32)

    u = _modulate(h_ref[...], mod_ref, 3)
    u_ref[...] = u
    tb = u.shape[0]
    logits = lax.dot_general(rw_ref[...], u, _NT, precision=lax.Precision.HIGHEST,
                             preferred_element_type=F32) + rb_ref[...]
    eid = lax.broadcasted_iota(jnp.int32, logits.shape, 0)
    m1 = jnp.max(logits, axis=0, keepdims=True)
    i1 = jnp.min(jnp.where(logits == m1, eid, N_EXPERTS), axis=0, keepdims=True)
    rest = jnp.where(eid == i1, -jnp.inf, logits)
    m2 = jnp.max(rest, axis=0, keepdims=True)
    i2 = jnp.min(jnp.where(rest == m2, eid, N_EXPERTS), axis=0, keepdims=True)
    e2 = jnp.exp(m2 - m1)
    den = 1.0 + e2
    idx_ref[...] = jnp.concatenate([i1, i2], axis=0)
    gate_ref[...] = jnp.concatenate([1.0 / den, e2 / den], axis=0)
    onehot = jnp.where((eid == i1) | (eid == i2), 1.0, 0.0)
    tri = jnp.where(lax.broadcasted_iota(jnp.int32, (tb, tb), 0) <= lax.broadcasted_iota(jnp.int32, (tb, tb), 1),
                    1.0, 0.0).astype(BF16)
    incl = jnp.dot(onehot.astype(BF16), tri, preferred_element_type=F32)
    excl = incl - onehot + carry_sc[:, 0:1]
    r1 = jnp.sum(jnp.where(eid == i1, excl, 0.0), axis=0, keepdims=True)
    r2 = jnp.sum(jnp.where(eid == i2, excl, 0.0), axis=0, keepdims=True)
    rank_ref[...] = jnp.concatenate([r1, r2], axis=0).astype(jnp.int32)
    carry_sc[...] = carry_sc[...] + jnp.sum(onehot, axis=1, keepdims=True)
    cnt_ref[...] = carry_sc[...].astype(jnp.int32)


def _router(h, mod_l, router_w, router_b, S):
    T, D = h.shape
    E = N_EXPERTS
    tb = 512
    return pl.pallas_call(
        _router_kernel,
        grid=(T // tb,),
        in_specs=[
            pl.BlockSpec((tb, D), lambda t: (t, 0)),
            pl.BlockSpec((1, 8, D), lambda t: (t * tb // S, 0, 0)),
            pl.BlockSpec((E, D), lambda t: (0, 0)),
            pl.BlockSpec((E, 1), lambda t: (0, 0)),
        ],
        out_specs=[
            pl.BlockSpec((tb, D), lambda t: (t, 0)),
            pl.BlockSpec((TOP_K, tb), lambda t: (0, t)),
            pl.BlockSpec((TOP_K, tb), lambda t: (0, t)),
            pl.BlockSpec((TOP_K, tb), lambda t: (0, t)),
            pl.BlockSpec((E, LANES), lambda t: (0, 0)),
        ],
        out_shape=[
            jax.ShapeDtypeStruct((T, D), F32),
            jax.ShapeDtypeStruct((TOP_K, T), jnp.int32),
            jax.ShapeDtypeStruct((TOP_K, T), F32),
            jax.ShapeDtypeStruct((TOP_K, T), jnp.int32),
            jax.ShapeDtypeStruct((E, LANES), jnp.int32),
        ],
        scratch_shapes=[pltpu.VMEM((E, LANES), F32)],
        compiler_params=_params("arbitrary"),
        name="router",
    )(h, mod_l, router_w.T, router_b.reshape(E, 1))


SUBLANES = 8


def _row_gather_start(src_hbm, dst, sem, rows_ref, base):
    def issue(g, carry):
        for k in range(SUBLANES):
            r = rows_ref[base + g * SUBLANES + k]
            pltpu.make_async_copy(src_hbm.at[pl.ds(r, 1), :], dst.at[g, pl.ds(k, 1), :], sem).start(priority=k % 2)
        return carry
    lax.fori_loop(0, dst.shape[0], issue, 0)


def _row_gather_wait(dst, sem):
    pltpu.make_async_copy(dst, dst, sem).wait()


def _gathered_rows(buf):
    return buf.reshape(buf.shape[0] * SUBLANES, buf.shape[2])


def _dispatch_kernel(row_tok_ref, blk_ref, nv_ref, u_hbm, o_ref, buf, sem):
    c = pl.program_id(0)
    nv = nv_ref[0]
    rb = MOE_ROW_BLOCK
    slot = c % 2

    def start(step, sl):
        _row_gather_start(u_hbm, buf.at[sl], sem.at[sl], row_tok_ref, blk_ref[step] * rb)

    @pl.when((c == 0) & (nv > 0))
    def _():
        start(0, 0)

    @pl.when(c + 1 < nv)
    def _():
        start(c + 1, 1 - slot)

    @pl.when(c < nv)
    def _():
        _row_gather_wait(buf.at[slot], sem.at[slot])
        o_ref[...] = _gathered_rows(buf[slot]).astype(BF16)

    @pl.when(c >= nv)
    def _():
        o_ref[...] = jnp.zeros(o_ref.shape, BF16)


def _dispatch(u, row_tok, blk_ids, n_valid, n_rows):
    T, D = u.shape
    rb = MOE_ROW_BLOCK
    return pl.pallas_call(
        _dispatch_kernel,
        grid_spec=pltpu.PrefetchScalarGridSpec(
            num_scalar_prefetch=3,
            grid=(blk_ids.shape[0],),
            in_specs=[pl.BlockSpec(memory_space=pl.ANY)],
            out_specs=pl.BlockSpec((rb, D), lambda c, rt, blk, nv: (blk[c], 0)),
            scratch_shapes=[pltpu.VMEM((2, rb // SUBLANES, SUBLANES, D), F32), pltpu.SemaphoreType.DMA((2,))],
        ),
        out_shape=jax.ShapeDtypeStruct((n_rows + rb, D), BF16),
        compiler_params=_params("arbitrary"),
        name="moe_dispatch",
    )(row_tok, blk_ids, n_valid, u)


def _moe_ffn_kernel(se_ref, nblk_ref, x_ref, wg_ref, wu_ref, wd_ref, y_hbm, acc, wg_sc, wu_sc, wd_sc, sem):
    s = pl.program_id(0)
    f = pl.program_id(1)
    nb = nblk_ref[s]
    rb = MOE_ROW_BLOCK

    @pl.when(nb > 0)
    def _():
        @pl.when(f == 0)
        def _():
            acc[...] = jnp.zeros(acc.shape, F32)

        wg_sc[...] = wg_ref[...].astype(BF16)
        wu_sc[...] = wu_ref[...].astype(BF16)
        wd_sc[...] = wd_ref[...].astype(BF16)

        last_f = f == pl.num_programs(1) - 1

        def out_copy(start, size):
            return pltpu.make_async_copy(
                acc.at[pl.ds(pl.multiple_of(start, rb), size), :],
                y_hbm.at[pl.ds(pl.multiple_of(s * MOE_SLOT_ROWS + start, rb), size), :], sem.at[0])

        def rows_at(start, size):
            rows = pl.ds(pl.multiple_of(start, rb), size)
            x = x_ref[rows, :]
            g = jnp.dot(x, wg_sc[...], preferred_element_type=F32)
            up = jnp.dot(x, wu_sc[...], preferred_element_type=F32)
            hm = (_silu(g) * up).astype(BF16)
            acc[rows, :] += jnp.dot(hm, wd_sc[...], preferred_element_type=F32)

            @pl.when(last_f)
            def _():
                out_copy(start, size).start()

        def quad(r, carry):
            rows_at(r * (4 * rb), 4 * rb)
            return carry

        lax.fori_loop(0, nb // 4, quad, 0)

        @pl.when(nb % 4 >= 2)
        def _():
            rows_at((nb // 4) * (4 * rb), 2 * rb)

        @pl.when(nb % 2 == 1)
        def _():
            rows_at((nb - 1) * rb, rb)

        @pl.when(last_f)
        def _():
            def wait(r, carry):
                out_copy(r * rb, rb).wait()
                return carry

            lax.fori_loop(0, nb, wait, 0)


def _moe_ffn(xs, slot_e, nblk, w_gate, w_up, w_down, layer):
    D = xs.shape[1]
    F = w_gate.shape[3]
    R = MOE_SLOT_ROWS
    n_slots = slot_e.shape[0]
    tf = 256
    nf = F // tf

    def f_eff(s, f, nb):
        return jnp.where(nb[s] > 0, f, nf - 1)

    return pl.pallas_call(
        _moe_ffn_kernel,
        grid_spec=pltpu.PrefetchScalarGridSpec(
            num_scalar_prefetch=2,
            grid=(n_slots, nf),
            in_specs=[
                pl.BlockSpec((R, D), lambda s, f, se, nb: (s, 0), pipeline_mode=pl.Buffered(1)),
                pl.BlockSpec((None, None, D, tf), lambda s, f, se, nb: (layer, se[s], 0, f_eff(s, f, nb))),
                pl.BlockSpec((None, None, D, tf), lambda s, f, se, nb: (layer, se[s], 0, f_eff(s, f, nb))),
                pl.BlockSpec((None, None, tf, D), lambda s, f, se, nb: (layer, se[s], f_eff(s, f, nb), 0)),
            ],
            out_specs=pl.BlockSpec(memory_space=pl.ANY),
            scratch_shapes=[pltpu.VMEM((R, D), F32), pltpu.VMEM((D, tf), BF16), pltpu.VMEM((D, tf), BF16),
                            pltpu.VMEM((tf, D), BF16), pltpu.SemaphoreType.DMA((1,))],
        ),
        out_shape=jax.ShapeDtypeStruct((n_slots * R, D), F32),
        compiler_params=_params("arbitrary", "arbitrary"),
        name="moe_ffn",
    )(slot_e, nblk, xs, w_gate, w_up, w_down)


def _combine_kernel(dest_ref, y_hbm, gate_ref, h_ref, mod_ref, g_ref, b_ref, out_ref, buf0, buf1, sem, *, n_tok):
    i = pl.program_id(0)
    tb = out_ref.shape[0]
    slot = i % 2

    def start(step, sl):
        _row_gather_start(y_hbm, buf0.at[sl], sem.at[0, sl], dest_ref, step * tb)
        _row_gather_start(y_hbm, buf1.at[sl], sem.at[1, sl], dest_ref, n_tok + step * tb)

    @pl.when(i == 0)
    def _():
        start(0, 0)

    @pl.when(i + 1 < pl.num_programs(0))
    def _():
        start(i + 1, 1 - slot)

    _row_gather_wait(buf0.at[slot], sem.at[0, slot])
    _row_gather_wait(buf1.at[slot], sem.at[1, slot])
    y = gate_ref[:, 0:1] * _gathered_rows(buf0[slot]) + gate_ref[:, 1:2] * _gathered_rows(buf1[slot])
    z = DEEPNORM_ALPHA * h_ref[...] + (1.0 + mod_ref[0, 5:6, :]) * y
    out_ref[...] = _layer_norm(z, g_ref[...], b_ref[...])


def _combine(y, dest, gates_tk, h, mod_l, ln_g, ln_b, S):
    T, D = h.shape
    tb = 256
    return pl.pallas_call(
        functools.partial(_combine_kernel, n_tok=T),
        grid_spec=pltpu.PrefetchScalarGridSpec(
            num_scalar_prefetch=1,
            grid=(T // tb,),
            in_specs=[
                pl.BlockSpec(memory_space=pl.ANY),
                pl.BlockSpec((tb, TOP_K), lambda i, d: (i, 0)),
                pl.BlockSpec((tb, D), lambda i, d: (i, 0)),
                pl.BlockSpec((1, 8, D), lambda i, d: (i * tb // S, 0, 0)),
                pl.BlockSpec((1, D), lambda i, d: (0, 0)),
                pl.BlockSpec((1, D), lambda i, d: (0, 0)),
            ],
            out_specs=pl.BlockSpec((tb, D), lambda i, d: (i, 0)),
            scratch_shapes=[pltpu.VMEM((2, tb // SUBLANES, SUBLANES, D), F32),
                            pltpu.VMEM((2, tb // SUBLANES, SUBLANES, D), F32),
                            pltpu.SemaphoreType.DMA((2, 2))],
        ),
        out_shape=jax.ShapeDtypeStruct((T, D), F32),
        compiler_params=_params("arbitrary"),
        name="moe_combine",
    )(dest, y, gates_tk, h, mod_l, ln_g.reshape(1, D), ln_b.reshape(1, D))


def _moe_layer(h, mod_l, router_w, router_b, w_gate, w_up, w_down, layer, ln_g, ln_b, S):
    T, D = h.shape
    E, R, rb = N_EXPERTS, MOE_SLOT_ROWS, MOE_ROW_BLOCK
    n_slots = -(-T * TOP_K // R) + E
    n_rows = n_slots * R
    per_slot = R // rb
    max_blocks = T * TOP_K // rb + E
    u, idx, gates, rank, cnt = _router(h, mod_l, router_w, router_b, S)
    cnt = cnt[:, 0]
    slots_e = (cnt + R - 1) // R
    slot_end = jnp.cumsum(slots_e)
    slot_start = slot_end - slots_e
    row_start = slot_start * R
    dest = rank
    for e in range(E):
        dest = dest + jnp.where(idx == e, row_start[e], 0)
    tok = jnp.tile(jnp.arange(T, dtype=jnp.int32), TOP_K)
    row_tok = jnp.zeros((n_rows,), jnp.int32).at[dest.reshape(-1)].set(tok)
    sid = jnp.arange(n_slots, dtype=jnp.int32)
    n_used = slot_end[-1]
    last_used = jnp.maximum(n_used - 1, 0)
    se = jnp.minimum(jnp.searchsorted(slot_end, jnp.minimum(sid, last_used), side='right'), E - 1).astype(jnp.int32)
    rows_in = jnp.clip(cnt[se] - (sid - slot_start[se]) * R, 0, R)
    rows_in = jnp.where(sid < n_used, rows_in, 0)
    nblk = ((rows_in + rb - 1) // rb).astype(jnp.int32)
    blk = jnp.arange(n_rows // rb, dtype=jnp.int32)
    blk_valid = (blk % per_slot) < nblk[blk // per_slot]
    order = jnp.argsort(jnp.logical_not(blk_valid), stable=True).astype(jnp.int32)
    n_valid = jnp.sum(blk_valid.astype(jnp.int32))
    blk_ids = jnp.where(jnp.arange(max_blocks) < n_valid, order[:max_blocks], n_rows // rb).astype(jnp.int32)
    xs = _dispatch(u, row_tok, blk_ids, n_valid.reshape(1), n_rows)
    y = _moe_ffn(xs, se, nblk, w_gate, w_up, w_down, layer)
    return _combine(y, dest.reshape(-1).astype(jnp.int32), gates.T, h, mod_l, ln_g, ln_b, S)


def kernel(x, c, positions, ada_w, ada_b, ada_table, ln_g, ln_b, mla_w_dq, mla_q_norm, mla_w_uq, mla_w_dkv, mla_kv_norm, mla_w_ukv, mla_w_o, ca_w_qkv, ca_b_qkv, ca_rel_bias, ca_w_o, ffn_w_gate, ffn_w_up, ffn_w_down, moe_router_w, moe_router_b, moe_w_gate, moe_w_up, moe_w_down):
    B, S, D = x.shape
    T = B * S
    mod = _cond(c, ada_w, ada_b, ada_table)
    half = QK_ROPE // 2
    inv_freq = ROPE_THETA ** (-jnp.arange(half, dtype=F32) / half)
    freq_row = jnp.concatenate([inv_freq, inv_freq, jnp.zeros((LANES - QK_ROPE,), F32)]).reshape(1, LANES)
    pos_col = positions.reshape(T, 1).astype(jnp.int32)
    h = x.reshape(T, D)
    for i in range(DEPTH):
        j = i // 2
        mod_l = mod[i]
        if i % 2 == 0:
            q, k, v = _mla_proj(h, mod_l, pos_col, freq_row, mla_w_dq[j], mla_q_norm[j], mla_w_dkv[j],
                                mla_kv_norm[j], mla_w_uq[j], mla_w_ukv[j], S)
            o = _mla_attn(q, k, v, B, S)
            h = _proj_ln(o, mla_w_o, j, h, mod_l, ln_g[i, 0], ln_b[i, 0], S)
            h = _ffn(h, mod_l, ffn_w_gate, ffn_w_up, ffn_w_down, j, ln_g[i, 1], ln_b[i, 1], S)
        else:
            qkv = _qkv(h, mod_l, ca_w_qkv, j, ca_b_qkv[j], S)
            o = _ca_attn(qkv, ca_rel_bias[j], B, S)
            h = _proj_ln(o, ca_w_o, j, h, mod_l, ln_g[i, 0], ln_b[i, 0], S)
            h = _moe_layer(h, mod_l, moe_router_w[j], moe_router_b[j], moe_w_gate, moe_w_up,
                           moe_w_down, j, ln_g[i, 1], ln_b[i, 1], S)
    return h.reshape(B, S, D)
```

```python
import functools

import numpy as np
import jax
import jax.numpy as jnp
from jax import lax
from jax.experimental import pallas as pl
from jax.experimental.pallas import tpu as pltpu

F32 = jnp.float32
BF16 = jnp.bfloat16

CHUNK = 64
MLA_HEADS = 16
QK_NOPE = 128
QK_ROPE = 64
V_HEAD = 128
KV_LORA = 512
ROPE_THETA = 10000.0
CA_HEADS = 16
CA_HEAD_DIM = 128
LEFT_CHUNKS = 8
MAX_REL = 256
N_EXPERTS = 8
TOP_K = 2
N_MOD = 6
LN_EPS = 1e-5
RMS_EPS = 1e-6
DEPTH = 4
DEEPNORM_ALPHA = (2.0 * DEPTH) ** 0.25
LOG2E = 1.4426950408889634

LANES = 128
VMEM_LIMIT = 56 * 1024 * 1024

MLA_HEAD_PAD = 2 * LANES
MOE_SLOT_ROWS = 2560
MOE_ROW_BLOCK = 256
CA_QBLOCK = 256
CA_KBLOCKS = 3
ATTN_HEADS_PER_STEP = 8

_NT = (((1,), (1,)), ((), ()))


def _params(*sem):
    return pltpu.CompilerParams(dimension_semantics=sem, vmem_limit_bytes=VMEM_LIMIT)


def _silu(x):
    return x * jax.nn.sigmoid(x)


def _layer_norm(z, g, b):
    mu = jnp.mean(z, axis=-1, keepdims=True)
    zc = z - mu
    var = jnp.mean(zc * zc, axis=-1, keepdims=True)
    return zc * lax.rsqrt(var + LN_EPS) * g + b


def _modulate(h, mod_ref, k):
    return h * (1.0 + mod_ref[0, k + 1:k + 2, :]) + mod_ref[0, k:k + 1, :]


def _cond_kernel(c_ref, w_ref, b_ref, tab_ref, o_ref):
    s = _silu(c_ref[...]).astype(BF16)
    y = jnp.dot(s, w_ref[...].astype(BF16), preferred_element_type=F32) + b_ref[...]
    o_ref[...] = y[None, :, :] + tab_ref[...][:, None, :]


def _cond(c, ada_w, ada_b, ada_table):
    B, D = c.shape
    N = ada_w.shape[1]
    tn = 1024
    c8 = jnp.zeros((8, D), F32).at[:B].set(c)
    out = pl.pallas_call(
        _cond_kernel,
        grid=(N // tn,),
        in_specs=[
            pl.BlockSpec((8, D), lambda n: (0, 0)),
            pl.BlockSpec((D, tn), lambda n: (0, n)),
            pl.BlockSpec((1, tn), lambda n: (0, n)),
            pl.BlockSpec((DEPTH, tn), lambda n: (0, n)),
        ],
        out_specs=pl.BlockSpec((DEPTH, 8, tn), lambda n: (0, 0, n)),
        out_shape=jax.ShapeDtypeStruct((DEPTH, 8, N), F32),
        compiler_params=_params("arbitrary"),
        name="cond",
    )(c8, ada_w, ada_b.reshape(1, N), ada_table.reshape(DEPTH, N))
    mod = out[:, :B].reshape(DEPTH, B, N_MOD, D)
    return jnp.pad(mod, ((0, 0), (0, 0), (0, 8 - N_MOD), (0, 0)))


def _rope_lanes(x, cos, sin_signed):
    lane = lax.broadcasted_iota(jnp.int32, x.shape, 1)
    partner = jnp.where(lane < QK_ROPE // 2, pltpu.roll(x, LANES - QK_ROPE // 2, 1),
                        pltpu.roll(x, QK_ROPE // 2, 1))
    return x * cos + partner * sin_signed


def _mla_proj_kernel(h_ref, mod_ref, pos_ref, freq_ref, wdq_ref, qn_ref, wdkv_ref, kvn_ref, wkr_ref,
                     wuq_ref, wukv_ref, q_ref, k_ref, v_ref):
    u = _modulate(h_ref[...], mod_ref, 0).astype(BF16)
    ang = pos_ref[...].astype(F32) * freq_ref[...]
    cos = jnp.cos(ang)
    sin = jnp.sin(ang)
    lane = lax.broadcasted_iota(jnp.int32, ang.shape, 1)
    sin_signed = jnp.where(lane < QK_ROPE // 2, -sin, sin)

    def rms(x, g):
        return (x * lax.rsqrt(jnp.mean(x * x, axis=-1, keepdims=True) + RMS_EPS)) * g

    cq = rms(jnp.dot(u, wdq_ref[...], preferred_element_type=F32), qn_ref[...]).astype(BF16)
    ckv = rms(jnp.dot(u, wdkv_ref[...], preferred_element_type=F32), kvn_ref[...]).astype(BF16)
    kr = jnp.dot(u, wkr_ref[...], preferred_element_type=F32)
    kr = _rope_lanes(kr, cos, sin_signed).astype(BF16)
    for hd in range(MLA_HEADS):
        c0 = hd * MLA_HEAD_PAD
        qh = jnp.dot(cq, wuq_ref[:, c0:c0 + MLA_HEAD_PAD], preferred_element_type=F32)
        q_ref[:, c0:c0 + LANES] = qh[:, :LANES].astype(BF16)
        q_ref[:, c0 + LANES:c0 + MLA_HEAD_PAD] = _rope_lanes(qh[:, LANES:], cos, sin_signed).astype(BF16)
        kvh = jnp.dot(ckv, wukv_ref[:, c0:c0 + MLA_HEAD_PAD], preferred_element_type=F32)
        k_ref[:, c0:c0 + LANES] = kvh[:, :LANES].astype(BF16)
        k_ref[:, c0 + LANES:c0 + MLA_HEAD_PAD] = kr
        v_ref[:, hd * V_HEAD:(hd + 1) * V_HEAD] = kvh[:, LANES:].astype(BF16)


def _mla_proj(h, mod_l, pos_col, freq_row, w_dq, q_norm, w_dkv, kv_norm, w_uq, w_ukv, S):
    T, D = h.shape
    H = MLA_HEADS
    tm = 256
    q_lora = w_dq.shape[1]
    wdq = w_dq.astype(BF16)
    wdkv = w_dkv[:, :KV_LORA].astype(BF16)
    wkr = jnp.pad(w_dkv[:, KV_LORA:], ((0, 0), (0, LANES - QK_ROPE))).astype(BF16)
    wuq = jnp.pad(w_uq.reshape(q_lora, H, QK_NOPE + QK_ROPE),
                  ((0, 0), (0, 0), (0, MLA_HEAD_PAD - QK_NOPE - QK_ROPE))).reshape(q_lora, H * MLA_HEAD_PAD)
    wuq = wuq.astype(BF16)
    wukv = w_ukv.astype(BF16)
    const = lambda i: (0, 0)
    row = lambda i: (i, 0)
    return pl.pallas_call(
        _mla_proj_kernel,
        grid=(T // tm,),
        in_specs=[
            pl.BlockSpec((tm, D), row),
            pl.BlockSpec((1, 8, D), lambda i: (i * tm // S, 0, 0)),
            pl.BlockSpec((tm, 1), row),
            pl.BlockSpec((1, LANES), const),
            pl.BlockSpec(wdq.shape, const),
            pl.BlockSpec((1, q_lora), const),
            pl.BlockSpec(wdkv.shape, const),
            pl.BlockSpec((1, KV_LORA), const),
            pl.BlockSpec(wkr.shape, const),
            pl.BlockSpec(wuq.shape, const),
            pl.BlockSpec(wukv.shape, const),
        ],
        out_specs=[
            pl.BlockSpec((tm, H * MLA_HEAD_PAD), row),
            pl.BlockSpec((tm, H * MLA_HEAD_PAD), row),
            pl.BlockSpec((tm, H * V_HEAD), row),
        ],
        out_shape=[
            jax.ShapeDtypeStruct((T, H * MLA_HEAD_PAD), BF16),
            jax.ShapeDtypeStruct((T, H * MLA_HEAD_PAD), BF16),
            jax.ShapeDtypeStruct((T, H * V_HEAD), BF16),
        ],
        compiler_params=_params("arbitrary"),
        name="mla_proj",
    )(h, mod_l, pos_col, freq_row, wdq, q_norm.reshape(1, -1), wdkv, kv_norm.reshape(1, -1), wkr, wuq, wukv)


def _with_ones(v):
    return jnp.concatenate([v, jnp.ones_like(v)], axis=1)


def _mla_attn_kernel(pi_ref, pj_ref, q_ref, k_ref, v_ref, o_ref, m_sc, acc_sc, *, scale):
    p = pl.program_id(2)
    i = pi_ref[p]
    j = pj_ref[p]

    @pl.when(j == 0)
    def _():
        m_sc[...] = jnp.full(m_sc.shape, -jnp.inf, F32)
        acc_sc[...] = jnp.zeros(acc_sc.shape, F32)

    def accumulate(diagonal):
        for hd in range(ATTN_HEADS_PER_STEP):
            qk = slice(hd * MLA_HEAD_PAD, (hd + 1) * MLA_HEAD_PAD)
            s = lax.dot_general(q_ref[:, qk], k_ref[:, qk], _NT, preferred_element_type=F32) * scale
            if diagonal:
                qc = lax.broadcasted_iota(jnp.int32, s.shape, 0) // CHUNK
                kc = lax.broadcasted_iota(jnp.int32, s.shape, 1) // CHUNK
                s = jnp.where(kc <= qc, s, -jnp.inf)
            m_prev = m_sc[hd]
            m_new = jnp.maximum(m_prev, jnp.max(s, axis=-1, keepdims=True))
            a = jnp.exp2(m_prev - m_new)
            e = jnp.exp2(s - m_new).astype(BF16)
            v = _with_ones(v_ref[:, hd * V_HEAD:(hd + 1) * V_HEAD])
            acc_sc[hd] = a * acc_sc[hd] + jnp.dot(e, v, preferred_element_type=F32)
            m_sc[hd] = m_new

    @pl.when(j < i)
    def _():
        accumulate(False)

    @pl.when(j == i)
    def _():
        accumulate(True)
        for hd in range(ATTN_HEADS_PER_STEP):
            acc = acc_sc[hd]
            o_ref[:, hd * V_HEAD:(hd + 1) * V_HEAD] = (acc[:, :V_HEAD] / acc[:, V_HEAD:]).astype(o_ref.dtype)


def _mla_attn(q, k, v, B, S):
    H = MLA_HEADS
    hp = ATTN_HEADS_PER_STEP
    tq = 512
    nq = S // tq
    scale = (QK_NOPE + QK_ROPE) ** -0.5 * LOG2E
    pairs = [(i, j) for i in range(nq) for j in range(i + 1)]
    pi = jnp.asarray([p[0] for p in pairs], jnp.int32)
    pj = jnp.asarray([p[1] for p in pairs], jnp.int32)
    return pl.pallas_call(
        functools.partial(_mla_attn_kernel, scale=scale),
        grid_spec=pltpu.PrefetchScalarGridSpec(
            num_scalar_prefetch=2,
            grid=(B, H // hp, len(pairs)),
            in_specs=[
                pl.BlockSpec((tq, hp * MLA_HEAD_PAD), lambda b, g, p, pi, pj: (b * nq + pi[p], g)),
                pl.BlockSpec((tq, hp * MLA_HEAD_PAD), lambda b, g, p, pi, pj: (b * nq + pj[p], g)),
                pl.BlockSpec((tq, hp * V_HEAD), lambda b, g, p, pi, pj: (b * nq + pj[p], g)),
            ],
            out_specs=pl.BlockSpec((tq, hp * V_HEAD), lambda b, g, p, pi, pj: (b * nq + pi[p], g)),
            scratch_shapes=[pltpu.VMEM((hp, tq, 1), F32), pltpu.VMEM((hp, tq, 2 * V_HEAD), F32)],
        ),
        out_shape=jax.ShapeDtypeStruct((B * S, H * V_HEAD), BF16),
        compiler_params=_params("arbitrary", "arbitrary", "arbitrary"),
        name="mla_attn",
    )(pi, pj, q, k, v)


def _proj_ln_kernel(o_ref, w_ref, h_ref, mod_ref, g_ref, b_ref, out_ref, w_sc, *, gate_row):
    @pl.when(pl.program_id(0) == 0)
    def _():
        w_sc[...] = w_ref[...].astype(BF16)

    half = out_ref.shape[0] // 2
    for part in range(2):
        rows = slice(part * half, (part + 1) * half)
        y = jnp.dot(o_ref[rows, :], w_sc[...], preferred_element_type=F32)
        z = DEEPNORM_ALPHA * h_ref[rows, :] + (1.0 + mod_ref[0, gate_row:gate_row + 1, :]) * y
        out_ref[rows, :] = _layer_norm(z, g_ref[...], b_ref[...])


def _proj_ln(o, w_o, layer, h, mod_l, ln_g, ln_b, S):
    T, D = h.shape
    tm = 512
    K = w_o.shape[1]
    row = lambda i: (i, 0)
    const = lambda i: (0, 0)
    return pl.pallas_call(
        functools.partial(_proj_ln_kernel, gate_row=2),
        grid=(T // tm,),
        in_specs=[
            pl.BlockSpec((tm, K), row),
            pl.BlockSpec((None, K, D), lambda i: (layer, 0, 0), pipeline_mode=pl.Buffered(1)),
            pl.BlockSpec((tm, D), row),
            pl.BlockSpec((1, 8, D), lambda i: (i * tm // S, 0, 0)),
            pl.BlockSpec((1, D), const),
            pl.BlockSpec((1, D), const),
        ],
        out_specs=pl.BlockSpec((tm, D), row),
        out_shape=jax.ShapeDtypeStruct((T, D), F32),
        scratch_shapes=[pltpu.VMEM((K, D), BF16)],
        compiler_params=_params("arbitrary"),
        name="proj_ln",
    )(o, w_o, h, mod_l, ln_g.reshape(1, D), ln_b.reshape(1, D))


def _ffn_kernel(h_ref, mod_ref, wg_ref, wu_ref, wd_ref, g_ref, b_ref, out_ref, u_sc):
    f = pl.program_id(1)

    @pl.when(f == 0)
    def _():
        u_sc[...] = _modulate(h_ref[...], mod_ref, 3).astype(BF16)
        out_ref[...] = jnp.zeros(out_ref.shape, F32)

    u = u_sc[...]
    g = jnp.dot(u, wg_ref[...].astype(BF16), preferred_element_type=F32)
    up = jnp.dot(u, wu_ref[...].astype(BF16), preferred_element_type=F32)
    hm = (_silu(g) * up).astype(BF16)
    out_ref[...] += jnp.dot(hm, wd_ref[...].astype(BF16), preferred_element_type=F32)

    @pl.when(f == pl.num_programs(1) - 1)
    def _():
        z = DEEPNORM_ALPHA * h_ref[...] + (1.0 + mod_ref[0, 5:6, :]) * out_ref[...]
        out_ref[...] = _layer_norm(z, g_ref[...], b_ref[...])


def _ffn(h, mod_l, w_gate, w_up, w_down, layer, ln_g, ln_b, S):
    T, D = h.shape
    F = w_gate.shape[2]
    tm = min(1024, T)
    tf = 256
    row = lambda i, f: (i, 0)
    const = lambda i, f: (0, 0)
    return pl.pallas_call(
        _ffn_kernel,
        grid=(T // tm, F // tf),
        in_specs=[
            pl.BlockSpec((tm, D), row, pipeline_mode=pl.Buffered(1)),
            pl.BlockSpec((1, 8, D), lambda i, f: (i * tm // S, 0, 0)),
            pl.BlockSpec((None, D, tf), lambda i, f: (layer, 0, f)),
            pl.BlockSpec((None, D, tf), lambda i, f: (layer, 0, f)),
            pl.BlockSpec((None, tf, D), lambda i, f: (layer, f, 0)),
            pl.BlockSpec((1, D), const),
            pl.BlockSpec((1, D), const),
        ],
        out_specs=pl.BlockSpec((tm, D), row),
        out_shape=jax.ShapeDtypeStruct((T, D), F32),
        scratch_shapes=[pltpu.VMEM((tm, D), BF16)],
        compiler_params=_params("arbitrary", "arbitrary"),
        name="ffn",
    )(h, mod_l, w_gate, w_up, w_down, ln_g.reshape(1, D), ln_b.reshape(1, D))


def _qkv_kernel(h_ref, mod_ref, w_ref, b_ref, o_ref, u_sc):
    @pl.when(pl.program_id(1) == 0)
    def _():
        u_sc[...] = _modulate(h_ref[...], mod_ref, 0).astype(BF16)

    y = jnp.dot(u_sc[...], w_ref[...], preferred_element_type=F32) + b_ref[...]
    o_ref[...] = y.astype(o_ref.dtype)


def _qkv(h, mod_l, w_qkv, layer, b_qkv, S):
    T, D = h.shape
    N = w_qkv.shape[2]
    tm = min(1024, T)
    tn = 1024
    w = w_qkv[layer].astype(BF16)
    return pl.pallas_call(
        _qkv_kernel,
        grid=(T // tm, N // tn),
        in_specs=[
            pl.BlockSpec((tm, D), lambda i, n: (i, 0), pipeline_mode=pl.Buffered(1)),
            pl.BlockSpec((1, 8, D), lambda i, n: (i * tm // S, 0, 0)),
            pl.BlockSpec((D, tn), lambda i, n: (0, n)),
            pl.BlockSpec((1, tn), lambda i, n: (0, n)),
        ],
        out_specs=pl.BlockSpec((tm, tn), lambda i, n: (i, n)),
        out_shape=jax.ShapeDtypeStruct((T, N), BF16),
        scratch_shapes=[pltpu.VMEM((tm, D), BF16)],
        compiler_params=_params("arbitrary", "arbitrary"),
        name="qkv",
    )(h, mod_l, w, b_qkv.reshape(1, N))


def _ca_attn_kernel(q_ref, k0_ref, k1_ref, k2_ref, v0_ref, v1_ref, v2_ref, bias_ref, o_ref, *, scale):
    i = pl.program_id(2)
    tq = q_ref.shape[0]
    k_refs = (k0_ref, k1_ref, k2_ref)
    v_refs = (v0_ref, v1_ref, v2_ref)

    def attend(stream_start):
        for hd in range(ATTN_HEADS_PER_STEP):
            cols = slice(hd * CA_HEAD_DIM, (hd + 1) * CA_HEAD_DIM)
            q = q_ref[:, cols]
            s = []
            for p in range(CA_KBLOCKS):
                sp = lax.dot_general(q, k_refs[p][:, cols], _NT, preferred_element_type=F32) * scale
                sp = sp + bias_ref[hd, :, p * tq:(p + 1) * tq]
                first = CA_KBLOCKS - 1 - p
                if stream_start and first > 0:
                    sp = jnp.where(i >= first, sp, -jnp.inf)
                s.append(sp)
            m = functools.reduce(jnp.maximum, [jnp.max(sp, axis=-1, keepdims=True) for sp in s])
            acc = functools.reduce(jnp.add, [
                jnp.dot(jnp.exp2(sp - m).astype(BF16), _with_ones(v_refs[p][:, cols]), preferred_element_type=F32)
                for p, sp in enumerate(s)])
            o_ref[:, cols] = (acc[:, :CA_HEAD_DIM] / acc[:, CA_HEAD_DIM:]).astype(o_ref.dtype)

    @pl.when(i < CA_KBLOCKS - 1)
    def _():
        attend(True)

    @pl.when(i >= CA_KBLOCKS - 1)
    def _():
        attend(False)


def _ca_bias_kernel(w_ref, o_ref):
    tq, nk = o_ref.shape[1], o_ref.shape[2]
    W = w_ref.shape[2]
    x = jnp.broadcast_to(w_ref[0], (tq, W))
    t = pltpu.roll(x, W - (tq - 1), 1, stride=1, stride_axis=0)[:, :nk]
    qc = lax.broadcasted_iota(jnp.int32, (tq, nk), 0) // CHUNK + (nk - tq) // CHUNK
    kc = lax.broadcasted_iota(jnp.int32, (tq, nk), 1) // CHUNK
    o_ref[0] = jnp.where((kc <= qc) & (kc >= qc - LEFT_CHUNKS), t * LOG2E, -jnp.inf)


def _ca_bias_table(rel_bias):
    H = rel_bias.shape[0]
    tq = CA_QBLOCK
    nk = CA_KBLOCKS * tq
    W = tq + nk
    d = np.arange(W) - (tq - 1)
    rel_idx = np.clip((nk - tq) - d, -MAX_REL, MAX_REL) + MAX_REL
    w = rel_bias[:, rel_idx].astype(F32).reshape(H, 1, W)
    return pl.pallas_call(
        _ca_bias_kernel,
        grid=(H,),
        in_specs=[pl.BlockSpec((1, 1, W), lambda h: (h, 0, 0))],
        out_specs=pl.BlockSpec((1, tq, nk), lambda h: (h, 0, 0)),
        out_shape=jax.ShapeDtypeStruct((H, tq, nk), F32),
        compiler_params=_params("arbitrary"),
        name="ca_bias",
    )(w)


def _ca_attn(qkv, rel_bias, B, S):
    H, Dh = CA_HEADS, CA_HEAD_DIM
    hp = ATTN_HEADS_PER_STEP
    ng = H // hp
    tq = CA_QBLOCK
    nq = S // tq
    bias = _ca_bias_table(rel_bias)

    def kv_spec(which, p):
        back = CA_KBLOCKS - 1 - p
        return pl.BlockSpec((tq, hp * Dh), lambda b, g, i: (b * nq + jnp.maximum(i - back, 0), which * ng + g))

    return pl.pallas_call(
        functools.partial(_ca_attn_kernel, scale=Dh ** -0.5 * LOG2E),
        grid=(B, ng, nq),
        in_specs=[pl.BlockSpec((tq, hp * Dh), lambda b, g, i: (b * nq + i, g))]
        + [kv_spec(1, p) for p in range(CA_KBLOCKS)]
        + [kv_spec(2, p) for p in range(CA_KBLOCKS)]
        + [pl.BlockSpec((hp, tq, CA_KBLOCKS * tq), lambda b, g, i: (g, 0, 0))],
        out_specs=pl.BlockSpec((tq, hp * Dh), lambda b, g, i: (b * nq + i, g)),
        out_shape=jax.ShapeDtypeStruct((B * S, H * Dh), BF16),
        compiler_params=_params("arbitrary", "arbitrary", "arbitrary"),
        name="ca_attn",
    )(qkv, qkv, qkv, qkv, qkv, qkv, qkv, bias)


def _router_kernel(h_ref, mod_ref, rw_ref, rb_ref, u_ref, idx_ref, gate_ref, rank_ref, cnt_ref, carry_sc):
    t = pl.program_id(0)

    @pl.when(t == 0)
    def _():
        carry_sc[...] = jnp.zeros(carry_sc.shape, F32)

    u = _modulate(h_ref[...], mod_ref, 3)
    u_ref[...] = u
    tb = u.shape[0]
    logits = lax.dot_general(rw_ref[...], u, _NT, precision=lax.Precision.HIGHEST,
                             preferred_element_type=F32) + rb_ref[...]
    eid = lax.broadcasted_iota(jnp.int32, logits.shape, 0)
    m1 = jnp.max(logits, axis=0, keepdims=True)
    i1 = jnp.min(jnp.where(logits == m1, eid, N_EXPERTS), axis=0, keepdims=True)
    rest = jnp.where(eid == i1, -jnp.inf, logits)
    m2 = jnp.max(rest, axis=0, keepdims=True)
    i2 = jnp.min(jnp.where(rest == m2, eid, N_EXPERTS), axis=0, keepdims=True)
    e2 = jnp.exp(m2 - m1)
    den = 1.0 + e2
    idx_ref[...] = jnp.concatenate([i1, i2], axis=0)
    gate_ref[...] = jnp.concatenate([1.0 / den, e2 / den], axis=0)
    onehot = jnp.where((eid == i1) | (eid == i2), 1.0, 0.0)
    tri = jnp.where(lax.broadcasted_iota(jnp.int32, (tb, tb), 0) <= lax.broadcasted_iota(jnp.int32, (tb, tb), 1),
                    1.0, 0.0).astype(BF16)
    incl = jnp.dot(onehot.astype(BF16), tri, preferred_element_type=F32)
    excl = incl - onehot + carry_sc[:, 0:1]
    r1 = jnp.sum(jnp.where(eid == i1, excl, 0.0), axis=0, keepdims=True)
    r2 = jnp.sum(jnp.where(eid == i2, excl, 0.0), axis=0, keepdims=True)
    rank_ref[...] = jnp.concatenate([r1, r2], axis=0).astype(jnp.int32)
    carry_sc[...] = carry_sc[...] + jnp.sum(onehot, axis=1, keepdims=True)
    cnt_ref[...] = carry_sc[...].astype(jnp.int32)


def _router(h, mod_l, router_w, router_b, S):
    T, D = h.shape
    E = N_EXPERTS
    tb = 512
    return pl.pallas_call(
        _router_kernel,
        grid=(T // tb,),
        in_specs=[
            pl.BlockSpec((tb, D), lambda t: (t, 0)),
            pl.BlockSpec((1, 8, D), lambda t: (t * tb // S, 0, 0)),
            pl.BlockSpec((E, D), lambda t: (0, 0)),
            pl.BlockSpec((E, 1), lambda t: (0, 0)),
        ],
        out_specs=[
            pl.BlockSpec((tb, D), lambda t: (t, 0)),
            pl.BlockSpec((TOP_K, tb), lambda t: (0, t)),
            pl.BlockSpec((TOP_K, tb), lambda t: (0, t)),
            pl.BlockSpec((TOP_K, tb), lambda t: (0, t)),
            pl.BlockSpec((E, LANES), lambda t: (0, 0)),
        ],
        out_shape=[
            jax.ShapeDtypeStruct((T, D), F32),
            jax.ShapeDtypeStruct((TOP_K, T), jnp.int32),
            jax.ShapeDtypeStruct((TOP_K, T), F32),
            jax.ShapeDtypeStruct((TOP_K, T), jnp.int32),
            jax.ShapeDtypeStruct((E, LANES), jnp.int32),
        ],
        scratch_shapes=[pltpu.VMEM((E, LANES), F32)],
        compiler_params=_params("arbitrary"),
        name="router",
    )(h, mod_l, router_w.T, router_b.reshape(E, 1))


SUBLANES = 8


def _row_gather_group(src_hbm, dst, sem, rows_ref, base, g):
    for k in range(SUBLANES):
        r = rows_ref[base + g * SUBLANES + k]
        pltpu.make_async_copy(src_hbm.at[pl.ds(r, 1), :], dst.at[g, pl.ds(k, 1), :], sem).start(priority=k % 2)


def _row_gather_start(src_hbm, dst, sem, rows_ref, base):
    def issue(g, carry):
        _row_gather_group(src_hbm, dst, sem, rows_ref, base, g)
        return carry
    lax.fori_loop(0, dst.shape[0], issue, 0)


def _row_gather_wait(dst, sem):
    pltpu.make_async_copy(dst, dst, sem).wait()


def _gathered_rows(buf):
    return buf.reshape(buf.shape[0] * SUBLANES, buf.shape[2])


def _dispatch_kernel(row_tok_ref, blk_ref, nv_ref, u_hbm, o_ref, buf, sem):
    c = pl.program_id(0)
    nv = nv_ref[0]
    rb = MOE_ROW_BLOCK
    slot = c % 2

    def start(step, sl):
        _row_gather_start(u_hbm, buf.at[sl], sem.at[sl], row_tok_ref, blk_ref[step] * rb)

    @pl.when((c == 0) & (nv > 0))
    def _():
        start(0, 0)

    @pl.when(c + 1 < nv)
    def _():
        start(c + 1, 1 - slot)

    @pl.when(c < nv)
    def _():
        _row_gather_wait(buf.at[slot], sem.at[slot])
        o_ref[...] = _gathered_rows(buf[slot]).astype(BF16)

    @pl.when(c >= nv)
    def _():
        o_ref[...] = jnp.zeros(o_ref.shape, BF16)


def _dispatch(u, row_tok, blk_ids, n_valid, n_rows):
    T, D = u.shape
    rb = MOE_ROW_BLOCK
    return pl.pallas_call(
        _dispatch_kernel,
        grid_spec=pltpu.PrefetchScalarGridSpec(
            num_scalar_prefetch=3,
            grid=(blk_ids.shape[0],),
            in_specs=[pl.BlockSpec(memory_space=pl.ANY)],
            out_specs=pl.BlockSpec((rb, D), lambda c, rt, blk, nv: (blk[c], 0)),
            scratch_shapes=[pltpu.VMEM((2, rb // SUBLANES, SUBLANES, D), F32), pltpu.SemaphoreType.DMA((2,))],
        ),
        out_shape=jax.ShapeDtypeStruct((n_rows + rb, D), BF16),
        compiler_params=_params("arbitrary"),
        name="moe_dispatch",
    )(row_tok, blk_ids, n_valid, u)


def _moe_ffn_kernel(se_ref, nblk_ref, x_ref, wg_ref, wu_ref, wd_ref, y_hbm, acc, wg_sc, wu_sc, wd_sc, sem):
    s = pl.program_id(0)
    f = pl.program_id(1)
    nb = nblk_ref[s]
    rb = MOE_ROW_BLOCK

    @pl.when(nb > 0)
    def _():
        @pl.when(f == 0)
        def _():
            acc[...] = jnp.zeros(acc.shape, F32)

        wg_sc[...] = wg_ref[...].astype(BF16)
        wu_sc[...] = wu_ref[...].astype(BF16)
        wd_sc[...] = wd_ref[...].astype(BF16)

        last_f = f == pl.num_programs(1) - 1

        def out_copy(start, size):
            return pltpu.make_async_copy(
                acc.at[pl.ds(pl.multiple_of(start, rb), size), :],
                y_hbm.at[pl.ds(pl.multiple_of(s * MOE_SLOT_ROWS + start, rb), size), :], sem.at[0])

        def rows_at(start, size):
            rows = pl.ds(pl.multiple_of(start, rb), size)
            x = x_ref[rows, :]
            g = jnp.dot(x, wg_sc[...], preferred_element_type=F32)
            up = jnp.dot(x, wu_sc[...], preferred_element_type=F32)
            hm = (_silu(g) * up).astype(BF16)
            acc[rows, :] += jnp.dot(hm, wd_sc[...], preferred_element_type=F32)

            @pl.when(last_f)
            def _():
                out_copy(start, size).start()

        def quad(r, carry):
            rows_at(r * (4 * rb), 4 * rb)
            return carry

        lax.fori_loop(0, nb // 4, quad, 0)

        @pl.when(nb % 4 >= 2)
        def _():
            rows_at((nb // 4) * (4 * rb), 2 * rb)

        @pl.when(nb % 2 == 1)
        def _():
            rows_at((nb - 1) * rb, rb)

        @pl.when(last_f)
        def _():
            def wait(r, carry):
                out_copy(r * rb, rb).wait()
                return carry

            lax.fori_loop(0, nb, wait, 0)


def _moe_ffn(xs, slot_e, nblk, w_gate, w_up, w_down, layer):
    D = xs.shape[1]
    F = w_gate.shape[3]
    R = MOE_SLOT_ROWS
    n_slots = slot_e.shape[0]
    tf = 256
    nf = F // tf

    def f_eff(s, f, nb):
        return jnp.where(nb[s] > 0, f, nf - 1)

    return pl.pallas_call(
        _moe_ffn_kernel,
        grid_spec=pltpu.PrefetchScalarGridSpec(
            num_scalar_prefetch=2,
            grid=(n_slots, nf),
            in_specs=[
                pl.BlockSpec((R, D), lambda s, f, se, nb: (s, 0), pipeline_mode=pl.Buffered(1)),
                pl.BlockSpec((None, None, D, tf), lambda s, f, se, nb: (layer, se[s], 0, f_eff(s, f, nb))),
                pl.BlockSpec((None, None, D, tf), lambda s, f, se, nb: (layer, se[s], 0, f_eff(s, f, nb))),
                pl.BlockSpec((None, None, tf, D), lambda s, f, se, nb: (layer, se[s], f_eff(s, f, nb), 0)),
            ],
            out_specs=pl.BlockSpec(memory_space=pl.ANY),
            scratch_shapes=[pltpu.VMEM((R, D), F32), pltpu.VMEM((D, tf), BF16), pltpu.VMEM((D, tf), BF16),
                            pltpu.VMEM((tf, D), BF16), pltpu.SemaphoreType.DMA((1,))],
        ),
        out_shape=jax.ShapeDtypeStruct((n_slots * R, D), F32),
        compiler_params=_params("arbitrary", "arbitrary"),
        name="moe_ffn",
    )(slot_e, nblk, xs, w_gate, w_up, w_down)


def _combine_kernel(dest_ref, y_hbm, gate_ref, h_ref, mod_ref, g_ref, b_ref, out_ref, buf0, buf1, sem, *, n_tok):
    i = pl.program_id(0)
    tb = out_ref.shape[0]
    slot = i % 2

    def start(step, sl):
        _row_gather_start(y_hbm, buf0.at[sl], sem.at[0, sl], dest_ref, step * tb)
        _row_gather_start(y_hbm, buf1.at[sl], sem.at[1, sl], dest_ref, n_tok + step * tb)

    @pl.when(i == 0)
    def _():
        start(0, 0)

    @pl.when(i + 1 < pl.num_programs(0))
    def _():
        start(i + 1, 1 - slot)

    _row_gather_wait(buf0.at[slot], sem.at[0, slot])
    _row_gather_wait(buf1.at[slot], sem.at[1, slot])
    y = gate_ref[:, 0:1] * _gathered_rows(buf0[slot]) + gate_ref[:, 1:2] * _gathered_rows(buf1[slot])
    z = DEEPNORM_ALPHA * h_ref[...] + (1.0 + mod_ref[0, 5:6, :]) * y
    out_ref[...] = _layer_norm(z, g_ref[...], b_ref[...])


def _combine(y, dest, gates_tk, h, mod_l, ln_g, ln_b, S):
    T, D = h.shape
    tb = 256
    return pl.pallas_call(
        functools.partial(_combine_kernel, n_tok=T),
        grid_spec=pltpu.PrefetchScalarGridSpec(
            num_scalar_prefetch=1,
            grid=(T // tb,),
            in_specs=[
                pl.BlockSpec(memory_space=pl.ANY),
                pl.BlockSpec((tb, TOP_K), lambda i, d: (i, 0)),
                pl.BlockSpec((tb, D), lambda i, d: (i, 0)),
                pl.BlockSpec((1, 8, D), lambda i, d: (i * tb // S, 0, 0)),
                pl.BlockSpec((1, D), lambda i, d: (0, 0)),
                pl.BlockSpec((1, D), lambda i, d: (0, 0)),
            ],
            out_specs=pl.BlockSpec((tb, D), lambda i, d: (i, 0)),
            scratch_shapes=[pltpu.VMEM((2, tb // SUBLANES, SUBLANES, D), F32),
                            pltpu.VMEM((2, tb // SUBLANES, SUBLANES, D), F32),
                            pltpu.SemaphoreType.DMA((2, 2))],
        ),
        out_shape=jax.ShapeDtypeStruct((T, D), F32),
        compiler_params=_params("arbitrary"),
        name="moe_combine",
    )(dest, y, gates_tk, h, mod_l, ln_g.reshape(1, D), ln_b.reshape(1, D))


def _moe_layer(h, mod_l, router_w, router_b, w_gate, w_up, w_down, layer, ln_g, ln_b, S):
    T, D = h.shape
    E, R, rb = N_EXPERTS, MOE_SLOT_ROWS, MOE_ROW_BLOCK
    n_slots = (T * TOP_K + E * (R - 1)) // R
    n_rows = n_slots * R
    per_slot = R // rb
    max_blocks = T * TOP_K // rb + E
    u, idx, gates, rank, cnt = _router(h, mod_l, router_w, router_b, S)
    cnt = cnt[:, 0]
    slots_e = (cnt + R - 1) // R
    slot_end = jnp.cumsum(slots_e)
    slot_start = slot_end - slots_e
    row_start = slot_start * R
    dest = rank
    for e in range(E):
        dest = dest + jnp.where(idx == e, row_start[e], 0)
    tok = jnp.tile(jnp.arange(T, dtype=jnp.int32), TOP_K)
    row_tok = jnp.zeros((n_rows,), jnp.int32).at[dest.reshape(-1)].set(tok)
    sid = jnp.arange(n_slots, dtype=jnp.int32)
    n_used = slot_end[-1]
    last_used = jnp.maximum(n_used - 1, 0)
    se = jnp.minimum(jnp.searchsorted(slot_end, jnp.minimum(sid, last_used), side='right'), E - 1).astype(jnp.int32)
    rows_in = jnp.clip(cnt[se] - (sid - slot_start[se]) * R, 0, R)
    rows_in = jnp.where(sid < n_used, rows_in, 0)
    nblk = ((rows_in + rb - 1) // rb).astype(jnp.int32)
    blk = jnp.arange(n_rows // rb, dtype=jnp.int32)
    blk_valid = (blk % per_slot) < nblk[blk // per_slot]
    order = jnp.argsort(jnp.logical_not(blk_valid), stable=True).astype(jnp.int32)
    n_valid = jnp.sum(blk_valid.astype(jnp.int32))
    blk_ids = jnp.where(jnp.arange(max_blocks) < n_valid, order[:max_blocks], n_rows // rb).astype(jnp.int32)
    xs = _dispatch(u, row_tok, blk_ids, n_valid.reshape(1), n_rows)
    y = _moe_ffn(xs, se, nblk, w_gate, w_up, w_down, layer)
    return _combine(y, dest.reshape(-1).astype(jnp.int32), gates.T, h, mod_l, ln_g, ln_b, S)


def kernel(x, c, positions, ada_w, ada_b, ada_table, ln_g, ln_b, mla_w_dq, mla_q_norm, mla_w_uq, mla_w_dkv, mla_kv_norm, mla_w_ukv, mla_w_o, ca_w_qkv, ca_b_qkv, ca_rel_bias, ca_w_o, ffn_w_gate, ffn_w_up, ffn_w_down, moe_router_w, moe_router_b, moe_w_gate, moe_w_up, moe_w_down):
    B, S, D = x.shape
    T = B * S
    mod = _cond(c, ada_w, ada_b, ada_table)
    half = QK_ROPE // 2
    inv_freq = ROPE_THETA ** (-jnp.arange(half, dtype=F32) / half)
    freq_row = jnp.concatenate([inv_freq, inv_freq, jnp.zeros((LANES - QK_ROPE,), F32)]).reshape(1, LANES)
    pos_col = positions.reshape(T, 1).astype(jnp.int32)
    h = x.reshape(T, D)
    for i in range(DEPTH):
        j = i // 2
        mod_l = mod[i]
        if i % 2 == 0:
            q, k, v = _mla_proj(h, mod_l, pos_col, freq_row, mla_w_dq[j], mla_q_norm[j], mla_w_dkv[j],
                                mla_kv_norm[j], mla_w_uq[j], mla_w_ukv[j], S)
            o = _mla_attn(q, k, v, B, S)
            h = _proj_ln(o, mla_w_o, j, h, mod_l, ln_g[i, 0], ln_b[i, 0], S)
            h = _ffn(h, mod_l, ffn_w_gate, ffn_w_up, ffn_w_down, j, ln_g[i, 1], ln_b[i, 1], S)
        else:
            qkv = _qkv(h, mod_l, ca_w_qkv, j, ca_b_qkv[j], S)
            o = _ca_attn(qkv, ca_rel_bias[j], B, S)
            h = _proj_ln(o, ca_w_o, j, h, mod_l, ln_g[i, 0], ln_b[i, 0], S)
            h = _moe_layer(h, mod_l, moe_router_w[j], moe_router_b[j], moe_w_gate, moe_w_up,
                           moe_w_down, j, ln_g[i, 1], ln_b[i, 1], S)
    return h.reshape(B, S, D)
```

```python
import functools

import numpy as np
import jax
import jax.numpy as jnp
from jax import lax
from jax.experimental import pallas as pl
from jax.experimental.pallas import tpu as pltpu

F32 = jnp.float32
BF16 = jnp.bfloat16

CHUNK = 64
MLA_HEADS = 16
QK_NOPE = 128
QK_ROPE = 64
V_HEAD = 128
KV_LORA = 512
ROPE_THETA = 10000.0
CA_HEADS = 16
CA_HEAD_DIM = 128
LEFT_CHUNKS = 8
MAX_REL = 256
N_EXPERTS = 8
TOP_K = 2
N_MOD = 6
LN_EPS = 1e-5
RMS_EPS = 1e-6
DEPTH = 4
DEEPNORM_ALPHA = (2.0 * DEPTH) ** 0.25
LOG2E = 1.4426950408889634

LANES = 128
VMEM_LIMIT = 56 * 1024 * 1024

MLA_HEAD_PAD = 2 * LANES
MOE_SLOT_ROWS = 2560
MOE_ROW_BLOCK = 256
CA_QBLOCK = 256
CA_KBLOCKS = 3
ATTN_HEADS_PER_STEP = 8

_NT = (((1,), (1,)), ((), ()))


def _params(*sem):
    return pltpu.CompilerParams(dimension_semantics=sem, vmem_limit_bytes=VMEM_LIMIT)


def _silu(x):
    return x * jax.nn.sigmoid(x)


def _layer_norm(z, g, b):
    mu = jnp.mean(z, axis=-1, keepdims=True)
    zc = z - mu
    var = jnp.mean(zc * zc, axis=-1, keepdims=True)
    return zc * lax.rsqrt(var + LN_EPS) * g + b


def _modulate(h, mod_ref, k):
    return h * (1.0 + mod_ref[0, k + 1:k + 2, :]) + mod_ref[0, k:k + 1, :]


def _cond_kernel(c_ref, w_ref, b_ref, tab_ref, o_ref):
    s = _silu(c_ref[...]).astype(BF16)
    y = jnp.dot(s, w_ref[...].astype(BF16), preferred_element_type=F32) + b_ref[...]
    o_ref[...] = y[None, :, :] + tab_ref[...][:, None, :]


def _cond(c, ada_w, ada_b, ada_table):
    B, D = c.shape
    N = ada_w.shape[1]
    tn = 1024
    c8 = jnp.zeros((8, D), F32).at[:B].set(c)
    out = pl.pallas_call(
        _cond_kernel,
        grid=(N // tn,),
        in_specs=[
            pl.BlockSpec((8, D), lambda n: (0, 0)),
            pl.BlockSpec((D, tn), lambda n: (0, n)),
            pl.BlockSpec((1, tn), lambda n: (0, n)),
            pl.BlockSpec((DEPTH, tn), lambda n: (0, n)),
        ],
        out_specs=pl.BlockSpec((DEPTH, 8, tn), lambda n: (0, 0, n)),
        out_shape=jax.ShapeDtypeStruct((DEPTH, 8, N), F32),
        compiler_params=_params("arbitrary"),
        name="cond",
    )(c8, ada_w, ada_b.reshape(1, N), ada_table.reshape(DEPTH, N))
    mod = out[:, :B].reshape(DEPTH, B, N_MOD, D)
    return jnp.pad(mod, ((0, 0), (0, 0), (0, 8 - N_MOD), (0, 0)))


def _rope_lanes(x, cos, sin_signed):
    lane = lax.broadcasted_iota(jnp.int32, x.shape, 1)
    partner = jnp.where(lane < QK_ROPE // 2, pltpu.roll(x, LANES - QK_ROPE // 2, 1),
                        pltpu.roll(x, QK_ROPE // 2, 1))
    return x * cos + partner * sin_signed


def _mla_proj_kernel(h_ref, mod_ref, pos_ref, freq_ref, wdq_ref, qn_ref, wdkv_ref, kvn_ref, wkr_ref,
                     wuq_ref, wukv_ref, q_ref, k_ref, v_ref):
    u = _modulate(h_ref[...], mod_ref, 0).astype(BF16)
    ang = pos_ref[...].astype(F32) * freq_ref[...]
    cos = jnp.cos(ang)
    sin = jnp.sin(ang)
    lane = lax.broadcasted_iota(jnp.int32, ang.shape, 1)
    sin_signed = jnp.where(lane < QK_ROPE // 2, -sin, sin)

    def rms(x, g):
        return (x * lax.rsqrt(jnp.mean(x * x, axis=-1, keepdims=True) + RMS_EPS)) * g

    cq = rms(jnp.dot(u, wdq_ref[...], preferred_element_type=F32), qn_ref[...]).astype(BF16)
    ckv = rms(jnp.dot(u, wdkv_ref[...], preferred_element_type=F32), kvn_ref[...]).astype(BF16)
    kr = jnp.dot(u, wkr_ref[...], preferred_element_type=F32)
    kr = _rope_lanes(kr, cos, sin_signed).astype(BF16)
    for hd in range(MLA_HEADS):
        c0 = hd * MLA_HEAD_PAD
        qh = jnp.dot(cq, wuq_ref[:, c0:c0 + MLA_HEAD_PAD], preferred_element_type=F32)
        q_ref[:, c0:c0 + LANES] = qh[:, :LANES].astype(BF16)
        q_ref[:, c0 + LANES:c0 + MLA_HEAD_PAD] = _rope_lanes(qh[:, LANES:], cos, sin_signed).astype(BF16)
        kvh = jnp.dot(ckv, wukv_ref[:, c0:c0 + MLA_HEAD_PAD], preferred_element_type=F32)
        k_ref[:, c0:c0 + LANES] = kvh[:, :LANES].astype(BF16)
        k_ref[:, c0 + LANES:c0 + MLA_HEAD_PAD] = kr
        v_ref[:, hd * V_HEAD:(hd + 1) * V_HEAD] = kvh[:, LANES:].astype(BF16)


def _mla_proj(h, mod_l, pos_col, freq_row, w_dq, q_norm, w_dkv, kv_norm, w_uq, w_ukv, S):
    T, D = h.shape
    H = MLA_HEADS
    tm = 256
    q_lora = w_dq.shape[1]
    wdq = w_dq.astype(BF16)
    wdkv = w_dkv[:, :KV_LORA].astype(BF16)
    wkr = jnp.pad(w_dkv[:, KV_LORA:], ((0, 0), (0, LANES - QK_ROPE))).astype(BF16)
    wuq = jnp.pad(w_uq.reshape(q_lora, H, QK_NOPE + QK_ROPE),
                  ((0, 0), (0, 0), (0, MLA_HEAD_PAD - QK_NOPE - QK_ROPE))).reshape(q_lora, H * MLA_HEAD_PAD)
    wuq = wuq.astype(BF16)
    wukv = w_ukv.astype(BF16)
    const = lambda i: (0, 0)
    row = lambda i: (i, 0)
    return pl.pallas_call(
        _mla_proj_kernel,
        grid=(T // tm,),
        in_specs=[
            pl.BlockSpec((tm, D), row),
            pl.BlockSpec((1, 8, D), lambda i: (i * tm // S, 0, 0)),
            pl.BlockSpec((tm, 1), row),
            pl.BlockSpec((1, LANES), const),
            pl.BlockSpec(wdq.shape, const),
            pl.BlockSpec((1, q_lora), const),
            pl.BlockSpec(wdkv.shape, const),
            pl.BlockSpec((1, KV_LORA), const),
            pl.BlockSpec(wkr.shape, const),
            pl.BlockSpec(wuq.shape, const),
            pl.BlockSpec(wukv.shape, const),
        ],
        out_specs=[
            pl.BlockSpec((tm, H * MLA_HEAD_PAD), row),
            pl.BlockSpec((tm, H * MLA_HEAD_PAD), row),
            pl.BlockSpec((tm, H * V_HEAD), row),
        ],
        out_shape=[
            jax.ShapeDtypeStruct((T, H * MLA_HEAD_PAD), BF16),
            jax.ShapeDtypeStruct((T, H * MLA_HEAD_PAD), BF16),
            jax.ShapeDtypeStruct((T, H * V_HEAD), BF16),
        ],
        compiler_params=_params("arbitrary"),
        name="mla_proj",
    )(h, mod_l, pos_col, freq_row, wdq, q_norm.reshape(1, -1), wdkv, kv_norm.reshape(1, -1), wkr, wuq, wukv)


def _with_ones(v):
    return jnp.concatenate([v, jnp.ones_like(v)], axis=1)


def _mla_attn_kernel(pi_ref, pj_ref, q_ref, k_ref, v_ref, o_ref, m_sc, acc_sc, *, scale):
    p = pl.program_id(2)
    i = pi_ref[p]
    j = pj_ref[p]

    @pl.when(j == 0)
    def _():
        m_sc[...] = jnp.full(m_sc.shape, -jnp.inf, F32)
        acc_sc[...] = jnp.zeros(acc_sc.shape, F32)

    def accumulate(diagonal):
        for hd in range(ATTN_HEADS_PER_STEP):
            qk = slice(hd * MLA_HEAD_PAD, (hd + 1) * MLA_HEAD_PAD)
            s = lax.dot_general(q_ref[:, qk], k_ref[:, qk], _NT, preferred_element_type=F32) * scale
            if diagonal:
                qc = lax.broadcasted_iota(jnp.int32, s.shape, 0) // CHUNK
                kc = lax.broadcasted_iota(jnp.int32, s.shape, 1) // CHUNK
                s = jnp.where(kc <= qc, s, -jnp.inf)
            m_prev = m_sc[hd]
            m_new = jnp.maximum(m_prev, jnp.max(s, axis=-1, keepdims=True))
            a = jnp.exp2(m_prev - m_new)
            e = jnp.exp2(s - m_new).astype(BF16)
            v = _with_ones(v_ref[:, hd * V_HEAD:(hd + 1) * V_HEAD])
            acc_sc[hd] = a * acc_sc[hd] + jnp.dot(e, v, preferred_element_type=F32)
            m_sc[hd] = m_new

    @pl.when(j < i)
    def _():
        accumulate(False)

    @pl.when(j == i)
    def _():
        accumulate(True)
        for hd in range(ATTN_HEADS_PER_STEP):
            acc = acc_sc[hd]
            o_ref[:, hd * V_HEAD:(hd + 1) * V_HEAD] = (acc[:, :V_HEAD] / acc[:, V_HEAD:]).astype(o_ref.dtype)


def _mla_attn(q, k, v, B, S):
    H = MLA_HEADS
    hp = ATTN_HEADS_PER_STEP
    tq = 512
    nq = S // tq
    scale = (QK_NOPE + QK_ROPE) ** -0.5 * LOG2E
    pairs = [(i, j) for i in range(nq) for j in range(i + 1)]
    pi = jnp.asarray([p[0] for p in pairs], jnp.int32)
    pj = jnp.asarray([p[1] for p in pairs], jnp.int32)
    return pl.pallas_call(
        functools.partial(_mla_attn_kernel, scale=scale),
        grid_spec=pltpu.PrefetchScalarGridSpec(
            num_scalar_prefetch=2,
            grid=(B, H // hp, len(pairs)),
            in_specs=[
                pl.BlockSpec((tq, hp * MLA_HEAD_PAD), lambda b, g, p, pi, pj: (b * nq + pi[p], g)),
                pl.BlockSpec((tq, hp * MLA_HEAD_PAD), lambda b, g, p, pi, pj: (b * nq + pj[p], g)),
                pl.BlockSpec((tq, hp * V_HEAD), lambda b, g, p, pi, pj: (b * nq + pj[p], g)),
            ],
            out_specs=pl.BlockSpec((tq, hp * V_HEAD), lambda b, g, p, pi, pj: (b * nq + pi[p], g)),
            scratch_shapes=[pltpu.VMEM((hp, tq, 1), F32), pltpu.VMEM((hp, tq, 2 * V_HEAD), F32)],
        ),
        out_shape=jax.ShapeDtypeStruct((B * S, H * V_HEAD), BF16),
        compiler_params=_params("arbitrary", "arbitrary", "arbitrary"),
        name="mla_attn",
    )(pi, pj, q, k, v)


def _proj_ln_kernel(o_ref, w_ref, h_ref, mod_ref, g_ref, b_ref, out_ref, w_sc, *, gate_row):
    @pl.when(pl.program_id(0) == 0)
    def _():
        w_sc[...] = w_ref[...].astype(BF16)

    half = out_ref.shape[0] // 2
    for part in range(2):
        rows = slice(part * half, (part + 1) * half)
        y = jnp.dot(o_ref[rows, :], w_sc[...], preferred_element_type=F32)
        z = DEEPNORM_ALPHA * h_ref[rows, :] + (1.0 + mod_ref[0, gate_row:gate_row + 1, :]) * y
        out_ref[rows, :] = _layer_norm(z, g_ref[...], b_ref[...])


def _proj_ln(o, w_o, layer, h, mod_l, ln_g, ln_b, S):
    T, D = h.shape
    tm = 512
    K = w_o.shape[1]
    row = lambda i: (i, 0)
    const = lambda i: (0, 0)
    return pl.pallas_call(
        functools.partial(_proj_ln_kernel, gate_row=2),
        grid=(T // tm,),
        in_specs=[
            pl.BlockSpec((tm, K), row),
            pl.BlockSpec((None, K, D), lambda i: (layer, 0, 0), pipeline_mode=pl.Buffered(1)),
            pl.BlockSpec((tm, D), row),
            pl.BlockSpec((1, 8, D), lambda i: (i * tm // S, 0, 0)),
            pl.BlockSpec((1, D), const),
            pl.BlockSpec((1, D), const),
        ],
        out_specs=pl.BlockSpec((tm, D), row),
        out_shape=jax.ShapeDtypeStruct((T, D), F32),
        scratch_shapes=[pltpu.VMEM((K, D), BF16)],
        compiler_params=_params("arbitrary"),
        name="proj_ln",
    )(o, w_o, h, mod_l, ln_g.reshape(1, D), ln_b.reshape(1, D))


def _ffn_kernel(h_ref, mod_ref, wg_ref, wu_ref, wd_ref, g_ref, b_ref, out_ref, u_sc):
    f = pl.program_id(1)

    @pl.when(f == 0)
    def _():
        u_sc[...] = _modulate(h_ref[...], mod_ref, 3).astype(BF16)
        out_ref[...] = jnp.zeros(out_ref.shape, F32)

    u = u_sc[...]
    g = jnp.dot(u, wg_ref[...].astype(BF16), preferred_element_type=F32)
    up = jnp.dot(u, wu_ref[...].astype(BF16), preferred_element_type=F32)
    hm = (_silu(g) * up).astype(BF16)
    out_ref[...] += jnp.dot(hm, wd_ref[...].astype(BF16), preferred_element_type=F32)

    @pl.when(f == pl.num_programs(1) - 1)
    def _():
        z = DEEPNORM_ALPHA * h_ref[...] + (1.0 + mod_ref[0, 5:6, :]) * out_ref[...]
        out_ref[...] = _layer_norm(z, g_ref[...], b_ref[...])


def _ffn(h, mod_l, w_gate, w_up, w_down, layer, ln_g, ln_b, S):
    T, D = h.shape
    F = w_gate.shape[2]
    tm = min(1024, T)
    tf = 256
    row = lambda i, f: (i, 0)
    const = lambda i, f: (0, 0)
    return pl.pallas_call(
        _ffn_kernel,
        grid=(T // tm, F // tf),
        in_specs=[
            pl.BlockSpec((tm, D), row, pipeline_mode=pl.Buffered(1)),
            pl.BlockSpec((1, 8, D), lambda i, f: (i * tm // S, 0, 0)),
            pl.BlockSpec((None, D, tf), lambda i, f: (layer, 0, f)),
            pl.BlockSpec((None, D, tf), lambda i, f: (layer, 0, f)),
            pl.BlockSpec((None, tf, D), lambda i, f: (layer, f, 0)),
            pl.BlockSpec((1, D), const),
            pl.BlockSpec((1, D), const),
        ],
        out_specs=pl.BlockSpec((tm, D), row),
        out_shape=jax.ShapeDtypeStruct((T, D), F32),
        scratch_shapes=[pltpu.VMEM((tm, D), BF16)],
        compiler_params=_params("arbitrary", "arbitrary"),
        name="ffn",
    )(h, mod_l, w_gate, w_up, w_down, ln_g.reshape(1, D), ln_b.reshape(1, D))


def _qkv_kernel(h_ref, mod_ref, w_ref, b_ref, o_ref, u_sc):
    @pl.when(pl.program_id(1) == 0)
    def _():
        u_sc[...] = _modulate(h_ref[...], mod_ref, 0).astype(BF16)

    y = jnp.dot(u_sc[...], w_ref[...], preferred_element_type=F32) + b_ref[...]
    o_ref[...] = y.astype(o_ref.dtype)


def _qkv(h, mod_l, w_qkv, layer, b_qkv, S):
    T, D = h.shape
    N = w_qkv.shape[2]
    tm = min(1024, T)
    tn = 1024
    w = w_qkv[layer].astype(BF16)
    return pl.pallas_call(
        _qkv_kernel,
        grid=(T // tm, N // tn),
        in_specs=[
            pl.BlockSpec((tm, D), lambda i, n: (i, 0), pipeline_mode=pl.Buffered(1)),
            pl.BlockSpec((1, 8, D), lambda i, n: (i * tm // S, 0, 0)),
            pl.BlockSpec((D, tn), lambda i, n: (0, n)),
            pl.BlockSpec((1, tn), lambda i, n: (0, n)),
        ],
        out_specs=pl.BlockSpec((tm, tn), lambda i, n: (i, n)),
        out_shape=jax.ShapeDtypeStruct((T, N), BF16),
        scratch_shapes=[pltpu.VMEM((tm, D), BF16)],
        compiler_params=_params("arbitrary", "arbitrary"),
        name="qkv",
    )(h, mod_l, w, b_qkv.reshape(1, N))


def _ca_attn_kernel(q_ref, k0_ref, k1_ref, k2_ref, v0_ref, v1_ref, v2_ref, bias_ref, o_ref, *, scale):
    i = pl.program_id(2)
    tq = q_ref.shape[0]
    k_refs = (k0_ref, k1_ref, k2_ref)
    v_refs = (v0_ref, v1_ref, v2_ref)

    def attend(stream_start):
        for hd in range(ATTN_HEADS_PER_STEP):
            cols = slice(hd * CA_HEAD_DIM, (hd + 1) * CA_HEAD_DIM)
            q = q_ref[:, cols]
            s = []
            for p in range(CA_KBLOCKS):
                sp = lax.dot_general(q, k_refs[p][:, cols], _NT, preferred_element_type=F32) * scale
                sp = sp + bias_ref[hd, :, p * tq:(p + 1) * tq]
                first = CA_KBLOCKS - 1 - p
                if stream_start and first > 0:
                    sp = jnp.where(i >= first, sp, -jnp.inf)
                s.append(sp)
            m = functools.reduce(jnp.maximum, [jnp.max(sp, axis=-1, keepdims=True) for sp in s])
            acc = functools.reduce(jnp.add, [
                jnp.dot(jnp.exp2(sp - m).astype(BF16), _with_ones(v_refs[p][:, cols]), preferred_element_type=F32)
                for p, sp in enumerate(s)])
            o_ref[:, cols] = (acc[:, :CA_HEAD_DIM] / acc[:, CA_HEAD_DIM:]).astype(o_ref.dtype)

    @pl.when(i < CA_KBLOCKS - 1)
    def _():
        attend(True)

    @pl.when(i >= CA_KBLOCKS - 1)
    def _():
        attend(False)


def _ca_bias_kernel(w_ref, o_ref):
    tq, nk = o_ref.shape[1], o_ref.shape[2]
    W = w_ref.shape[2]
    x = jnp.broadcast_to(w_ref[0], (tq, W))
    t = pltpu.roll(x, W - (tq - 1), 1, stride=1, stride_axis=0)[:, :nk]
    qc = lax.broadcasted_iota(jnp.int32, (tq, nk), 0) // CHUNK + (nk - tq) // CHUNK
    kc = lax.broadcasted_iota(jnp.int32, (tq, nk), 1) // CHUNK
    o_ref[0] = jnp.where((kc <= qc) & (kc >= qc - LEFT_CHUNKS), t * LOG2E, -jnp.inf)


def _ca_bias_table(rel_bias):
    H = rel_bias.shape[0]
    tq = CA_QBLOCK
    nk = CA_KBLOCKS * tq
    W = tq + nk
    d = np.arange(W) - (tq - 1)
    rel_idx = np.clip((nk - tq) - d, -MAX_REL, MAX_REL) + MAX_REL
    w = rel_bias[:, rel_idx].astype(F32).reshape(H, 1, W)
    return pl.pallas_call(
        _ca_bias_kernel,
        grid=(H,),
        in_specs=[pl.BlockSpec((1, 1, W), lambda h: (h, 0, 0))],
        out_specs=pl.BlockSpec((1, tq, nk), lambda h: (h, 0, 0)),
        out_shape=jax.ShapeDtypeStruct((H, tq, nk), F32),
        compiler_params=_params("arbitrary"),
        name="ca_bias",
    )(w)


def _ca_attn(qkv, rel_bias, B, S):
    H, Dh = CA_HEADS, CA_HEAD_DIM
    hp = ATTN_HEADS_PER_STEP
    ng = H // hp
    tq = CA_QBLOCK
    nq = S // tq
    bias = _ca_bias_table(rel_bias)

    def kv_spec(which, p):
        back = CA_KBLOCKS - 1 - p
        return pl.BlockSpec((tq, hp * Dh), lambda b, g, i: (b * nq + jnp.maximum(i - back, 0), which * ng + g))

    return pl.pallas_call(
        functools.partial(_ca_attn_kernel, scale=Dh ** -0.5 * LOG2E),
        grid=(B, ng, nq),
        in_specs=[pl.BlockSpec((tq, hp * Dh), lambda b, g, i: (b * nq + i, g))]
        + [kv_spec(1, p) for p in range(CA_KBLOCKS)]
        + [kv_spec(2, p) for p in range(CA_KBLOCKS)]
        + [pl.BlockSpec((hp, tq, CA_KBLOCKS * tq), lambda b, g, i: (g, 0, 0))],
        out_specs=pl.BlockSpec((tq, hp * Dh), lambda b, g, i: (b * nq + i, g)),
        out_shape=jax.ShapeDtypeStruct((B * S, H * Dh), BF16),
        compiler_params=_params("arbitrary", "arbitrary", "arbitrary"),
        name="ca_attn",
    )(qkv, qkv, qkv, qkv, qkv, qkv, qkv, bias)


def _pack_bf16_halves(x):
    half = x.shape[1] // 2
    lo = pltpu.bitcast(x[:, :half].astype(BF16).astype(F32), jnp.uint32)
    hi = pltpu.bitcast(x[:, half:].astype(BF16).astype(F32), jnp.uint32)
    return lax.shift_right_logical(lo, jnp.uint32(16)) | (hi & jnp.uint32(0xFFFF0000))


def _unpack_bf16_halves(p):
    lo = pltpu.bitcast(lax.shift_left(p, jnp.uint32(16)), F32).astype(BF16)
    hi = pltpu.bitcast(p & jnp.uint32(0xFFFF0000), F32).astype(BF16)
    return jnp.concatenate([lo, hi], axis=1)


def _router_kernel(h_ref, mod_ref, rw_ref, rb_ref, u_ref, idx_ref, gate_ref, rank_ref, cnt_ref, carry_sc):
    t = pl.program_id(0)

    @pl.when(t == 0)
    def _():
        carry_sc[...] = jnp.zeros(carry_sc.shape, F32)

    u = _modulate(h_ref[...], mod_ref, 3)
    u_ref[...] = _pack_bf16_halves(u)
    tb = u.shape[0]
    logits = lax.dot_general(rw_ref[...], u, _NT, precision=lax.Precision.HIGHEST,
                             preferred_element_type=F32) + rb_ref[...]
    eid = lax.broadcasted_iota(jnp.int32, logits.shape, 0)
    m1 = jnp.max(logits, axis=0, keepdims=True)
    i1 = jnp.min(jnp.where(logits == m1, eid, N_EXPERTS), axis=0, keepdims=True)
    rest = jnp.where(eid == i1, -jnp.inf, logits)
    m2 = jnp.max(rest, axis=0, keepdims=True)
    i2 = jnp.min(jnp.where(rest == m2, eid, N_EXPERTS), axis=0, keepdims=True)
    e2 = jnp.exp(m2 - m1)
    den = 1.0 + e2
    idx_ref[...] = jnp.concatenate([i1, i2], axis=0)
    gate_ref[...] = jnp.concatenate([1.0 / den, e2 / den], axis=0)
    onehot = jnp.where((eid == i1) | (eid == i2), 1.0, 0.0)
    tri = jnp.where(lax.broadcasted_iota(jnp.int32, (tb, tb), 0) <= lax.broadcasted_iota(jnp.int32, (tb, tb), 1),
                    1.0, 0.0).astype(BF16)
    incl = jnp.dot(onehot.astype(BF16), tri, preferred_element_type=F32)
    excl = incl - onehot + carry_sc[:, 0:1]
    r1 = jnp.sum(jnp.where(eid == i1, excl, 0.0), axis=0, keepdims=True)
    r2 = jnp.sum(jnp.where(eid == i2, excl, 0.0), axis=0, keepdims=True)
    rank_ref[...] = jnp.concatenate([r1, r2], axis=0).astype(jnp.int32)
    carry_sc[...] = carry_sc[...] + jnp.sum(onehot, axis=1, keepdims=True)
    cnt_ref[...] = carry_sc[...].astype(jnp.int32)


def _router(h, mod_l, router_w, router_b, S):
    T, D = h.shape
    E = N_EXPERTS
    tb = 512
    return pl.pallas_call(
        _router_kernel,
        grid=(T // tb,),
        in_specs=[
            pl.BlockSpec((tb, D), lambda t: (t, 0)),
            pl.BlockSpec((1, 8, D), lambda t: (t * tb // S, 0, 0)),
            pl.BlockSpec((E, D), lambda t: (0, 0)),
            pl.BlockSpec((E, 1), lambda t: (0, 0)),
        ],
        out_specs=[
            pl.BlockSpec((tb, D // 2), lambda t: (t, 0)),
            pl.BlockSpec((TOP_K, tb), lambda t: (0, t)),
            pl.BlockSpec((TOP_K, tb), lambda t: (0, t)),
            pl.BlockSpec((TOP_K, tb), lambda t: (0, t)),
            pl.BlockSpec((E, LANES), lambda t: (0, 0)),
        ],
        out_shape=[
            jax.ShapeDtypeStruct((T, D // 2), jnp.uint32),
            jax.ShapeDtypeStruct((TOP_K, T), jnp.int32),
            jax.ShapeDtypeStruct((TOP_K, T), F32),
            jax.ShapeDtypeStruct((TOP_K, T), jnp.int32),
            jax.ShapeDtypeStruct((E, LANES), jnp.int32),
        ],
        scratch_shapes=[pltpu.VMEM((E, LANES), F32)],
        compiler_params=_params("arbitrary"),
        name="router",
    )(h, mod_l, router_w.T, router_b.reshape(E, 1))


SUBLANES = 8


def _row_gather_group(src_hbm, dst, sem, rows_ref, base, g):
    for k in range(SUBLANES):
        r = rows_ref[base + g * SUBLANES + k]
        pltpu.make_async_copy(src_hbm.at[pl.ds(r, 1), :], dst.at[g, pl.ds(k, 1), :], sem).start(priority=k % 2)


def _row_gather_start(src_hbm, dst, sem, rows_ref, base):
    def issue(g, carry):
        _row_gather_group(src_hbm, dst, sem, rows_ref, base, g)
        return carry
    lax.fori_loop(0, dst.shape[0], issue, 0)


def _row_gather_wait(dst, sem):
    pltpu.make_async_copy(dst, dst, sem).wait()


def _gathered_rows(buf):
    return buf.reshape(buf.shape[0] * SUBLANES, buf.shape[2])


def _dispatch_kernel(row_tok_ref, blk_ref, nv_ref, u_hbm, o_ref, buf, sem):
    c = pl.program_id(0)
    nv = nv_ref[0]
    rb = MOE_ROW_BLOCK
    slot = c % 2

    def start(step, sl):
        _row_gather_start(u_hbm, buf.at[sl], sem.at[sl], row_tok_ref, blk_ref[step] * rb)

    @pl.when((c == 0) & (nv > 0))
    def _():
        start(0, 0)

    @pl.when(c + 1 < nv)
    def _():
        start(c + 1, 1 - slot)

    @pl.when(c < nv)
    def _():
        _row_gather_wait(buf.at[slot], sem.at[slot])
        o_ref[...] = _unpack_bf16_halves(_gathered_rows(buf[slot]))

    @pl.when(c >= nv)
    def _():
        o_ref[...] = jnp.zeros(o_ref.shape, BF16)


def _dispatch(u, row_tok, blk_ids, n_valid, n_rows):
    T, half = u.shape
    D = 2 * half
    rb = MOE_ROW_BLOCK
    return pl.pallas_call(
        _dispatch_kernel,
        grid_spec=pltpu.PrefetchScalarGridSpec(
            num_scalar_prefetch=3,
            grid=(blk_ids.shape[0],),
            in_specs=[pl.BlockSpec(memory_space=pl.ANY)],
            out_specs=pl.BlockSpec((rb, D), lambda c, rt, blk, nv: (blk[c], 0)),
            scratch_shapes=[pltpu.VMEM((2, rb // SUBLANES, SUBLANES, half), jnp.uint32),
                            pltpu.SemaphoreType.DMA((2,))],
        ),
        out_shape=jax.ShapeDtypeStruct((n_rows + rb, D), BF16),
        compiler_params=_params("arbitrary"),
        name="moe_dispatch",
    )(row_tok, blk_ids, n_valid, u)


def _moe_ffn_kernel(se_ref, nblk_ref, x_ref, wg_ref, wu_ref, wd_ref, y_hbm, acc, wg_sc, wu_sc, wd_sc, sem):
    s = pl.program_id(0)
    f = pl.program_id(1)
    nb = nblk_ref[s]
    rb = MOE_ROW_BLOCK

    @pl.when(nb > 0)
    def _():
        @pl.when(f == 0)
        def _():
            acc[...] = jnp.zeros(acc.shape, F32)

        wg_sc[...] = wg_ref[...].astype(BF16)
        wu_sc[...] = wu_ref[...].astype(BF16)
        wd_sc[...] = wd_ref[...].astype(BF16)

        last_f = f == pl.num_programs(1) - 1

        def out_copy(start, size):
            return pltpu.make_async_copy(
                acc.at[pl.ds(pl.multiple_of(start, rb), size), :],
                y_hbm.at[pl.ds(pl.multiple_of(s * MOE_SLOT_ROWS + start, rb), size), :], sem.at[0])

        def rows_at(start, size):
            rows = pl.ds(pl.multiple_of(start, rb), size)
            x = x_ref[rows, :]
            g = jnp.dot(x, wg_sc[...], preferred_element_type=F32)
            up = jnp.dot(x, wu_sc[...], preferred_element_type=F32)
            hm = (_silu(g) * up).astype(BF16)
            acc[rows, :] += jnp.dot(hm, wd_sc[...], preferred_element_type=F32)

            @pl.when(last_f)
            def _():
                out_copy(start, size).start()

        def quad(r, carry):
            rows_at(r * (4 * rb), 4 * rb)
            return carry

        lax.fori_loop(0, nb // 4, quad, 0)

        @pl.when(nb % 4 >= 2)
        def _():
            rows_at((nb // 4) * (4 * rb), 2 * rb)

        @pl.when(nb % 2 == 1)
        def _():
            rows_at((nb - 1) * rb, rb)

        @pl.when(last_f)
        def _():
            def wait(r, carry):
                out_copy(r * rb, rb).wait()
                return carry

            lax.fori_loop(0, nb, wait, 0)


def _moe_ffn(xs, slot_e, nblk, n_used, w_gate, w_up, w_down, layer):
    D = xs.shape[1]
    F = w_gate.shape[3]
    R = MOE_SLOT_ROWS
    n_slots = slot_e.shape[0]
    tf = 256
    nf = F // tf

    def f_eff(s, f, nb):
        return jnp.where(nb[s] > 0, f, nf - 1)

    return pl.pallas_call(
        _moe_ffn_kernel,
        grid_spec=pltpu.PrefetchScalarGridSpec(
            num_scalar_prefetch=2,
            grid=(n_used, nf),
            in_specs=[
                pl.BlockSpec((R, D), lambda s, f, se, nb: (s, 0), pipeline_mode=pl.Buffered(1)),
                pl.BlockSpec((None, None, D, tf), lambda s, f, se, nb: (layer, se[s], 0, f_eff(s, f, nb))),
                pl.BlockSpec((None, None, D, tf), lambda s, f, se, nb: (layer, se[s], 0, f_eff(s, f, nb))),
                pl.BlockSpec((None, None, tf, D), lambda s, f, se, nb: (layer, se[s], f_eff(s, f, nb), 0)),
            ],
            out_specs=pl.BlockSpec(memory_space=pl.ANY),
            scratch_shapes=[pltpu.VMEM((R, D), F32), pltpu.VMEM((D, tf), BF16), pltpu.VMEM((D, tf), BF16),
                            pltpu.VMEM((tf, D), BF16), pltpu.SemaphoreType.DMA((1,))],
        ),
        out_shape=jax.ShapeDtypeStruct((n_slots * R, D), F32),
        compiler_params=_params("arbitrary", "arbitrary"),
        name="moe_ffn",
    )(slot_e, nblk, xs, w_gate, w_up, w_down)


def _combine_kernel(dest_ref, y_hbm, gate_ref, h_ref, mod_ref, g_ref, b_ref, out_ref, buf0, buf1, sem, *, n_tok):
    i = pl.program_id(0)
    tb = out_ref.shape[0]
    slot = i % 2

    def start(step, sl):
        _row_gather_start(y_hbm, buf0.at[sl], sem.at[0, sl], dest_ref, step * tb)
        _row_gather_start(y_hbm, buf1.at[sl], sem.at[1, sl], dest_ref, n_tok + step * tb)

    @pl.when(i == 0)
    def _():
        start(0, 0)

    @pl.when(i + 1 < pl.num_programs(0))
    def _():
        start(i + 1, 1 - slot)

    _row_gather_wait(buf0.at[slot], sem.at[0, slot])
    _row_gather_wait(buf1.at[slot], sem.at[1, slot])
    y = gate_ref[:, 0:1] * _gathered_rows(buf0[slot]) + gate_ref[:, 1:2] * _gathered_rows(buf1[slot])
    z = DEEPNORM_ALPHA * h_ref[...] + (1.0 + mod_ref[0, 5:6, :]) * y
    out_ref[...] = _layer_norm(z, g_ref[...], b_ref[...])


def _combine(y, dest, gates_tk, h, mod_l, ln_g, ln_b, S):
    T, D = h.shape
    tb = 256
    return pl.pallas_call(
        functools.partial(_combine_kernel, n_tok=T),
        grid_spec=pltpu.PrefetchScalarGridSpec(
            num_scalar_prefetch=1,
            grid=(T // tb,),
            in_specs=[
                pl.BlockSpec(memory_space=pl.ANY),
                pl.BlockSpec((tb, TOP_K), lambda i, d: (i, 0)),
                pl.BlockSpec((tb, D), lambda i, d: (i, 0)),
                pl.BlockSpec((1, 8, D), lambda i, d: (i * tb // S, 0, 0)),
                pl.BlockSpec((1, D), lambda i, d: (0, 0)),
                pl.BlockSpec((1, D), lambda i, d: (0, 0)),
            ],
            out_specs=pl.BlockSpec((tb, D), lambda i, d: (i, 0)),
            scratch_shapes=[pltpu.VMEM((2, tb // SUBLANES, SUBLANES, D), F32),
                            pltpu.VMEM((2, tb // SUBLANES, SUBLANES, D), F32),
                            pltpu.SemaphoreType.DMA((2, 2))],
        ),
        out_shape=jax.ShapeDtypeStruct((T, D), F32),
        compiler_params=_params("arbitrary"),
        name="moe_combine",
    )(dest, y, gates_tk, h, mod_l, ln_g.reshape(1, D), ln_b.reshape(1, D))


def _moe_layer(h, mod_l, router_w, router_b, w_gate, w_up, w_down, layer, ln_g, ln_b, S):
    T, D = h.shape
    E, R, rb = N_EXPERTS, MOE_SLOT_ROWS, MOE_ROW_BLOCK
    n_slots = (T * TOP_K + E * (R - 1)) // R
    n_rows = n_slots * R
    per_slot = R // rb
    max_blocks = T * TOP_K // rb + E
    u, idx, gates, rank, cnt = _router(h, mod_l, router_w, router_b, S)
    cnt = cnt[:, 0]
    slots_e = (cnt + R - 1) // R
    slot_end = jnp.cumsum(slots_e)
    slot_start = slot_end - slots_e
    row_start = slot_start * R
    dest = rank
    for e in range(E):
        dest = dest + jnp.where(idx == e, row_start[e], 0)
    tok = jnp.tile(jnp.arange(T, dtype=jnp.int32), TOP_K)
    row_tok = jnp.zeros((n_rows,), jnp.int32).at[dest.reshape(-1)].set(tok)
    sid = jnp.arange(n_slots, dtype=jnp.int32)
    n_used = slot_end[-1]
    last_used = jnp.maximum(n_used - 1, 0)
    se = jnp.minimum(jnp.searchsorted(slot_end, jnp.minimum(sid, last_used), side='right'), E - 1).astype(jnp.int32)
    rows_in = jnp.clip(cnt[se] - (sid - slot_start[se]) * R, 0, R)
    rows_in = jnp.where(sid < n_used, rows_in, 0)
    nblk = ((rows_in + rb - 1) // rb).astype(jnp.int32)
    blk = jnp.arange(n_rows // rb, dtype=jnp.int32)
    blk_valid = (blk % per_slot) < nblk[blk // per_slot]
    order = jnp.argsort(jnp.logical_not(blk_valid), stable=True).astype(jnp.int32)
    n_valid = jnp.sum(blk_valid.astype(jnp.int32))
    blk_ids = jnp.where(jnp.arange(max_blocks) < n_valid, order[:max_blocks], n_rows // rb).astype(jnp.int32)
    xs = _dispatch(u, row_tok, blk_ids, n_valid.reshape(1), n_rows)
    y = _moe_ffn(xs, se, nblk, n_used.astype(jnp.int32), w_gate, w_up, w_down, layer)
    return _combine(y, dest.reshape(-1).astype(jnp.int32), gates.T, h, mod_l, ln_g, ln_b, S)


def kernel(x, c, positions, ada_w, ada_b, ada_table, ln_g, ln_b, mla_w_dq, mla_q_norm, mla_w_uq, mla_w_dkv, mla_kv_norm, mla_w_ukv, mla_w_o, ca_w_qkv, ca_b_qkv, ca_rel_bias, ca_w_o, ffn_w_gate, ffn_w_up, ffn_w_down, moe_router_w, moe_router_b, moe_w_gate, moe_w_up, moe_w_down):
    B, S, D = x.shape
    T = B * S
    mod = _cond(c, ada_w, ada_b, ada_table)
    half = QK_ROPE // 2
    inv_freq = ROPE_THETA ** (-jnp.arange(half, dtype=F32) / half)
    freq_row = jnp.concatenate([inv_freq, inv_freq, jnp.zeros((LANES - QK_ROPE,), F32)]).reshape(1, LANES)
    pos_col = positions.reshape(T, 1).astype(jnp.int32)
    h = x.reshape(T, D)
    for i in range(DEPTH):
        j = i // 2
        mod_l = mod[i]
        if i % 2 == 0:
            q, k, v = _mla_proj(h, mod_l, pos_col, freq_row, mla_w_dq[j], mla_q_norm[j], mla_w_dkv[j],
                                mla_kv_norm[j], mla_w_uq[j], mla_w_ukv[j], S)
            o = _mla_attn(q, k, v, B, S)
            h = _proj_ln(o, mla_w_o, j, h, mod_l, ln_g[i, 0], ln_b[i, 0], S)
            h = _ffn(h, mod_l, ffn_w_gate, ffn_w_up, ffn_w_down, j, ln_g[i, 1], ln_b[i, 1], S)
        else:
            qkv = _qkv(h, mod_l, ca_w_qkv, j, ca_b_qkv[j], S)
            o = _ca_attn(qkv, ca_rel_bias[j], B, S)
            h = _proj_ln(o, ca_w_o, j, h, mod_l, ln_g[i, 0], ln_b[i, 0], S)
            h = _moe_layer(h, mod_l, moe_router_w[j], moe_router_b[j], moe_w_gate, moe_w_up,
                           moe_w_down, j, ln_g[i, 1], ln_b[i, 1], S)
    return h.reshape(B, S, D)
```

```python
import functools

import numpy as np
import jax
import jax.numpy as jnp
from jax import lax
from jax.experimental import pallas as pl
from jax.experimental.pallas import tpu as pltpu

F32 = jnp.float32
BF16 = jnp.bfloat16

CHUNK = 64
MLA_HEADS = 16
QK_NOPE = 128
QK_ROPE = 64
V_HEAD = 128
KV_LORA = 512
ROPE_THETA = 10000.0
CA_HEADS = 16
CA_HEAD_DIM = 128
LEFT_CHUNKS = 8
MAX_REL = 256
N_EXPERTS = 8
TOP_K = 2
N_MOD = 6
LN_EPS = 1e-5
RMS_EPS = 1e-6
DEPTH = 4
DEEPNORM_ALPHA = (2.0 * DEPTH) ** 0.25
LOG2E = 1.4426950408889634

LANES = 128
VMEM_LIMIT = 56 * 1024 * 1024

MLA_HEAD_PAD = 2 * LANES
MOE_SLOT_ROWS = 2560
MOE_ROW_BLOCK = 256
CA_QBLOCK = 256
CA_KBLOCKS = 3
ATTN_HEADS_PER_STEP = 16

_NT = (((1,), (1,)), ((), ()))


def _params(*sem):
    return pltpu.CompilerParams(dimension_semantics=sem, vmem_limit_bytes=VMEM_LIMIT)


def _silu(x):
    return x * jax.nn.sigmoid(x)


def _layer_norm(z, g, b):
    mu = jnp.mean(z, axis=-1, keepdims=True)
    zc = z - mu
    var = jnp.mean(zc * zc, axis=-1, keepdims=True)
    return zc * lax.rsqrt(var + LN_EPS) * g + b


def _modulate(h, mod_ref, k):
    return h * (1.0 + mod_ref[0, k + 1:k + 2, :]) + mod_ref[0, k:k + 1, :]


def _cond_kernel(c_ref, w_ref, b_ref, tab_ref, o_ref):
    s = _silu(c_ref[...]).astype(BF16)
    y = jnp.dot(s, w_ref[...].astype(BF16), preferred_element_type=F32) + b_ref[...]
    o_ref[...] = y[None, :, :] + tab_ref[...][:, None, :]


def _cond(c, ada_w, ada_b, ada_table):
    B, D = c.shape
    N = ada_w.shape[1]
    tn = 1024
    c8 = jnp.zeros((8, D), F32).at[:B].set(c)
    out = pl.pallas_call(
        _cond_kernel,
        grid=(N // tn,),
        in_specs=[
            pl.BlockSpec((8, D), lambda n: (0, 0)),
            pl.BlockSpec((D, tn), lambda n: (0, n)),
            pl.BlockSpec((1, tn), lambda n: (0, n)),
            pl.BlockSpec((DEPTH, tn), lambda n: (0, n)),
        ],
        out_specs=pl.BlockSpec((DEPTH, 8, tn), lambda n: (0, 0, n)),
        out_shape=jax.ShapeDtypeStruct((DEPTH, 8, N), F32),
        compiler_params=_params("arbitrary"),
        name="cond",
    )(c8, ada_w, ada_b.reshape(1, N), ada_table.reshape(DEPTH, N))
    mod = out[:, :B].reshape(DEPTH, B, N_MOD, D)
    return jnp.pad(mod, ((0, 0), (0, 0), (0, 8 - N_MOD), (0, 0)))


def _rope_lanes(x, cos, sin_signed):
    lane = lax.broadcasted_iota(jnp.int32, x.shape, 1)
    partner = jnp.where(lane < QK_ROPE // 2, pltpu.roll(x, LANES - QK_ROPE // 2, 1),
                        pltpu.roll(x, QK_ROPE // 2, 1))
    return x * cos + partner * sin_signed


def _mla_proj_kernel(h_ref, mod_ref, pos_ref, freq_ref, wdq_ref, qn_ref, wdkv_ref, kvn_ref, wkr_ref,
                     wuq_ref, wukv_ref, q_ref, k_ref, v_ref):
    u = _modulate(h_ref[...], mod_ref, 0).astype(BF16)
    ang = pos_ref[...].astype(F32) * freq_ref[...]
    cos = jnp.cos(ang)
    sin = jnp.sin(ang)
    lane = lax.broadcasted_iota(jnp.int32, ang.shape, 1)
    sin_signed = jnp.where(lane < QK_ROPE // 2, -sin, sin)

    def rms(x, g):
        return (x * lax.rsqrt(jnp.mean(x * x, axis=-1, keepdims=True) + RMS_EPS)) * g

    cq = rms(jnp.dot(u, wdq_ref[...], preferred_element_type=F32), qn_ref[...]).astype(BF16)
    ckv = rms(jnp.dot(u, wdkv_ref[...], preferred_element_type=F32), kvn_ref[...]).astype(BF16)
    kr = jnp.dot(u, wkr_ref[...], preferred_element_type=F32)
    kr = _rope_lanes(kr, cos, sin_signed).astype(BF16)
    for hd in range(MLA_HEADS):
        c0 = hd * MLA_HEAD_PAD
        qh = jnp.dot(cq, wuq_ref[:, c0:c0 + MLA_HEAD_PAD], preferred_element_type=F32)
        q_ref[:, c0:c0 + LANES] = qh[:, :LANES].astype(BF16)
        q_ref[:, c0 + LANES:c0 + MLA_HEAD_PAD] = _rope_lanes(qh[:, LANES:], cos, sin_signed).astype(BF16)
        kvh = jnp.dot(ckv, wukv_ref[:, c0:c0 + MLA_HEAD_PAD], preferred_element_type=F32)
        k_ref[:, c0:c0 + LANES] = kvh[:, :LANES].astype(BF16)
        k_ref[:, c0 + LANES:c0 + MLA_HEAD_PAD] = kr
        v_ref[:, hd * V_HEAD:(hd + 1) * V_HEAD] = kvh[:, LANES:].astype(BF16)


def _mla_proj(h, mod_l, pos_col, freq_row, w_dq, q_norm, w_dkv, kv_norm, w_uq, w_ukv, S):
    T, D = h.shape
    H = MLA_HEADS
    tm = 256
    q_lora = w_dq.shape[1]
    wdq = w_dq.astype(BF16)
    wdkv = w_dkv[:, :KV_LORA].astype(BF16)
    wkr = jnp.pad(w_dkv[:, KV_LORA:], ((0, 0), (0, LANES - QK_ROPE))).astype(BF16)
    wuq = jnp.pad(w_uq.reshape(q_lora, H, QK_NOPE + QK_ROPE),
                  ((0, 0), (0, 0), (0, MLA_HEAD_PAD - QK_NOPE - QK_ROPE))).reshape(q_lora, H * MLA_HEAD_PAD)
    wuq = wuq.astype(BF16)
    wukv = w_ukv.astype(BF16)
    const = lambda i: (0, 0)
    row = lambda i: (i, 0)
    return pl.pallas_call(
        _mla_proj_kernel,
        grid=(T // tm,),
        in_specs=[
            pl.BlockSpec((tm, D), row),
            pl.BlockSpec((1, 8, D), lambda i: (i * tm // S, 0, 0)),
            pl.BlockSpec((tm, 1), row),
            pl.BlockSpec((1, LANES), const),
            pl.BlockSpec(wdq.shape, const),
            pl.BlockSpec((1, q_lora), const),
            pl.BlockSpec(wdkv.shape, const),
            pl.BlockSpec((1, KV_LORA), const),
            pl.BlockSpec(wkr.shape, const),
            pl.BlockSpec(wuq.shape, const),
            pl.BlockSpec(wukv.shape, const),
        ],
        out_specs=[
            pl.BlockSpec((tm, H * MLA_HEAD_PAD), row),
            pl.BlockSpec((tm, H * MLA_HEAD_PAD), row),
            pl.BlockSpec((tm, H * V_HEAD), row),
        ],
        out_shape=[
            jax.ShapeDtypeStruct((T, H * MLA_HEAD_PAD), BF16),
            jax.ShapeDtypeStruct((T, H * MLA_HEAD_PAD), BF16),
            jax.ShapeDtypeStruct((T, H * V_HEAD), BF16),
        ],
        compiler_params=_params("arbitrary"),
        name="mla_proj",
    )(h, mod_l, pos_col, freq_row, wdq, q_norm.reshape(1, -1), wdkv, kv_norm.reshape(1, -1), wkr, wuq, wukv)


def _with_ones(v):
    return jnp.concatenate([v, jnp.ones_like(v)], axis=1)


def _mla_attn_kernel(pi_ref, pj_ref, q_ref, k_ref, v_ref, o_ref, m_sc, acc_sc, *, scale):
    p = pl.program_id(2)
    i = pi_ref[p]
    j = pj_ref[p]

    @pl.when(j == 0)
    def _():
        m_sc[...] = jnp.full(m_sc.shape, -jnp.inf, F32)
        acc_sc[...] = jnp.zeros(acc_sc.shape, F32)

    def accumulate(diagonal):
        for hd in range(ATTN_HEADS_PER_STEP):
            qk = slice(hd * MLA_HEAD_PAD, (hd + 1) * MLA_HEAD_PAD)
            s = lax.dot_general(q_ref[:, qk], k_ref[:, qk], _NT, preferred_element_type=F32) * scale
            if diagonal:
                qc = lax.broadcasted_iota(jnp.int32, s.shape, 0) // CHUNK
                kc = lax.broadcasted_iota(jnp.int32, s.shape, 1) // CHUNK
                s = jnp.where(kc <= qc, s, -jnp.inf)
            m_prev = m_sc[hd]
            m_new = jnp.maximum(m_prev, jnp.max(s, axis=-1, keepdims=True))
            a = jnp.exp2(m_prev - m_new)
            e = jnp.exp2(s - m_new).astype(BF16)
            v = _with_ones(v_ref[:, hd * V_HEAD:(hd + 1) * V_HEAD])
            acc_sc[hd] = a * acc_sc[hd] + jnp.dot(e, v, preferred_element_type=F32)
            m_sc[hd] = m_new

    @pl.when(j < i)
    def _():
        accumulate(False)

    @pl.when(j == i)
    def _():
        accumulate(True)
        for hd in range(ATTN_HEADS_PER_STEP):
            acc = acc_sc[hd]
            o_ref[:, hd * V_HEAD:(hd + 1) * V_HEAD] = (acc[:, :V_HEAD] / acc[:, V_HEAD:]).astype(o_ref.dtype)


def _mla_attn(q, k, v, B, S):
    H = MLA_HEADS
    hp = ATTN_HEADS_PER_STEP
    tq = 512
    nq = S // tq
    scale = (QK_NOPE + QK_ROPE) ** -0.5 * LOG2E
    pairs = [(i, j) for i in range(nq) for j in range(i + 1)]
    pi = jnp.asarray([p[0] for p in pairs], jnp.int32)
    pj = jnp.asarray([p[1] for p in pairs], jnp.int32)
    return pl.pallas_call(
        functools.partial(_mla_attn_kernel, scale=scale),
        grid_spec=pltpu.PrefetchScalarGridSpec(
            num_scalar_prefetch=2,
            grid=(B, H // hp, len(pairs)),
            in_specs=[
                pl.BlockSpec((tq, hp * MLA_HEAD_PAD), lambda b, g, p, pi, pj: (b * nq + pi[p], g)),
                pl.BlockSpec((tq, hp * MLA_HEAD_PAD), lambda b, g, p, pi, pj: (b * nq + pj[p], g)),
                pl.BlockSpec((tq, hp * V_HEAD), lambda b, g, p, pi, pj: (b * nq + pj[p], g)),
            ],
            out_specs=pl.BlockSpec((tq, hp * V_HEAD), lambda b, g, p, pi, pj: (b * nq + pi[p], g)),
            scratch_shapes=[pltpu.VMEM((hp, tq, 1), F32), pltpu.VMEM((hp, tq, 2 * V_HEAD), F32)],
        ),
        out_shape=jax.ShapeDtypeStruct((B * S, H * V_HEAD), BF16),
        compiler_params=_params("arbitrary", "arbitrary", "arbitrary"),
        name="mla_attn",
    )(pi, pj, q, k, v)


def _proj_ln_kernel(o_ref, w_ref, h_ref, mod_ref, g_ref, b_ref, out_ref, w_sc, *, gate_row):
    @pl.when(pl.program_id(0) == 0)
    def _():
        w_sc[...] = w_ref[...].astype(BF16)

    half = out_ref.shape[0] // 2
    for part in range(2):
        rows = slice(part * half, (part + 1) * half)
        y = jnp.dot(o_ref[rows, :], w_sc[...], preferred_element_type=F32)
        z = DEEPNORM_ALPHA * h_ref[rows, :] + (1.0 + mod_ref[0, gate_row:gate_row + 1, :]) * y
        out_ref[rows, :] = _layer_norm(z, g_ref[...], b_ref[...])


def _proj_ln(o, w_o, layer, h, mod_l, ln_g, ln_b, S):
    T, D = h.shape
    tm = 512
    K = w_o.shape[1]
    row = lambda i: (i, 0)
    const = lambda i: (0, 0)
    return pl.pallas_call(
        functools.partial(_proj_ln_kernel, gate_row=2),
        grid=(T // tm,),
        in_specs=[
            pl.BlockSpec((tm, K), row),
            pl.BlockSpec((None, K, D), lambda i: (layer, 0, 0), pipeline_mode=pl.Buffered(1)),
            pl.BlockSpec((tm, D), row),
            pl.BlockSpec((1, 8, D), lambda i: (i * tm // S, 0, 0)),
            pl.BlockSpec((1, D), const),
            pl.BlockSpec((1, D), const),
        ],
        out_specs=pl.BlockSpec((tm, D), row),
        out_shape=jax.ShapeDtypeStruct((T, D), F32),
        scratch_shapes=[pltpu.VMEM((K, D), BF16)],
        compiler_params=_params("arbitrary"),
        name="proj_ln",
    )(o, w_o, h, mod_l, ln_g.reshape(1, D), ln_b.reshape(1, D))


def _ffn_kernel(h_ref, mod_ref, wg_ref, wu_ref, wd_ref, g_ref, b_ref, out_ref, u_sc):
    f = pl.program_id(1)

    @pl.when(f == 0)
    def _():
        u_sc[...] = _modulate(h_ref[...], mod_ref, 3).astype(BF16)
        out_ref[...] = jnp.zeros(out_ref.shape, F32)

    u = u_sc[...]
    g = jnp.dot(u, wg_ref[...].astype(BF16), preferred_element_type=F32)
    up = jnp.dot(u, wu_ref[...].astype(BF16), preferred_element_type=F32)
    hm = (_silu(g) * up).astype(BF16)
    out_ref[...] += jnp.dot(hm, wd_ref[...].astype(BF16), preferred_element_type=F32)

    @pl.when(f == pl.num_programs(1) - 1)
    def _():
        z = DEEPNORM_ALPHA * h_ref[...] + (1.0 + mod_ref[0, 5:6, :]) * out_ref[...]
        out_ref[...] = _layer_norm(z, g_ref[...], b_ref[...])


def _ffn(h, mod_l, w_gate, w_up, w_down, layer, ln_g, ln_b, S):
    T, D = h.shape
    F = w_gate.shape[2]
    tm = min(1024, T)
    tf = 256
    row = lambda i, f: (i, 0)
    const = lambda i, f: (0, 0)
    return pl.pallas_call(
        _ffn_kernel,
        grid=(T // tm, F // tf),
        in_specs=[
            pl.BlockSpec((tm, D), row, pipeline_mode=pl.Buffered(1)),
            pl.BlockSpec((1, 8, D), lambda i, f: (i * tm // S, 0, 0)),
            pl.BlockSpec((None, D, tf), lambda i, f: (layer, 0, f)),
            pl.BlockSpec((None, D, tf), lambda i, f: (layer, 0, f)),
            pl.BlockSpec((None, tf, D), lambda i, f: (layer, f, 0)),
            pl.BlockSpec((1, D), const),
            pl.BlockSpec((1, D), const),
        ],
        out_specs=pl.BlockSpec((tm, D), row),
        out_shape=jax.ShapeDtypeStruct((T, D), F32),
        scratch_shapes=[pltpu.VMEM((tm, D), BF16)],
        compiler_params=_params("arbitrary", "arbitrary"),
        name="ffn",
    )(h, mod_l, w_gate, w_up, w_down, ln_g.reshape(1, D), ln_b.reshape(1, D))


def _qkv_kernel(h_ref, mod_ref, w_ref, b_ref, o_ref, u_sc):
    @pl.when(pl.program_id(1) == 0)
    def _():
        u_sc[...] = _modulate(h_ref[...], mod_ref, 0).astype(BF16)

    y = jnp.dot(u_sc[...], w_ref[...], preferred_element_type=F32) + b_ref[...]
    o_ref[...] = y.astype(o_ref.dtype)


def _qkv(h, mod_l, w_qkv, layer, b_qkv, S):
    T, D = h.shape
    N = w_qkv.shape[2]
    tm = min(1024, T)
    tn = 1024
    w = w_qkv[layer].astype(BF16)
    return pl.pallas_call(
        _qkv_kernel,
        grid=(T // tm, N // tn),
        in_specs=[
            pl.BlockSpec((tm, D), lambda i, n: (i, 0), pipeline_mode=pl.Buffered(1)),
            pl.BlockSpec((1, 8, D), lambda i, n: (i * tm // S, 0, 0)),
            pl.BlockSpec((D, tn), lambda i, n: (0, n)),
            pl.BlockSpec((1, tn), lambda i, n: (0, n)),
        ],
        out_specs=pl.BlockSpec((tm, tn), lambda i, n: (i, n)),
        out_shape=jax.ShapeDtypeStruct((T, N), BF16),
        scratch_shapes=[pltpu.VMEM((tm, D), BF16)],
        compiler_params=_params("arbitrary", "arbitrary"),
        name="qkv",
    )(h, mod_l, w, b_qkv.reshape(1, N))


def _ca_attn_kernel(q_ref, k0_ref, k1_ref, k2_ref, v0_ref, v1_ref, v2_ref, bias_ref, o_ref, *, scale):
    i = pl.program_id(2)
    tq = q_ref.shape[0]
    k_refs = (k0_ref, k1_ref, k2_ref)
    v_refs = (v0_ref, v1_ref, v2_ref)

    def attend(stream_start):
        for hd in range(ATTN_HEADS_PER_STEP):
            cols = slice(hd * CA_HEAD_DIM, (hd + 1) * CA_HEAD_DIM)
            q = q_ref[:, cols]
            s = []
            for p in range(CA_KBLOCKS):
                sp = lax.dot_general(q, k_refs[p][:, cols], _NT, preferred_element_type=F32) * scale
                sp = sp + bias_ref[hd, :, p * tq:(p + 1) * tq]
                first = CA_KBLOCKS - 1 - p
                if stream_start and first > 0:
                    sp = jnp.where(i >= first, sp, -jnp.inf)
                s.append(sp)
            m = functools.reduce(jnp.maximum, [jnp.max(sp, axis=-1, keepdims=True) for sp in s])
            acc = functools.reduce(jnp.add, [
                jnp.dot(jnp.exp2(sp - m).astype(BF16), _with_ones(v_refs[p][:, cols]), preferred_element_type=F32)
                for p, sp in enumerate(s)])
            o_ref[:, cols] = (acc[:, :CA_HEAD_DIM] / acc[:, CA_HEAD_DIM:]).astype(o_ref.dtype)

    @pl.when(i < CA_KBLOCKS - 1)
    def _():
        attend(True)

    @pl.when(i >= CA_KBLOCKS - 1)
    def _():
        attend(False)


def _ca_bias_kernel(w_ref, o_ref):
    tq, nk = o_ref.shape[1], o_ref.shape[2]
    W = w_ref.shape[2]
    x = jnp.broadcast_to(w_ref[0], (tq, W))
    t = pltpu.roll(x, W - (tq - 1), 1, stride=1, stride_axis=0)[:, :nk]
    qc = lax.broadcasted_iota(jnp.int32, (tq, nk), 0) // CHUNK + (nk - tq) // CHUNK
    kc = lax.broadcasted_iota(jnp.int32, (tq, nk), 1) // CHUNK
    o_ref[0] = jnp.where((kc <= qc) & (kc >= qc - LEFT_CHUNKS), t * LOG2E, -jnp.inf)


def _ca_bias_table(rel_bias):
    H = rel_bias.shape[0]
    tq = CA_QBLOCK
    nk = CA_KBLOCKS * tq
    W = tq + nk
    d = np.arange(W) - (tq - 1)
    rel_idx = np.clip((nk - tq) - d, -MAX_REL, MAX_REL) + MAX_REL
    w = rel_bias[:, rel_idx].astype(F32).reshape(H, 1, W)
    return pl.pallas_call(
        _ca_bias_kernel,
        grid=(H,),
        in_specs=[pl.BlockSpec((1, 1, W), lambda h: (h, 0, 0))],
        out_specs=pl.BlockSpec((1, tq, nk), lambda h: (h, 0, 0)),
        out_shape=jax.ShapeDtypeStruct((H, tq, nk), F32),
        compiler_params=_params("arbitrary"),
        name="ca_bias",
    )(w)


def _ca_attn(qkv, rel_bias, B, S):
    H, Dh = CA_HEADS, CA_HEAD_DIM
    hp = ATTN_HEADS_PER_STEP
    ng = H // hp
    tq = CA_QBLOCK
    nq = S // tq
    bias = _ca_bias_table(rel_bias)

    def kv_spec(which, p):
        back = CA_KBLOCKS - 1 - p
        return pl.BlockSpec((tq, hp * Dh), lambda b, g, i: (b * nq + jnp.maximum(i - back, 0), which * ng + g))

    return pl.pallas_call(
        functools.partial(_ca_attn_kernel, scale=Dh ** -0.5 * LOG2E),
        grid=(B, ng, nq),
        in_specs=[pl.BlockSpec((tq, hp * Dh), lambda b, g, i: (b * nq + i, g))]
        + [kv_spec(1, p) for p in range(CA_KBLOCKS)]
        + [kv_spec(2, p) for p in range(CA_KBLOCKS)]
        + [pl.BlockSpec((hp, tq, CA_KBLOCKS * tq), lambda b, g, i: (g, 0, 0))],
        out_specs=pl.BlockSpec((tq, hp * Dh), lambda b, g, i: (b * nq + i, g)),
        out_shape=jax.ShapeDtypeStruct((B * S, H * Dh), BF16),
        compiler_params=_params("arbitrary", "arbitrary", "arbitrary"),
        name="ca_attn",
    )(qkv, qkv, qkv, qkv, qkv, qkv, qkv, bias)


def _pack_bf16_halves(x):
    half = x.shape[1] // 2
    lo = pltpu.bitcast(x[:, :half].astype(BF16).astype(F32), jnp.uint32)
    hi = pltpu.bitcast(x[:, half:].astype(BF16).astype(F32), jnp.uint32)
    return lax.shift_right_logical(lo, jnp.uint32(16)) | (hi & jnp.uint32(0xFFFF0000))


def _unpack_bf16_halves(p):
    lo = pltpu.bitcast(lax.shift_left(p, jnp.uint32(16)), F32).astype(BF16)
    hi = pltpu.bitcast(p & jnp.uint32(0xFFFF0000), F32).astype(BF16)
    return jnp.concatenate([lo, hi], axis=1)


def _router_kernel(h_ref, mod_ref, rw_ref, rb_ref, u_ref, idx_ref, gate_ref, rank_ref, cnt_ref, carry_sc):
    t = pl.program_id(0)

    @pl.when(t == 0)
    def _():
        carry_sc[...] = jnp.zeros(carry_sc.shape, F32)

    u = _modulate(h_ref[...], mod_ref, 3)
    u_ref[...] = _pack_bf16_halves(u)
    tb = u.shape[0]
    logits = lax.dot_general(rw_ref[...], u, _NT, precision=lax.Precision.HIGHEST,
                             preferred_element_type=F32) + rb_ref[...]
    eid = lax.broadcasted_iota(jnp.int32, logits.shape, 0)
    m1 = jnp.max(logits, axis=0, keepdims=True)
    i1 = jnp.min(jnp.where(logits == m1, eid, N_EXPERTS), axis=0, keepdims=True)
    rest = jnp.where(eid == i1, -jnp.inf, logits)
    m2 = jnp.max(rest, axis=0, keepdims=True)
    i2 = jnp.min(jnp.where(rest == m2, eid, N_EXPERTS), axis=0, keepdims=True)
    e2 = jnp.exp(m2 - m1)
    den = 1.0 + e2
    idx_ref[...] = jnp.concatenate([i1, i2], axis=0)
    gate_ref[...] = jnp.concatenate([1.0 / den, e2 / den], axis=0)
    onehot = jnp.where((eid == i1) | (eid == i2), 1.0, 0.0)
    tri = jnp.where(lax.broadcasted_iota(jnp.int32, (tb, tb), 0) <= lax.broadcasted_iota(jnp.int32, (tb, tb), 1),
                    1.0, 0.0).astype(BF16)
    incl = jnp.dot(onehot.astype(BF16), tri, preferred_element_type=F32)
    excl = incl - onehot + carry_sc[:, 0:1]
    r1 = jnp.sum(jnp.where(eid == i1, excl, 0.0), axis=0, keepdims=True)
    r2 = jnp.sum(jnp.where(eid == i2, excl, 0.0), axis=0, keepdims=True)
    rank_ref[...] = jnp.concatenate([r1, r2], axis=0).astype(jnp.int32)
    carry_sc[...] = carry_sc[...] + jnp.sum(onehot, axis=1, keepdims=True)
    cnt_ref[...] = carry_sc[...].astype(jnp.int32)


def _router(h, mod_l, router_w, router_b, S):
    T, D = h.shape
    E = N_EXPERTS
    tb = 512
    return pl.pallas_call(
        _router_kernel,
        grid=(T // tb,),
        in_specs=[
            pl.BlockSpec((tb, D), lambda t: (t, 0)),
            pl.BlockSpec((1, 8, D), lambda t: (t * tb // S, 0, 0)),
            pl.BlockSpec((E, D), lambda t: (0, 0)),
            pl.BlockSpec((E, 1), lambda t: (0, 0)),
        ],
        out_specs=[
            pl.BlockSpec((tb, D // 2), lambda t: (t, 0)),
            pl.BlockSpec((TOP_K, tb), lambda t: (0, t)),
            pl.BlockSpec((TOP_K, tb), lambda t: (0, t)),
            pl.BlockSpec((TOP_K, tb), lambda t: (0, t)),
            pl.BlockSpec((E, LANES), lambda t: (0, 0)),
        ],
        out_shape=[
            jax.ShapeDtypeStruct((T, D // 2), jnp.uint32),
            jax.ShapeDtypeStruct((TOP_K, T), jnp.int32),
            jax.ShapeDtypeStruct((TOP_K, T), F32),
            jax.ShapeDtypeStruct((TOP_K, T), jnp.int32),
            jax.ShapeDtypeStruct((E, LANES), jnp.int32),
        ],
        scratch_shapes=[pltpu.VMEM((E, LANES), F32)],
        compiler_params=_params("arbitrary"),
        name="router",
    )(h, mod_l, router_w.T, router_b.reshape(E, 1))


SUBLANES = 8


def _row_gather_group(src_hbm, dst, sem, rows_ref, base, g):
    for k in range(SUBLANES):
        r = rows_ref[base + g * SUBLANES + k]
        pltpu.make_async_copy(src_hbm.at[pl.ds(r, 1), :], dst.at[g, pl.ds(k, 1), :], sem).start(priority=k % 2)


def _row_gather_start(src_hbm, dst, sem, rows_ref, base):
    def issue(g, carry):
        _row_gather_group(src_hbm, dst, sem, rows_ref, base, g)
        return carry
    lax.fori_loop(0, dst.shape[0], issue, 0)


def _row_gather_wait(dst, sem):
    pltpu.make_async_copy(dst, dst, sem).wait()


def _gathered_rows(buf):
    return buf.reshape(buf.shape[0] * SUBLANES, buf.shape[2])


def _dispatch_kernel(row_tok_ref, blk_ref, nv_ref, u_hbm, o_ref, buf, sem):
    c = pl.program_id(0)
    nv = nv_ref[0]
    rb = MOE_ROW_BLOCK
    slot = c % 2

    def start(step, sl):
        _row_gather_start(u_hbm, buf.at[sl], sem.at[sl], row_tok_ref, blk_ref[step] * rb)

    @pl.when((c == 0) & (nv > 0))
    def _():
        start(0, 0)

    @pl.when(c + 1 < nv)
    def _():
        start(c + 1, 1 - slot)

    @pl.when(c < nv)
    def _():
        _row_gather_wait(buf.at[slot], sem.at[slot])
        o_ref[...] = _unpack_bf16_halves(_gathered_rows(buf[slot]))

    @pl.when(c >= nv)
    def _():
        o_ref[...] = jnp.zeros(o_ref.shape, BF16)


def _dispatch(u, row_tok, blk_ids, n_valid, n_rows):
    T, half = u.shape
    D = 2 * half
    rb = MOE_ROW_BLOCK
    return pl.pallas_call(
        _dispatch_kernel,
        grid_spec=pltpu.PrefetchScalarGridSpec(
            num_scalar_prefetch=3,
            grid=(blk_ids.shape[0],),
            in_specs=[pl.BlockSpec(memory_space=pl.ANY)],
            out_specs=pl.BlockSpec((rb, D), lambda c, rt, blk, nv: (blk[c], 0)),
            scratch_shapes=[pltpu.VMEM((2, rb // SUBLANES, SUBLANES, half), jnp.uint32),
                            pltpu.SemaphoreType.DMA((2,))],
        ),
        out_shape=jax.ShapeDtypeStruct((n_rows + rb, D), BF16),
        compiler_params=_params("arbitrary"),
        name="moe_dispatch",
    )(row_tok, blk_ids, n_valid, u)


def _moe_ffn_kernel(se_ref, nblk_ref, x_ref, wg_ref, wu_ref, wd_ref, y_hbm, acc, wg_sc, wu_sc, wd_sc, sem):
    s = pl.program_id(0)
    f = pl.program_id(1)
    nb = nblk_ref[s]
    rb = MOE_ROW_BLOCK

    @pl.when(nb > 0)
    def _():
        @pl.when(f == 0)
        def _():
            acc[...] = jnp.zeros(acc.shape, F32)

        wg_sc[...] = wg_ref[...].astype(BF16)
        wu_sc[...] = wu_ref[...].astype(BF16)
        wd_sc[...] = wd_ref[...].astype(BF16)

        last_f = f == pl.num_programs(1) - 1

        def out_copy(start, size):
            return pltpu.make_async_copy(
                acc.at[pl.ds(pl.multiple_of(start, rb), size), :],
                y_hbm.at[pl.ds(pl.multiple_of(s * MOE_SLOT_ROWS + start, rb), size), :], sem.at[0])

        def rows_at(start, size):
            rows = pl.ds(pl.multiple_of(start, rb), size)
            x = x_ref[rows, :]
            g = jnp.dot(x, wg_sc[...], preferred_element_type=F32)
            up = jnp.dot(x, wu_sc[...], preferred_element_type=F32)
            hm = (_silu(g) * up).astype(BF16)
            acc[rows, :] += jnp.dot(hm, wd_sc[...], preferred_element_type=F32)

            @pl.when(last_f)
            def _():
                out_copy(start, size).start()

        def quad(r, carry):
            rows_at(r * (4 * rb), 4 * rb)
            return carry

        lax.fori_loop(0, nb // 4, quad, 0)

        @pl.when(nb % 4 >= 2)
        def _():
            rows_at((nb // 4) * (4 * rb), 2 * rb)

        @pl.when(nb % 2 == 1)
        def _():
            rows_at((nb - 1) * rb, rb)

        @pl.when(last_f)
        def _():
            def wait(r, carry):
                out_copy(r * rb, rb).wait()
                return carry

            lax.fori_loop(0, nb, wait, 0)


def _moe_ffn(xs, slot_e, nblk, n_used, w_gate, w_up, w_down, layer):
    D = xs.shape[1]
    F = w_gate.shape[3]
    R = MOE_SLOT_ROWS
    n_slots = slot_e.shape[0]
    tf = 256
    nf = F // tf

    def f_eff(s, f, nb):
        return jnp.where(nb[s] > 0, f, nf - 1)

    return pl.pallas_call(
        _moe_ffn_kernel,
        grid_spec=pltpu.PrefetchScalarGridSpec(
            num_scalar_prefetch=2,
            grid=(n_used, nf),
            in_specs=[
                pl.BlockSpec((R, D), lambda s, f, se, nb: (s, 0), pipeline_mode=pl.Buffered(1)),
                pl.BlockSpec((None, None, D, tf), lambda s, f, se, nb: (layer, se[s], 0, f_eff(s, f, nb))),
                pl.BlockSpec((None, None, D, tf), lambda s, f, se, nb: (layer, se[s], 0, f_eff(s, f, nb))),
                pl.BlockSpec((None, None, tf, D), lambda s, f, se, nb: (layer, se[s], f_eff(s, f, nb), 0)),
            ],
            out_specs=pl.BlockSpec(memory_space=pl.ANY),
            scratch_shapes=[pltpu.VMEM((R, D), F32), pltpu.VMEM((D, tf), BF16), pltpu.VMEM((D, tf), BF16),
                            pltpu.VMEM((tf, D), BF16), pltpu.SemaphoreType.DMA((1,))],
        ),
        out_shape=jax.ShapeDtypeStruct((n_slots * R, D), F32),
        compiler_params=_params("arbitrary", "arbitrary"),
        name="moe_ffn",
    )(slot_e, nblk, xs, w_gate, w_up, w_down)


def _combine_kernel(dest_ref, y_hbm, gate_ref, h_ref, mod_ref, g_ref, b_ref, out_ref, buf0, buf1, sem, *, n_tok):
    i = pl.program_id(0)
    tb = out_ref.shape[0]
    slot = i % 2

    def start(step, sl):
        _row_gather_start(y_hbm, buf0.at[sl], sem.at[0, sl], dest_ref, step * tb)
        _row_gather_start(y_hbm, buf1.at[sl], sem.at[1, sl], dest_ref, n_tok + step * tb)

    @pl.when(i == 0)
    def _():
        start(0, 0)

    @pl.when(i + 1 < pl.num_programs(0))
    def _():
        start(i + 1, 1 - slot)

    _row_gather_wait(buf0.at[slot], sem.at[0, slot])
    _row_gather_wait(buf1.at[slot], sem.at[1, slot])
    y = gate_ref[:, 0:1] * _gathered_rows(buf0[slot]) + gate_ref[:, 1:2] * _gathered_rows(buf1[slot])
    z = DEEPNORM_ALPHA * h_ref[...] + (1.0 + mod_ref[0, 5:6, :]) * y
    out_ref[...] = _layer_norm(z, g_ref[...], b_ref[...])


def _combine(y, dest, gates_tk, h, mod_l, ln_g, ln_b, S):
    T, D = h.shape
    tb = 256
    return pl.pallas_call(
        functools.partial(_combine_kernel, n_tok=T),
        grid_spec=pltpu.PrefetchScalarGridSpec(
            num_scalar_prefetch=1,
            grid=(T // tb,),
            in_specs=[
                pl.BlockSpec(memory_space=pl.ANY),
                pl.BlockSpec((tb, TOP_K), lambda i, d: (i, 0)),
                pl.BlockSpec((tb, D), lambda i, d: (i, 0)),
                pl.BlockSpec((1, 8, D), lambda i, d: (i * tb // S, 0, 0)),
                pl.BlockSpec((1, D), lambda i, d: (0, 0)),
                pl.BlockSpec((1, D), lambda i, d: (0, 0)),
            ],
            out_specs=pl.BlockSpec((tb, D), lambda i, d: (i, 0)),
            scratch_shapes=[pltpu.VMEM((2, tb // SUBLANES, SUBLANES, D), F32),
                            pltpu.VMEM((2, tb // SUBLANES, SUBLANES, D), F32),
                            pltpu.SemaphoreType.DMA((2, 2))],
        ),
        out_shape=jax.ShapeDtypeStruct((T, D), F32),
        compiler_params=_params("arbitrary"),
        name="moe_combine",
    )(dest, y, gates_tk, h, mod_l, ln_g.reshape(1, D), ln_b.reshape(1, D))


def _moe_layer(h, mod_l, router_w, router_b, w_gate, w_up, w_down, layer, ln_g, ln_b, S):
    T, D = h.shape
    E, R, rb = N_EXPERTS, MOE_SLOT_ROWS, MOE_ROW_BLOCK
    n_slots = (T * TOP_K + E * (R - 1)) // R
    n_rows = n_slots * R
    per_slot = R // rb
    max_blocks = T * TOP_K // rb + E
    u, idx, gates, rank, cnt = _router(h, mod_l, router_w, router_b, S)
    cnt = cnt[:, 0]
    slots_e = (cnt + R - 1) // R
    slot_end = jnp.cumsum(slots_e)
    slot_start = slot_end - slots_e
    row_start = slot_start * R
    dest = rank
    for e in range(E):
        dest = dest + jnp.where(idx == e, row_start[e], 0)
    tok = jnp.tile(jnp.arange(T, dtype=jnp.int32), TOP_K)
    row_tok = jnp.zeros((n_rows,), jnp.int32).at[dest.reshape(-1)].set(
        tok, unique_indices=True, mode='promise_in_bounds')
    sid = jnp.arange(n_slots, dtype=jnp.int32)
    n_used = slot_end[-1]
    last_used = jnp.maximum(n_used - 1, 0)
    se = jnp.minimum(jnp.searchsorted(slot_end, jnp.minimum(sid, last_used), side='right'), E - 1).astype(jnp.int32)
    rows_in = jnp.clip(cnt[se] - (sid - slot_start[se]) * R, 0, R)
    rows_in = jnp.where(sid < n_used, rows_in, 0)
    nblk = ((rows_in + rb - 1) // rb).astype(jnp.int32)
    blk = jnp.arange(n_rows // rb, dtype=jnp.int32)
    blk_valid = (blk % per_slot) < nblk[blk // per_slot]
    order = jnp.argsort(jnp.logical_not(blk_valid), stable=True).astype(jnp.int32)
    n_valid = jnp.sum(blk_valid.astype(jnp.int32))
    blk_ids = jnp.where(jnp.arange(max_blocks) < n_valid, order[:max_blocks], n_rows // rb).astype(jnp.int32)
    xs = _dispatch(u, row_tok, blk_ids, n_valid.reshape(1), n_rows)
    y = _moe_ffn(xs, se, nblk, n_used.astype(jnp.int32), w_gate, w_up, w_down, layer)
    return _combine(y, dest.reshape(-1).astype(jnp.int32), gates.T, h, mod_l, ln_g, ln_b, S)


def kernel(x, c, positions, ada_w, ada_b, ada_table, ln_g, ln_b, mla_w_dq, mla_q_norm, mla_w_uq, mla_w_dkv, mla_kv_norm, mla_w_ukv, mla_w_o, ca_w_qkv, ca_b_qkv, ca_rel_bias, ca_w_o, ffn_w_gate, ffn_w_up, ffn_w_down, moe_router_w, moe_router_b, moe_w_gate, moe_w_up, moe_w_down):
    B, S, D = x.shape
    T = B * S
    mod = _cond(c, ada_w, ada_b, ada_table)
    half = QK_ROPE // 2
    inv_freq = ROPE_THETA ** (-jnp.arange(half, dtype=F32) / half)
    freq_row = jnp.concatenate([inv_freq, inv_freq, jnp.zeros((LANES - QK_ROPE,), F32)]).reshape(1, LANES)
    pos_col = positions.reshape(T, 1).astype(jnp.int32)
    h = x.reshape(T, D)
    for i in range(DEPTH):
        j = i // 2
        mod_l = mod[i]
        if i % 2 == 0:
            q, k, v = _mla_proj(h, mod_l, pos_col, freq_row, mla_w_dq[j], mla_q_norm[j], mla_w_dkv[j],
                                mla_kv_norm[j], mla_w_uq[j], mla_w_ukv[j], S)
            o = _mla_attn(q, k, v, B, S)
            h = _proj_ln(o, mla_w_o, j, h, mod_l, ln_g[i, 0], ln_b[i, 0], S)
            h = _ffn(h, mod_l, ffn_w_gate, ffn_w_up, ffn_w_down, j, ln_g[i, 1], ln_b[i, 1], S)
        else:
            qkv = _qkv(h, mod_l, ca_w_qkv, j, ca_b_qkv[j], S)
            o = _ca_attn(qkv, ca_rel_bias[j], B, S)
            h = _proj_ln(o, ca_w_o, j, h, mod_l, ln_g[i, 0], ln_b[i, 0], S)
            h = _moe_layer(h, mod_l, moe_router_w[j], moe_router_b[j], moe_w_gate, moe_w_up,
                           moe_w_down, j, ln_g[i, 1], ln_b[i, 1], S)
    return h.reshape(B, S, D)
```

```python
import functools

import numpy as np
import jax
import jax.numpy as jnp
from jax import lax
from jax.experimental import pallas as pl
from jax.experimental.pallas import tpu as pltpu

F32 = jnp.float32
BF16 = jnp.bfloat16

CHUNK = 64
MLA_HEADS = 16
QK_NOPE = 128
QK_ROPE = 64
V_HEAD = 128
KV_LORA = 512
ROPE_THETA = 10000.0
CA_HEADS = 16
CA_HEAD_DIM = 128
LEFT_CHUNKS = 8
MAX_REL = 256
N_EXPERTS = 8
TOP_K = 2
N_MOD = 6
LN_EPS = 1e-5
RMS_EPS = 1e-6
DEPTH = 4
DEEPNORM_ALPHA = (2.0 * DEPTH) ** 0.25
LOG2E = 1.4426950408889634

LANES = 128
VMEM_LIMIT = 56 * 1024 * 1024

MLA_HEAD_PAD = 2 * LANES
MOE_SLOT_ROWS = 2560
MOE_ROW_BLOCK = 256
CA_QBLOCK = 256
CA_KBLOCKS = 3
ATTN_HEADS_PER_STEP = 16

_NT = (((1,), (1,)), ((), ()))


def _params(*sem):
    return pltpu.CompilerParams(dimension_semantics=sem, vmem_limit_bytes=VMEM_LIMIT)


def _silu(x):
    return x * jax.nn.sigmoid(x)


def _layer_norm(z, g, b):
    mu = jnp.mean(z, axis=-1, keepdims=True)
    zc = z - mu
    var = jnp.mean(zc * zc, axis=-1, keepdims=True)
    return zc * lax.rsqrt(var + LN_EPS) * g + b


def _modulate(h, mod_ref, k):
    return h * (1.0 + mod_ref[0, k + 1:k + 2, :]) + mod_ref[0, k:k + 1, :]


def _cond_kernel(c_ref, w_ref, b_ref, tab_ref, o_ref):
    s = _silu(c_ref[...]).astype(BF16)
    y = jnp.dot(s, w_ref[...].astype(BF16), preferred_element_type=F32) + b_ref[...]
    o_ref[...] = y[None, :, :] + tab_ref[...][:, None, :]


def _cond(c, ada_w, ada_b, ada_table):
    B, D = c.shape
    N = ada_w.shape[1]
    tn = 1024
    c8 = jnp.zeros((8, D), F32).at[:B].set(c)
    out = pl.pallas_call(
        _cond_kernel,
        grid=(N // tn,),
        in_specs=[
            pl.BlockSpec((8, D), lambda n: (0, 0)),
            pl.BlockSpec((D, tn), lambda n: (0, n)),
            pl.BlockSpec((1, tn), lambda n: (0, n)),
            pl.BlockSpec((DEPTH, tn), lambda n: (0, n)),
        ],
        out_specs=pl.BlockSpec((DEPTH, 8, tn), lambda n: (0, 0, n)),
        out_shape=jax.ShapeDtypeStruct((DEPTH, 8, N), F32),
        compiler_params=_params("arbitrary"),
        name="cond",
    )(c8, ada_w, ada_b.reshape(1, N), ada_table.reshape(DEPTH, N))
    mod = out[:, :B].reshape(DEPTH, B, N_MOD, D)
    return jnp.pad(mod, ((0, 0), (0, 0), (0, 8 - N_MOD), (0, 0)))


def _rope_lanes(x, cos, sin_signed):
    lane = lax.broadcasted_iota(jnp.int32, x.shape, 1)
    partner = jnp.where(lane < QK_ROPE // 2, pltpu.roll(x, LANES - QK_ROPE // 2, 1),
                        pltpu.roll(x, QK_ROPE // 2, 1))
    return x * cos + partner * sin_signed


def _mla_proj_kernel(h_ref, mod_ref, pos_ref, freq_ref, wdq_ref, qn_ref, wdkv_ref, kvn_ref, wkr_ref,
                     wuq_ref, wukv_ref, q_ref, k_ref, v_ref):
    u = _modulate(h_ref[...], mod_ref, 0).astype(BF16)
    ang = pos_ref[...].astype(F32) * freq_ref[...]
    cos = jnp.cos(ang)
    sin = jnp.sin(ang)
    lane = lax.broadcasted_iota(jnp.int32, ang.shape, 1)
    sin_signed = jnp.where(lane < QK_ROPE // 2, -sin, sin)

    def rms(x, g):
        return (x * lax.rsqrt(jnp.mean(x * x, axis=-1, keepdims=True) + RMS_EPS)) * g

    cq = rms(jnp.dot(u, wdq_ref[...], preferred_element_type=F32), qn_ref[...]).astype(BF16)
    ckv = rms(jnp.dot(u, wdkv_ref[...], preferred_element_type=F32), kvn_ref[...]).astype(BF16)
    kr = jnp.dot(u, wkr_ref[...], preferred_element_type=F32)
    kr = _rope_lanes(kr, cos, sin_signed).astype(BF16)
    for hd in range(MLA_HEADS):
        c0 = hd * MLA_HEAD_PAD
        qh = jnp.dot(cq, wuq_ref[:, c0:c0 + MLA_HEAD_PAD], preferred_element_type=F32)
        q_ref[:, c0:c0 + LANES] = qh[:, :LANES].astype(BF16)
        q_ref[:, c0 + LANES:c0 + MLA_HEAD_PAD] = _rope_lanes(qh[:, LANES:], cos, sin_signed).astype(BF16)
        kvh = jnp.dot(ckv, wukv_ref[:, c0:c0 + MLA_HEAD_PAD], preferred_element_type=F32)
        k_ref[:, c0:c0 + LANES] = kvh[:, :LANES].astype(BF16)
        k_ref[:, c0 + LANES:c0 + MLA_HEAD_PAD] = kr
        v_ref[:, hd * V_HEAD:(hd + 1) * V_HEAD] = kvh[:, LANES:].astype(BF16)


def _mla_proj(h, mod_l, pos_col, freq_row, w_dq, q_norm, w_dkv, kv_norm, w_uq, w_ukv, S):
    T, D = h.shape
    H = MLA_HEADS
    tm = 256
    q_lora = w_dq.shape[1]
    wdq = w_dq.astype(BF16)
    wdkv = w_dkv[:, :KV_LORA].astype(BF16)
    wkr = jnp.pad(w_dkv[:, KV_LORA:], ((0, 0), (0, LANES - QK_ROPE))).astype(BF16)
    wuq = jnp.pad(w_uq.reshape(q_lora, H, QK_NOPE + QK_ROPE),
                  ((0, 0), (0, 0), (0, MLA_HEAD_PAD - QK_NOPE - QK_ROPE))).reshape(q_lora, H * MLA_HEAD_PAD)
    wuq = wuq.astype(BF16)
    wukv = w_ukv.astype(BF16)
    const = lambda i: (0, 0)
    row = lambda i: (i, 0)
    return pl.pallas_call(
        _mla_proj_kernel,
        grid=(T // tm,),
        in_specs=[
            pl.BlockSpec((tm, D), row),
            pl.BlockSpec((1, 8, D), lambda i: (i * tm // S, 0, 0)),
            pl.BlockSpec((tm, 1), row),
            pl.BlockSpec((1, LANES), const),
            pl.BlockSpec(wdq.shape, const),
            pl.BlockSpec((1, q_lora), const),
            pl.BlockSpec(wdkv.shape, const),
            pl.BlockSpec((1, KV_LORA), const),
            pl.BlockSpec(wkr.shape, const),
            pl.BlockSpec(wuq.shape, const),
            pl.BlockSpec(wukv.shape, const),
        ],
        out_specs=[
            pl.BlockSpec((tm, H * MLA_HEAD_PAD), row),
            pl.BlockSpec((tm, H * MLA_HEAD_PAD), row),
            pl.BlockSpec((tm, H * V_HEAD), row),
        ],
        out_shape=[
            jax.ShapeDtypeStruct((T, H * MLA_HEAD_PAD), BF16),
            jax.ShapeDtypeStruct((T, H * MLA_HEAD_PAD), BF16),
            jax.ShapeDtypeStruct((T, H * V_HEAD), BF16),
        ],
        compiler_params=_params("arbitrary"),
        name="mla_proj",
    )(h, mod_l, pos_col, freq_row, wdq, q_norm.reshape(1, -1), wdkv, kv_norm.reshape(1, -1), wkr, wuq, wukv)


def _with_ones(v):
    return jnp.concatenate([v, jnp.ones_like(v)], axis=1)


def _mla_attn_kernel(pi_ref, pj_ref, q_ref, k_ref, v_ref, o_ref, m_sc, acc_sc, *, scale):
    p = pl.program_id(2)
    i = pi_ref[p]
    j = pj_ref[p]

    @pl.when(j == 0)
    def _():
        m_sc[...] = jnp.full(m_sc.shape, -jnp.inf, F32)
        acc_sc[...] = jnp.zeros(acc_sc.shape, F32)

    def accumulate(diagonal):
        for hd in range(ATTN_HEADS_PER_STEP):
            qk = slice(hd * MLA_HEAD_PAD, (hd + 1) * MLA_HEAD_PAD)
            s = lax.dot_general(q_ref[:, qk], k_ref[:, qk], _NT, preferred_element_type=F32) * scale
            if diagonal:
                qc = lax.broadcasted_iota(jnp.int32, s.shape, 0) // CHUNK
                kc = lax.broadcasted_iota(jnp.int32, s.shape, 1) // CHUNK
                s = jnp.where(kc <= qc, s, -jnp.inf)
            m_prev = m_sc[hd]
            m_new = jnp.maximum(m_prev, jnp.max(s, axis=-1, keepdims=True))
            a = jnp.exp2(m_prev - m_new)
            e = jnp.exp2(s - m_new).astype(BF16)
            v = _with_ones(v_ref[:, hd * V_HEAD:(hd + 1) * V_HEAD])
            acc_sc[hd] = a * acc_sc[hd] + jnp.dot(e, v, preferred_element_type=F32)
            m_sc[hd] = m_new

    @pl.when(j < i)
    def _():
        accumulate(False)

    @pl.when(j == i)
    def _():
        accumulate(True)
        for hd in range(ATTN_HEADS_PER_STEP):
            acc = acc_sc[hd]
            o_ref[:, hd * V_HEAD:(hd + 1) * V_HEAD] = (acc[:, :V_HEAD] / acc[:, V_HEAD:]).astype(o_ref.dtype)


def _mla_attn(q, k, v, B, S):
    H = MLA_HEADS
    hp = ATTN_HEADS_PER_STEP
    tq = 512
    nq = S // tq
    scale = (QK_NOPE + QK_ROPE) ** -0.5 * LOG2E
    pairs = [(i, j) for i in range(nq) for j in range(i + 1)]
    pi = jnp.asarray([p[0] for p in pairs], jnp.int32)
    pj = jnp.asarray([p[1] for p in pairs], jnp.int32)
    return pl.pallas_call(
        functools.partial(_mla_attn_kernel, scale=scale),
        grid_spec=pltpu.PrefetchScalarGridSpec(
            num_scalar_prefetch=2,
            grid=(B, H // hp, len(pairs)),
            in_specs=[
                pl.BlockSpec((tq, hp * MLA_HEAD_PAD), lambda b, g, p, pi, pj: (b * nq + pi[p], g)),
                pl.BlockSpec((tq, hp * MLA_HEAD_PAD), lambda b, g, p, pi, pj: (b * nq + pj[p], g)),
                pl.BlockSpec((tq, hp * V_HEAD), lambda b, g, p, pi, pj: (b * nq + pj[p], g)),
            ],
            out_specs=pl.BlockSpec((tq, hp * V_HEAD), lambda b, g, p, pi, pj: (b * nq + pi[p], g)),
            scratch_shapes=[pltpu.VMEM((hp, tq, 1), F32), pltpu.VMEM((hp, tq, 2 * V_HEAD), F32)],
        ),
        out_shape=jax.ShapeDtypeStruct((B * S, H * V_HEAD), BF16),
        compiler_params=_params("arbitrary", "arbitrary", "arbitrary"),
        name="mla_attn",
    )(pi, pj, q, k, v)


def _proj_ln_kernel(o_ref, w_ref, h_ref, mod_ref, g_ref, b_ref, out_ref, w_sc, *, gate_row):
    @pl.when(pl.program_id(0) == 0)
    def _():
        w_sc[...] = w_ref[...].astype(BF16)

    half = out_ref.shape[0] // 2
    for part in range(2):
        rows = slice(part * half, (part + 1) * half)
        y = jnp.dot(o_ref[rows, :], w_sc[...], preferred_element_type=F32)
        z = DEEPNORM_ALPHA * h_ref[rows, :] + (1.0 + mod_ref[0, gate_row:gate_row + 1, :]) * y
        out_ref[rows, :] = _layer_norm(z, g_ref[...], b_ref[...])


def _proj_ln(o, w_o, layer, h, mod_l, ln_g, ln_b, S):
    T, D = h.shape
    tm = 512
    K = w_o.shape[1]
    row = lambda i: (i, 0)
    const = lambda i: (0, 0)
    return pl.pallas_call(
        functools.partial(_proj_ln_kernel, gate_row=2),
        grid=(T // tm,),
        in_specs=[
            pl.BlockSpec((tm, K), row),
            pl.BlockSpec((None, K, D), lambda i: (layer, 0, 0), pipeline_mode=pl.Buffered(1)),
            pl.BlockSpec((tm, D), row),
            pl.BlockSpec((1, 8, D), lambda i: (i * tm // S, 0, 0)),
            pl.BlockSpec((1, D), const),
            pl.BlockSpec((1, D), const),
        ],
        out_specs=pl.BlockSpec((tm, D), row),
        out_shape=jax.ShapeDtypeStruct((T, D), F32),
        scratch_shapes=[pltpu.VMEM((K, D), BF16)],
        compiler_params=_params("arbitrary"),
        name="proj_ln",
    )(o, w_o, h, mod_l, ln_g.reshape(1, D), ln_b.reshape(1, D))


def _ffn_kernel(h_ref, mod_ref, wg_ref, wu_ref, wd_ref, g_ref, b_ref, out_ref, u_sc):
    f = pl.program_id(1)

    @pl.when(f == 0)
    def _():
        u_sc[...] = _modulate(h_ref[...], mod_ref, 3).astype(BF16)
        out_ref[...] = jnp.zeros(out_ref.shape, F32)

    u = u_sc[...]
    g = jnp.dot(u, wg_ref[...].astype(BF16), preferred_element_type=F32)
    up = jnp.dot(u, wu_ref[...].astype(BF16), preferred_element_type=F32)
    hm = (_silu(g) * up).astype(BF16)
    out_ref[...] += jnp.dot(hm, wd_ref[...].astype(BF16), preferred_element_type=F32)

    @pl.when(f == pl.num_programs(1) - 1)
    def _():
        z = DEEPNORM_ALPHA * h_ref[...] + (1.0 + mod_ref[0, 5:6, :]) * out_ref[...]
        out_ref[...] = _layer_norm(z, g_ref[...], b_ref[...])


def _ffn(h, mod_l, w_gate, w_up, w_down, layer, ln_g, ln_b, S):
    T, D = h.shape
    F = w_gate.shape[2]
    tm = min(1024, T)
    tf = 256
    row = lambda i, f: (i, 0)
    const = lambda i, f: (0, 0)
    return pl.pallas_call(
        _ffn_kernel,
        grid=(T // tm, F // tf),
        in_specs=[
            pl.BlockSpec((tm, D), row, pipeline_mode=pl.Buffered(1)),
            pl.BlockSpec((1, 8, D), lambda i, f: (i * tm // S, 0, 0)),
            pl.BlockSpec((None, D, tf), lambda i, f: (layer, 0, f)),
            pl.BlockSpec((None, D, tf), lambda i, f: (layer, 0, f)),
            pl.BlockSpec((None, tf, D), lambda i, f: (layer, f, 0)),
            pl.BlockSpec((1, D), const),
            pl.BlockSpec((1, D), const),
        ],
        out_specs=pl.BlockSpec((tm, D), row),
        out_shape=jax.ShapeDtypeStruct((T, D), F32),
        scratch_shapes=[pltpu.VMEM((tm, D), BF16)],
        compiler_params=_params("arbitrary", "arbitrary"),
        name="ffn",
    )(h, mod_l, w_gate, w_up, w_down, ln_g.reshape(1, D), ln_b.reshape(1, D))


def _qkv_kernel(h_ref, mod_ref, w_ref, b_ref, o_ref, u_sc):
    @pl.when(pl.program_id(1) == 0)
    def _():
        u_sc[...] = _modulate(h_ref[...], mod_ref, 0).astype(BF16)

    y = jnp.dot(u_sc[...], w_ref[...], preferred_element_type=F32) + b_ref[...]
    o_ref[...] = y.astype(o_ref.dtype)


def _qkv(h, mod_l, w_qkv, layer, b_qkv, S):
    T, D = h.shape
    N = w_qkv.shape[2]
    tm = min(1024, T)
    tn = 1024
    w = w_qkv[layer].astype(BF16)
    return pl.pallas_call(
        _qkv_kernel,
        grid=(T // tm, N // tn),
        in_specs=[
            pl.BlockSpec((tm, D), lambda i, n: (i, 0), pipeline_mode=pl.Buffered(1)),
            pl.BlockSpec((1, 8, D), lambda i, n: (i * tm // S, 0, 0)),
            pl.BlockSpec((D, tn), lambda i, n: (0, n)),
            pl.BlockSpec((1, tn), lambda i, n: (0, n)),
        ],
        out_specs=pl.BlockSpec((tm, tn), lambda i, n: (i, n)),
        out_shape=jax.ShapeDtypeStruct((T, N), BF16),
        scratch_shapes=[pltpu.VMEM((tm, D), BF16)],
        compiler_params=_params("arbitrary", "arbitrary"),
        name="qkv",
    )(h, mod_l, w, b_qkv.reshape(1, N))


def _ca_attn_kernel(q_ref, k0_ref, k1_ref, k2_ref, v0_ref, v1_ref, v2_ref, bias_ref, o_ref, *, scale):
    i = pl.program_id(2)
    tq = q_ref.shape[0]
    k_refs = (k0_ref, k1_ref, k2_ref)
    v_refs = (v0_ref, v1_ref, v2_ref)

    def attend(stream_start):
        for hd in range(ATTN_HEADS_PER_STEP):
            cols = slice(hd * CA_HEAD_DIM, (hd + 1) * CA_HEAD_DIM)
            q = q_ref[:, cols]
            s = []
            for p in range(CA_KBLOCKS):
                sp = lax.dot_general(q, k_refs[p][:, cols], _NT, preferred_element_type=F32) * scale
                sp = sp + bias_ref[hd, :, p * tq:(p + 1) * tq]
                first = CA_KBLOCKS - 1 - p
                if stream_start and first > 0:
                    sp = jnp.where(i >= first, sp, -jnp.inf)
                s.append(sp)
            m = functools.reduce(jnp.maximum, [jnp.max(sp, axis=-1, keepdims=True) for sp in s])
            acc = functools.reduce(jnp.add, [
                jnp.dot(jnp.exp2(sp - m).astype(BF16), _with_ones(v_refs[p][:, cols]), preferred_element_type=F32)
                for p, sp in enumerate(s)])
            o_ref[:, cols] = (acc[:, :CA_HEAD_DIM] / acc[:, CA_HEAD_DIM:]).astype(o_ref.dtype)

    @pl.when(i < CA_KBLOCKS - 1)
    def _():
        attend(True)

    @pl.when(i >= CA_KBLOCKS - 1)
    def _():
        attend(False)


def _ca_bias_kernel(w_ref, o_ref):
    tq, nk = o_ref.shape[1], o_ref.shape[2]
    W = w_ref.shape[2]
    x = jnp.broadcast_to(w_ref[0], (tq, W))
    t = pltpu.roll(x, W - (tq - 1), 1, stride=1, stride_axis=0)[:, :nk]
    qc = lax.broadcasted_iota(jnp.int32, (tq, nk), 0) // CHUNK + (nk - tq) // CHUNK
    kc = lax.broadcasted_iota(jnp.int32, (tq, nk), 1) // CHUNK
    o_ref[0] = jnp.where((kc <= qc) & (kc >= qc - LEFT_CHUNKS), t * LOG2E, -jnp.inf)


def _ca_bias_table(rel_bias):
    H = rel_bias.shape[0]
    tq = CA_QBLOCK
    nk = CA_KBLOCKS * tq
    W = tq + nk
    d = np.arange(W) - (tq - 1)
    rel_idx = np.clip((nk - tq) - d, -MAX_REL, MAX_REL) + MAX_REL
    w = rel_bias[:, rel_idx].astype(F32).reshape(H, 1, W)
    return pl.pallas_call(
        _ca_bias_kernel,
        grid=(H,),
        in_specs=[pl.BlockSpec((1, 1, W), lambda h: (h, 0, 0))],
        out_specs=pl.BlockSpec((1, tq, nk), lambda h: (h, 0, 0)),
        out_shape=jax.ShapeDtypeStruct((H, tq, nk), F32),
        compiler_params=_params("arbitrary"),
        name="ca_bias",
    )(w)


def _ca_attn(qkv, rel_bias, B, S):
    H, Dh = CA_HEADS, CA_HEAD_DIM
    hp = ATTN_HEADS_PER_STEP
    ng = H // hp
    tq = CA_QBLOCK
    nq = S // tq
    bias = _ca_bias_table(rel_bias)

    def kv_spec(which, p):
        back = CA_KBLOCKS - 1 - p
        return pl.BlockSpec((tq, hp * Dh), lambda b, g, i: (b * nq + jnp.maximum(i - back, 0), which * ng + g))

    return pl.pallas_call(
        functools.partial(_ca_attn_kernel, scale=Dh ** -0.5 * LOG2E),
        grid=(B, ng, nq),
        in_specs=[pl.BlockSpec((tq, hp * Dh), lambda b, g, i: (b * nq + i, g))]
        + [kv_spec(1, p) for p in range(CA_KBLOCKS)]
        + [kv_spec(2, p) for p in range(CA_KBLOCKS)]
        + [pl.BlockSpec((hp, tq, CA_KBLOCKS * tq), lambda b, g, i: (g, 0, 0))],
        out_specs=pl.BlockSpec((tq, hp * Dh), lambda b, g, i: (b * nq + i, g)),
        out_shape=jax.ShapeDtypeStruct((B * S, H * Dh), BF16),
        compiler_params=_params("arbitrary", "arbitrary", "arbitrary"),
        name="ca_attn",
    )(qkv, qkv, qkv, qkv, qkv, qkv, qkv, bias)


def _pack_bf16_halves(x):
    half = x.shape[1] // 2
    lo = pltpu.bitcast(x[:, :half].astype(BF16).astype(F32), jnp.uint32)
    hi = pltpu.bitcast(x[:, half:].astype(BF16).astype(F32), jnp.uint32)
    return lax.shift_right_logical(lo, jnp.uint32(16)) | (hi & jnp.uint32(0xFFFF0000))


def _unpack_bf16_halves(p):
    lo = pltpu.bitcast(lax.shift_left(p, jnp.uint32(16)), F32).astype(BF16)
    hi = pltpu.bitcast(p & jnp.uint32(0xFFFF0000), F32).astype(BF16)
    return jnp.concatenate([lo, hi], axis=1)


def _router_kernel(h_ref, mod_ref, rw_ref, rb_ref, u_ref, idx_ref, gate_ref, rank_ref, cnt_ref, carry_sc):
    t = pl.program_id(0)

    @pl.when(t == 0)
    def _():
        carry_sc[...] = jnp.zeros(carry_sc.shape, F32)

    u = _modulate(h_ref[...], mod_ref, 3)
    u_ref[...] = _pack_bf16_halves(u)
    tb = u.shape[0]
    logits = lax.dot_general(rw_ref[...], u, _NT, precision=lax.Precision.HIGHEST,
                             preferred_element_type=F32) + rb_ref[...]
    eid = lax.broadcasted_iota(jnp.int32, logits.shape, 0)
    m1 = jnp.max(logits, axis=0, keepdims=True)
    i1 = jnp.min(jnp.where(logits == m1, eid, N_EXPERTS), axis=0, keepdims=True)
    rest = jnp.where(eid == i1, -jnp.inf, logits)
    m2 = jnp.max(rest, axis=0, keepdims=True)
    i2 = jnp.min(jnp.where(rest == m2, eid, N_EXPERTS), axis=0, keepdims=True)
    e2 = jnp.exp(m2 - m1)
    den = 1.0 + e2
    idx_ref[...] = jnp.concatenate([i1, i2], axis=0)
    gate_ref[...] = jnp.concatenate([1.0 / den, e2 / den], axis=0)
    onehot = jnp.where((eid == i1) | (eid == i2), 1.0, 0.0)
    tri = jnp.where(lax.broadcasted_iota(jnp.int32, (tb, tb), 0) <= lax.broadcasted_iota(jnp.int32, (tb, tb), 1),
                    1.0, 0.0).astype(BF16)
    incl = jnp.dot(onehot.astype(BF16), tri, preferred_element_type=F32)
    excl = incl - onehot + carry_sc[:, 0:1]
    r1 = jnp.sum(jnp.where(eid == i1, excl, 0.0), axis=0, keepdims=True)
    r2 = jnp.sum(jnp.where(eid == i2, excl, 0.0), axis=0, keepdims=True)
    rank_ref[...] = jnp.concatenate([r1, r2], axis=0).astype(jnp.int32)
    carry_sc[...] = carry_sc[...] + jnp.sum(onehot, axis=1, keepdims=True)
    cnt_ref[...] = carry_sc[...].astype(jnp.int32)


def _router(h, mod_l, router_w, router_b, S):
    T, D = h.shape
    E = N_EXPERTS
    tb = 512
    return pl.pallas_call(
        _router_kernel,
        grid=(T // tb,),
        in_specs=[
            pl.BlockSpec((tb, D), lambda t: (t, 0)),
            pl.BlockSpec((1, 8, D), lambda t: (t * tb // S, 0, 0)),
            pl.BlockSpec((E, D), lambda t: (0, 0)),
            pl.BlockSpec((E, 1), lambda t: (0, 0)),
        ],
        out_specs=[
            pl.BlockSpec((tb, D // 2), lambda t: (t, 0)),
            pl.BlockSpec((TOP_K, tb), lambda t: (0, t)),
            pl.BlockSpec((TOP_K, tb), lambda t: (0, t)),
            pl.BlockSpec((TOP_K, tb), lambda t: (0, t)),
            pl.BlockSpec((E, LANES), lambda t: (0, 0)),
        ],
        out_shape=[
            jax.ShapeDtypeStruct((T, D // 2), jnp.uint32),
            jax.ShapeDtypeStruct((TOP_K, T), jnp.int32),
            jax.ShapeDtypeStruct((TOP_K, T), F32),
            jax.ShapeDtypeStruct((TOP_K, T), jnp.int32),
            jax.ShapeDtypeStruct((E, LANES), jnp.int32),
        ],
        scratch_shapes=[pltpu.VMEM((E, LANES), F32)],
        compiler_params=_params("arbitrary"),
        name="router",
    )(h, mod_l, router_w.T, router_b.reshape(E, 1))


SUBLANES = 8


def _row_gather_group(src_hbm, dst, sem, rows_ref, base, g):
    for k in range(SUBLANES):
        r = rows_ref[base + g * SUBLANES + k]
        pltpu.make_async_copy(src_hbm.at[pl.ds(r, 1), :], dst.at[g, pl.ds(k, 1), :], sem).start(priority=k % 2)


def _row_gather_start(src_hbm, dst, sem, rows_ref, base):
    def issue(g, carry):
        _row_gather_group(src_hbm, dst, sem, rows_ref, base, g)
        return carry
    lax.fori_loop(0, dst.shape[0], issue, 0)


def _row_gather_wait(dst, sem):
    pltpu.make_async_copy(dst, dst, sem).wait()


def _gathered_rows(buf):
    return buf.reshape(buf.shape[0] * SUBLANES, buf.shape[2])


def _dispatch_kernel(row_tok_ref, blk_ref, base_ref, nv_ref, u_hbm, o_ref, buf, sem):
    c = pl.program_id(0)
    nv = nv_ref[0]
    rb = MOE_ROW_BLOCK
    slot = c % 2

    def start(step, sl):
        _row_gather_start(u_hbm, buf.at[sl], sem.at[sl], row_tok_ref, base_ref[step])

    @pl.when((c == 0) & (nv > 0))
    def _():
        start(0, 0)

    @pl.when(c + 1 < nv)
    def _():
        start(c + 1, 1 - slot)

    @pl.when(c < nv)
    def _():
        _row_gather_wait(buf.at[slot], sem.at[slot])
        o_ref[...] = _unpack_bf16_halves(_gathered_rows(buf[slot]))

    @pl.when(c >= nv)
    def _():
        o_ref[...] = jnp.zeros(o_ref.shape, BF16)


def _dispatch(u, row_tok, blk_ids, blk_base, n_valid, n_rows):
    T, half = u.shape
    D = 2 * half
    rb = MOE_ROW_BLOCK
    return pl.pallas_call(
        _dispatch_kernel,
        grid_spec=pltpu.PrefetchScalarGridSpec(
            num_scalar_prefetch=4,
            grid=(blk_ids.shape[0],),
            in_specs=[pl.BlockSpec(memory_space=pl.ANY)],
            out_specs=pl.BlockSpec((rb, D), lambda c, rt, blk, base, nv: (blk[c], 0)),
            scratch_shapes=[pltpu.VMEM((2, rb // SUBLANES, SUBLANES, half), jnp.uint32),
                            pltpu.SemaphoreType.DMA((2,))],
        ),
        out_shape=jax.ShapeDtypeStruct((n_rows + rb, D), BF16),
        compiler_params=_params("arbitrary"),
        name="moe_dispatch",
    )(row_tok, blk_ids, blk_base, n_valid, u)


def _moe_ffn_kernel(se_ref, nblk_ref, x_ref, wg_ref, wu_ref, wd_ref, y_hbm, acc, wg_sc, wu_sc, wd_sc, sem):
    s = pl.program_id(0)
    f = pl.program_id(1)
    nb = nblk_ref[s]
    rb = MOE_ROW_BLOCK

    @pl.when(nb > 0)
    def _():
        @pl.when(f == 0)
        def _():
            acc[...] = jnp.zeros(acc.shape, F32)

        wg_sc[...] = wg_ref[...].astype(BF16)
        wu_sc[...] = wu_ref[...].astype(BF16)
        wd_sc[...] = wd_ref[...].astype(BF16)

        last_f = f == pl.num_programs(1) - 1

        def out_copy(start, size):
            return pltpu.make_async_copy(
                acc.at[pl.ds(pl.multiple_of(start, rb), size), :],
                y_hbm.at[pl.ds(pl.multiple_of(s * MOE_SLOT_ROWS + start, rb), size), :], sem.at[0])

        def rows_at(start, size):
            rows = pl.ds(pl.multiple_of(start, rb), size)
            x = x_ref[rows, :]
            g = jnp.dot(x, wg_sc[...], preferred_element_type=F32)
            up = jnp.dot(x, wu_sc[...], preferred_element_type=F32)
            hm = (_silu(g) * up).astype(BF16)
            acc[rows, :] += jnp.dot(hm, wd_sc[...], preferred_element_type=F32)

            @pl.when(last_f)
            def _():
                out_copy(start, size).start()

        def quad(r, carry):
            rows_at(r * (4 * rb), 4 * rb)
            return carry

        lax.fori_loop(0, nb // 4, quad, 0)

        @pl.when(nb % 4 >= 2)
        def _():
            rows_at((nb // 4) * (4 * rb), 2 * rb)

        @pl.when(nb % 2 == 1)
        def _():
            rows_at((nb - 1) * rb, rb)

        @pl.when(last_f)
        def _():
            def wait(r, carry):
                out_copy(r * rb, rb).wait()
                return carry

            lax.fori_loop(0, nb, wait, 0)


def _moe_ffn(xs, slot_e, nblk, n_used, w_gate, w_up, w_down, layer):
    D = xs.shape[1]
    F = w_gate.shape[3]
    R = MOE_SLOT_ROWS
    n_slots = slot_e.shape[0]
    tf = 256
    nf = F // tf

    def f_eff(s, f, nb):
        return jnp.where(nb[s] > 0, f, nf - 1)

    return pl.pallas_call(
        _moe_ffn_kernel,
        grid_spec=pltpu.PrefetchScalarGridSpec(
            num_scalar_prefetch=2,
            grid=(n_used, nf),
            in_specs=[
                pl.BlockSpec((R, D), lambda s, f, se, nb: (s, 0), pipeline_mode=pl.Buffered(1)),
                pl.BlockSpec((None, None, D, tf), lambda s, f, se, nb: (layer, se[s], 0, f_eff(s, f, nb))),
                pl.BlockSpec((None, None, D, tf), lambda s, f, se, nb: (layer, se[s], 0, f_eff(s, f, nb))),
                pl.BlockSpec((None, None, tf, D), lambda s, f, se, nb: (layer, se[s], f_eff(s, f, nb), 0)),
            ],
            out_specs=pl.BlockSpec(memory_space=pl.ANY),
            scratch_shapes=[pltpu.VMEM((R, D), F32), pltpu.VMEM((D, tf), BF16), pltpu.VMEM((D, tf), BF16),
                            pltpu.VMEM((tf, D), BF16), pltpu.SemaphoreType.DMA((1,))],
        ),
        out_shape=jax.ShapeDtypeStruct((n_slots * R, D), F32),
        compiler_params=_params("arbitrary", "arbitrary"),
        name="moe_ffn",
    )(slot_e, nblk, xs, w_gate, w_up, w_down)


def _combine_kernel(dest_ref, y_hbm, gate_ref, h_ref, mod_ref, g_ref, b_ref, out_ref, buf0, buf1, sem, *, n_tok):
    i = pl.program_id(0)
    tb = out_ref.shape[0]
    slot = i % 2

    def start(step, sl):
        _row_gather_start(y_hbm, buf0.at[sl], sem.at[0, sl], dest_ref, step * tb)
        _row_gather_start(y_hbm, buf1.at[sl], sem.at[1, sl], dest_ref, n_tok + step * tb)

    @pl.when(i == 0)
    def _():
        start(0, 0)

    @pl.when(i + 1 < pl.num_programs(0))
    def _():
        start(i + 1, 1 - slot)

    _row_gather_wait(buf0.at[slot], sem.at[0, slot])
    _row_gather_wait(buf1.at[slot], sem.at[1, slot])
    y = gate_ref[:, 0:1] * _gathered_rows(buf0[slot]) + gate_ref[:, 1:2] * _gathered_rows(buf1[slot])
    z = DEEPNORM_ALPHA * h_ref[...] + (1.0 + mod_ref[0, 5:6, :]) * y
    out_ref[...] = _layer_norm(z, g_ref[...], b_ref[...])


def _combine(y, dest, gates_tk, h, mod_l, ln_g, ln_b, S):
    T, D = h.shape
    tb = 256
    return pl.pallas_call(
        functools.partial(_combine_kernel, n_tok=T),
        grid_spec=pltpu.PrefetchScalarGridSpec(
            num_scalar_prefetch=1,
            grid=(T // tb,),
            in_specs=[
                pl.BlockSpec(memory_space=pl.ANY),
                pl.BlockSpec((tb, TOP_K), lambda i, d: (i, 0)),
                pl.BlockSpec((tb, D), lambda i, d: (i, 0)),
                pl.BlockSpec((1, 8, D), lambda i, d: (i * tb // S, 0, 0)),
                pl.BlockSpec((1, D), lambda i, d: (0, 0)),
                pl.BlockSpec((1, D), lambda i, d: (0, 0)),
            ],
            out_specs=pl.BlockSpec((tb, D), lambda i, d: (i, 0)),
            scratch_shapes=[pltpu.VMEM((2, tb // SUBLANES, SUBLANES, D), F32),
                            pltpu.VMEM((2, tb // SUBLANES, SUBLANES, D), F32),
                            pltpu.SemaphoreType.DMA((2, 2))],
        ),
        out_shape=jax.ShapeDtypeStruct((T, D), F32),
        compiler_params=_params("arbitrary"),
        name="moe_combine",
    )(dest, y, gates_tk, h, mod_l, ln_g.reshape(1, D), ln_b.reshape(1, D))


def _moe_layer(h, mod_l, router_w, router_b, w_gate, w_up, w_down, layer, ln_g, ln_b, S):
    T, D = h.shape
    E, R, rb = N_EXPERTS, MOE_SLOT_ROWS, MOE_ROW_BLOCK
    n_slots = (T * TOP_K + E * (R - 1)) // R
    n_rows = n_slots * R
    per_slot = R // rb
    max_blocks = T * TOP_K // rb + E
    u, idx, gates, rank, cnt = _router(h, mod_l, router_w, router_b, S)
    cnt = cnt[:, 0]
    slots_e = (cnt + R - 1) // R
    slot_end = jnp.cumsum(slots_e)
    slot_start = slot_end - slots_e
    row_start = slot_start * R
    dest = rank
    for e in range(E):
        dest = dest + jnp.where(idx == e, row_start[e], 0)
    tok = jnp.tile(jnp.arange(T, dtype=jnp.int32), TOP_K)
    _, sorted_tok = lax.sort((dest.reshape(-1), tok), num_keys=1)
    sorted_tok = jnp.concatenate([sorted_tok, jnp.zeros((rb,), jnp.int32)])
    cstart = jnp.cumsum(cnt) - cnt
    sid = jnp.arange(n_slots, dtype=jnp.int32)
    n_used = slot_end[-1]
    last_used = jnp.maximum(n_used - 1, 0)
    se = jnp.minimum(jnp.searchsorted(slot_end, jnp.minimum(sid, last_used), side='right'), E - 1).astype(jnp.int32)
    rows_in = jnp.clip(cnt[se] - (sid - slot_start[se]) * R, 0, R)
    rows_in = jnp.where(sid < n_used, rows_in, 0)
    nblk = ((rows_in + rb - 1) // rb).astype(jnp.int32)
    blk = jnp.arange(n_rows // rb, dtype=jnp.int32)
    blk_valid = (blk % per_slot) < nblk[blk // per_slot]
    order = jnp.argsort(jnp.logical_not(blk_valid), stable=True).astype(jnp.int32)
    n_valid = jnp.sum(blk_valid.astype(jnp.int32))
    blk_ids = jnp.where(jnp.arange(max_blocks) < n_valid, order[:max_blocks], n_rows // rb).astype(jnp.int32)
    b_slot = jnp.minimum(blk_ids // per_slot, n_slots - 1)
    b_e = se[b_slot]
    blk_base = cstart[b_e] + (b_slot - slot_start[b_e]) * R + (blk_ids % per_slot) * rb
    blk_base = jnp.clip(blk_base, 0, T * TOP_K).astype(jnp.int32)
    xs = _dispatch(u, sorted_tok, blk_ids, blk_base, n_valid.reshape(1), n_rows)
    y = _moe_ffn(xs, se, nblk, n_used.astype(jnp.int32), w_gate, w_up, w_down, layer)
    return _combine(y, dest.reshape(-1).astype(jnp.int32), gates.T, h, mod_l, ln_g, ln_b, S)


def kernel(x, c, positions, ada_w, ada_b, ada_table, ln_g, ln_b, mla_w_dq, mla_q_norm, mla_w_uq, mla_w_dkv, mla_kv_norm, mla_w_ukv, mla_w_o, ca_w_qkv, ca_b_qkv, ca_rel_bias, ca_w_o, ffn_w_gate, ffn_w_up, ffn_w_down, moe_router_w, moe_router_b, moe_w_gate, moe_w_up, moe_w_down):
    B, S, D = x.shape
    T = B * S
    mod = _cond(c, ada_w, ada_b, ada_table)
    half = QK_ROPE // 2
    inv_freq = ROPE_THETA ** (-jnp.arange(half, dtype=F32) / half)
    freq_row = jnp.concatenate([inv_freq, inv_freq, jnp.zeros((LANES - QK_ROPE,), F32)]).reshape(1, LANES)
    pos_col = positions.reshape(T, 1).astype(jnp.int32)
    h = x.reshape(T, D)
    for i in range(DEPTH):
        j = i // 2
        mod_l = mod[i]
        if i % 2 == 0:
            q, k, v = _mla_proj(h, mod_l, pos_col, freq_row, mla_w_dq[j], mla_q_norm[j], mla_w_dkv[j],
                                mla_kv_norm[j], mla_w_uq[j], mla_w_ukv[j], S)
            o = _mla_attn(q, k, v, B, S)
            h = _proj_ln(o, mla_w_o, j, h, mod_l, ln_g[i, 0], ln_b[i, 0], S)
            h = _ffn(h, mod_l, ffn_w_gate, ffn_w_up, ffn_w_down, j, ln_g[i, 1], ln_b[i, 1], S)
        else:
            qkv = _qkv(h, mod_l, ca_w_qkv, j, ca_b_qkv[j], S)
            o = _ca_attn(qkv, ca_rel_bias[j], B, S)
            h = _proj_ln(o, ca_w_o, j, h, mod_l, ln_g[i, 0], ln_b[i, 0], S)
            h = _moe_layer(h, mod_l, moe_router_w[j], moe_router_b[j], moe_w_gate, moe_w_up,
                           moe_w_down, j, ln_g[i, 1], ln_b[i, 1], S)
    return h.reshape(B, S, D)
```

```python
import functools

import numpy as np
import jax
import jax.numpy as jnp
from jax import lax
from jax.experimental import pallas as pl
from jax.experimental.pallas import tpu as pltpu

F32 = jnp.float32
BF16 = jnp.bfloat16

CHUNK = 64
MLA_HEADS = 16
QK_NOPE = 128
QK_ROPE = 64
V_HEAD = 128
KV_LORA = 512
ROPE_THETA = 10000.0
CA_HEADS = 16
CA_HEAD_DIM = 128
LEFT_CHUNKS = 8
MAX_REL = 256
N_EXPERTS = 8
TOP_K = 2
N_MOD = 6
LN_EPS = 1e-5
RMS_EPS = 1e-6
DEPTH = 4
DEEPNORM_ALPHA = (2.0 * DEPTH) ** 0.25
LOG2E = 1.4426950408889634

LANES = 128
VMEM_LIMIT = 56 * 1024 * 1024

MLA_HEAD_PAD = 2 * LANES
MOE_SLOT_ROWS = 2560
MOE_ROW_BLOCK = 256
CA_QBLOCK = 256
CA_KBLOCKS = 3
ATTN_HEADS_PER_STEP = 16

_NT = (((1,), (1,)), ((), ()))


def _params(*sem):
    return pltpu.CompilerParams(dimension_semantics=sem, vmem_limit_bytes=VMEM_LIMIT)


def _silu(x):
    return x * jax.nn.sigmoid(x)


def _layer_norm(z, g, b):
    mu = jnp.mean(z, axis=-1, keepdims=True)
    zc = z - mu
    var = jnp.mean(zc * zc, axis=-1, keepdims=True)
    return zc * lax.rsqrt(var + LN_EPS) * g + b


def _modulate(h, mod_ref, k):
    return h * (1.0 + mod_ref[0, k + 1:k + 2, :]) + mod_ref[0, k:k + 1, :]


def _cond_kernel(c_ref, w_ref, b_ref, tab_ref, o_ref):
    s = _silu(c_ref[...]).astype(BF16)
    y = jnp.dot(s, w_ref[...].astype(BF16), preferred_element_type=F32) + b_ref[...]
    o_ref[...] = y[None, :, :] + tab_ref[...][:, None, :]


def _cond(c, ada_w, ada_b, ada_table):
    B, D = c.shape
    N = ada_w.shape[1]
    tn = 1024
    c8 = jnp.zeros((8, D), F32).at[:B].set(c)
    out = pl.pallas_call(
        _cond_kernel,
        grid=(N // tn,),
        in_specs=[
            pl.BlockSpec((8, D), lambda n: (0, 0)),
            pl.BlockSpec((D, tn), lambda n: (0, n)),
            pl.BlockSpec((1, tn), lambda n: (0, n)),
            pl.BlockSpec((DEPTH, tn), lambda n: (0, n)),
        ],
        out_specs=pl.BlockSpec((DEPTH, 8, tn), lambda n: (0, 0, n)),
        out_shape=jax.ShapeDtypeStruct((DEPTH, 8, N), F32),
        compiler_params=_params("arbitrary"),
        name="cond",
    )(c8, ada_w, ada_b.reshape(1, N), ada_table.reshape(DEPTH, N))
    mod = out[:, :B].reshape(DEPTH, B, N_MOD, D)
    return jnp.pad(mod, ((0, 0), (0, 0), (0, 8 - N_MOD), (0, 0)))


def _rope_lanes(x, cos, sin_signed):
    lane = lax.broadcasted_iota(jnp.int32, x.shape, 1)
    partner = jnp.where(lane < QK_ROPE // 2, pltpu.roll(x, LANES - QK_ROPE // 2, 1),
                        pltpu.roll(x, QK_ROPE // 2, 1))
    return x * cos + partner * sin_signed


def _mla_proj_kernel(h_ref, mod_ref, pos_ref, freq_ref, wdq_ref, qn_ref, wdkv_ref, kvn_ref, wkr_ref,
                     wuq_ref, wukv_ref, q_ref, k_ref, v_ref):
    u = _modulate(h_ref[...], mod_ref, 0).astype(BF16)
    ang = pos_ref[...].astype(F32) * freq_ref[...]
    cos = jnp.cos(ang)
    sin = jnp.sin(ang)
    lane = lax.broadcasted_iota(jnp.int32, ang.shape, 1)
    sin_signed = jnp.where(lane < QK_ROPE // 2, -sin, sin)

    def rms(x, g):
        return (x * lax.rsqrt(jnp.mean(x * x, axis=-1, keepdims=True) + RMS_EPS)) * g

    cq = rms(jnp.dot(u, wdq_ref[...], preferred_element_type=F32), qn_ref[...]).astype(BF16)
    ckv = rms(jnp.dot(u, wdkv_ref[...], preferred_element_type=F32), kvn_ref[...]).astype(BF16)
    kr = jnp.dot(u, wkr_ref[...], preferred_element_type=F32)
    kr = _rope_lanes(kr, cos, sin_signed).astype(BF16)
    for hd in range(MLA_HEADS):
        c0 = hd * MLA_HEAD_PAD
        qh = jnp.dot(cq, wuq_ref[:, c0:c0 + MLA_HEAD_PAD], preferred_element_type=F32)
        q_ref[:, c0:c0 + LANES] = qh[:, :LANES].astype(BF16)
        q_ref[:, c0 + LANES:c0 + MLA_HEAD_PAD] = _rope_lanes(qh[:, LANES:], cos, sin_signed).astype(BF16)
        kvh = jnp.dot(ckv, wukv_ref[:, c0:c0 + MLA_HEAD_PAD], preferred_element_type=F32)
        k_ref[:, c0:c0 + LANES] = kvh[:, :LANES].astype(BF16)
        k_ref[:, c0 + LANES:c0 + MLA_HEAD_PAD] = kr
        v_ref[:, hd * V_HEAD:(hd + 1) * V_HEAD] = kvh[:, LANES:].astype(BF16)


def _mla_proj(h, mod_l, pos_col, freq_row, w_dq, q_norm, w_dkv, kv_norm, w_uq, w_ukv, S):
    T, D = h.shape
    H = MLA_HEADS
    tm = 256
    q_lora = w_dq.shape[1]
    wdq = w_dq.astype(BF16)
    wdkv = w_dkv[:, :KV_LORA].astype(BF16)
    wkr = jnp.pad(w_dkv[:, KV_LORA:], ((0, 0), (0, LANES - QK_ROPE))).astype(BF16)
    wuq = jnp.pad(w_uq.reshape(q_lora, H, QK_NOPE + QK_ROPE),
                  ((0, 0), (0, 0), (0, MLA_HEAD_PAD - QK_NOPE - QK_ROPE))).reshape(q_lora, H * MLA_HEAD_PAD)
    wuq = wuq.astype(BF16)
    wukv = w_ukv.astype(BF16)
    const = lambda i: (0, 0)
    row = lambda i: (i, 0)
    return pl.pallas_call(
        _mla_proj_kernel,
        grid=(T // tm,),
        in_specs=[
            pl.BlockSpec((tm, D), row),
            pl.BlockSpec((1, 8, D), lambda i: (i * tm // S, 0, 0)),
            pl.BlockSpec((tm, 1), row),
            pl.BlockSpec((1, LANES), const),
            pl.BlockSpec(wdq.shape, const),
            pl.BlockSpec((1, q_lora), const),
            pl.BlockSpec(wdkv.shape, const),
            pl.BlockSpec((1, KV_LORA), const),
            pl.BlockSpec(wkr.shape, const),
            pl.BlockSpec(wuq.shape, const),
            pl.BlockSpec(wukv.shape, const),
        ],
        out_specs=[
            pl.BlockSpec((tm, H * MLA_HEAD_PAD), row),
            pl.BlockSpec((tm, H * MLA_HEAD_PAD), row),
            pl.BlockSpec((tm, H * V_HEAD), row),
        ],
        out_shape=[
            jax.ShapeDtypeStruct((T, H * MLA_HEAD_PAD), BF16),
            jax.ShapeDtypeStruct((T, H * MLA_HEAD_PAD), BF16),
            jax.ShapeDtypeStruct((T, H * V_HEAD), BF16),
        ],
        compiler_params=_params("arbitrary"),
        name="mla_proj",
    )(h, mod_l, pos_col, freq_row, wdq, q_norm.reshape(1, -1), wdkv, kv_norm.reshape(1, -1), wkr, wuq, wukv)


def _with_ones(v):
    return jnp.concatenate([v, jnp.ones_like(v)], axis=1)


def _mla_attn_kernel(pi_ref, pj_ref, q_ref, k_ref, v_ref, o_ref, m_sc, acc_sc, *, scale):
    p = pl.program_id(2)
    i = pi_ref[p]
    j = pj_ref[p]

    @pl.when(j == 0)
    def _():
        m_sc[...] = jnp.full(m_sc.shape, -jnp.inf, F32)
        acc_sc[...] = jnp.zeros(acc_sc.shape, F32)

    def accumulate(diagonal):
        for hd in range(ATTN_HEADS_PER_STEP):
            qk = slice(hd * MLA_HEAD_PAD, (hd + 1) * MLA_HEAD_PAD)
            s = lax.dot_general(q_ref[:, qk], k_ref[:, qk], _NT, preferred_element_type=F32) * scale
            if diagonal:
                qc = lax.broadcasted_iota(jnp.int32, s.shape, 0) // CHUNK
                kc = lax.broadcasted_iota(jnp.int32, s.shape, 1) // CHUNK
                s = jnp.where(kc <= qc, s, -jnp.inf)
            m_prev = m_sc[hd]
            m_new = jnp.maximum(m_prev, jnp.max(s, axis=-1, keepdims=True))
            a = jnp.exp2(m_prev - m_new)
            e = jnp.exp2(s - m_new).astype(BF16)
            v = _with_ones(v_ref[:, hd * V_HEAD:(hd + 1) * V_HEAD])
            acc_sc[hd] = a * acc_sc[hd] + jnp.dot(e, v, preferred_element_type=F32)
            m_sc[hd] = m_new

    @pl.when(j < i)
    def _():
        accumulate(False)

    @pl.when(j == i)
    def _():
        accumulate(True)
        for hd in range(ATTN_HEADS_PER_STEP):
            acc = acc_sc[hd]
            o_ref[:, hd * V_HEAD:(hd + 1) * V_HEAD] = (acc[:, :V_HEAD] / acc[:, V_HEAD:]).astype(o_ref.dtype)


def _mla_attn(q, k, v, B, S):
    H = MLA_HEADS
    hp = ATTN_HEADS_PER_STEP
    tq = 512
    nq = S // tq
    scale = (QK_NOPE + QK_ROPE) ** -0.5 * LOG2E
    pairs = [(i, j) for i in range(nq) for j in range(i + 1)]
    pi = jnp.asarray([p[0] for p in pairs], jnp.int32)
    pj = jnp.asarray([p[1] for p in pairs], jnp.int32)
    return pl.pallas_call(
        functools.partial(_mla_attn_kernel, scale=scale),
        grid_spec=pltpu.PrefetchScalarGridSpec(
            num_scalar_prefetch=2,
            grid=(B, H // hp, len(pairs)),
            in_specs=[
                pl.BlockSpec((tq, hp * MLA_HEAD_PAD), lambda b, g, p, pi, pj: (b * nq + pi[p], g)),
                pl.BlockSpec((tq, hp * MLA_HEAD_PAD), lambda b, g, p, pi, pj: (b * nq + pj[p], g)),
                pl.BlockSpec((tq, hp * V_HEAD), lambda b, g, p, pi, pj: (b * nq + pj[p], g)),
            ],
            out_specs=pl.BlockSpec((tq, hp * V_HEAD), lambda b, g, p, pi, pj: (b * nq + pi[p], g)),
            scratch_shapes=[pltpu.VMEM((hp, tq, 1), F32), pltpu.VMEM((hp, tq, 2 * V_HEAD), F32)],
        ),
        out_shape=jax.ShapeDtypeStruct((B * S, H * V_HEAD), BF16),
        compiler_params=_params("arbitrary", "arbitrary", "arbitrary"),
        name="mla_attn",
    )(pi, pj, q, k, v)


def _proj_ln_kernel(o_ref, w_ref, h_ref, mod_ref, g_ref, b_ref, out_ref, w_sc, *, gate_row):
    @pl.when(pl.program_id(0) == 0)
    def _():
        w_sc[...] = w_ref[...].astype(BF16)

    half = out_ref.shape[0] // 2
    for part in range(2):
        rows = slice(part * half, (part + 1) * half)
        y = jnp.dot(o_ref[rows, :], w_sc[...], preferred_element_type=F32)
        z = DEEPNORM_ALPHA * h_ref[rows, :] + (1.0 + mod_ref[0, gate_row:gate_row + 1, :]) * y
        out_ref[rows, :] = _layer_norm(z, g_ref[...], b_ref[...])


def _proj_ln(o, w_o, layer, h, mod_l, ln_g, ln_b, S):
    T, D = h.shape
    tm = 512
    K = w_o.shape[1]
    row = lambda i: (i, 0)
    const = lambda i: (0, 0)
    return pl.pallas_call(
        functools.partial(_proj_ln_kernel, gate_row=2),
        grid=(T // tm,),
        in_specs=[
            pl.BlockSpec((tm, K), row),
            pl.BlockSpec((None, K, D), lambda i: (layer, 0, 0), pipeline_mode=pl.Buffered(1)),
            pl.BlockSpec((tm, D), row),
            pl.BlockSpec((1, 8, D), lambda i: (i * tm // S, 0, 0)),
            pl.BlockSpec((1, D), const),
            pl.BlockSpec((1, D), const),
        ],
        out_specs=pl.BlockSpec((tm, D), row),
        out_shape=jax.ShapeDtypeStruct((T, D), F32),
        scratch_shapes=[pltpu.VMEM((K, D), BF16)],
        compiler_params=_params("arbitrary"),
        name="proj_ln",
    )(o, w_o, h, mod_l, ln_g.reshape(1, D), ln_b.reshape(1, D))


def _ffn_kernel(h_ref, mod_ref, wg_ref, wu_ref, wd_ref, g_ref, b_ref, out_ref, u_sc):
    f = pl.program_id(1)

    @pl.when(f == 0)
    def _():
        u_sc[...] = _modulate(h_ref[...], mod_ref, 3).astype(BF16)
        out_ref[...] = jnp.zeros(out_ref.shape, F32)

    u = u_sc[...]
    g = jnp.dot(u, wg_ref[...].astype(BF16), preferred_element_type=F32)
    up = jnp.dot(u, wu_ref[...].astype(BF16), preferred_element_type=F32)
    hm = (_silu(g) * up).astype(BF16)
    out_ref[...] += jnp.dot(hm, wd_ref[...].astype(BF16), preferred_element_type=F32)

    @pl.when(f == pl.num_programs(1) - 1)
    def _():
        z = DEEPNORM_ALPHA * h_ref[...] + (1.0 + mod_ref[0, 5:6, :]) * out_ref[...]
        out_ref[...] = _layer_norm(z, g_ref[...], b_ref[...])


def _ffn(h, mod_l, w_gate, w_up, w_down, layer, ln_g, ln_b, S):
    T, D = h.shape
    F = w_gate.shape[2]
    tm = min(1024, T)
    tf = 256
    row = lambda i, f: (i, 0)
    const = lambda i, f: (0, 0)
    return pl.pallas_call(
        _ffn_kernel,
        grid=(T // tm, F // tf),
        in_specs=[
            pl.BlockSpec((tm, D), row, pipeline_mode=pl.Buffered(1)),
            pl.BlockSpec((1, 8, D), lambda i, f: (i * tm // S, 0, 0)),
            pl.BlockSpec((None, D, tf), lambda i, f: (layer, 0, f)),
            pl.BlockSpec((None, D, tf), lambda i, f: (layer, 0, f)),
            pl.BlockSpec((None, tf, D), lambda i, f: (layer, f, 0)),
            pl.BlockSpec((1, D), const),
            pl.BlockSpec((1, D), const),
        ],
        out_specs=pl.BlockSpec((tm, D), row),
        out_shape=jax.ShapeDtypeStruct((T, D), F32),
        scratch_shapes=[pltpu.VMEM((tm, D), BF16)],
        compiler_params=_params("arbitrary", "arbitrary"),
        name="ffn",
    )(h, mod_l, w_gate, w_up, w_down, ln_g.reshape(1, D), ln_b.reshape(1, D))


def _qkv_kernel(h_ref, mod_ref, w_ref, b_ref, o_ref, u_sc):
    @pl.when(pl.program_id(1) == 0)
    def _():
        u_sc[...] = _modulate(h_ref[...], mod_ref, 0).astype(BF16)

    y = jnp.dot(u_sc[...], w_ref[...], preferred_element_type=F32) + b_ref[...]
    o_ref[...] = y.astype(o_ref.dtype)


def _qkv(h, mod_l, w_qkv, layer, b_qkv, S):
    T, D = h.shape
    N = w_qkv.shape[2]
    tm = min(1024, T)
    tn = 2048
    w = w_qkv[layer].astype(BF16)
    return pl.pallas_call(
        _qkv_kernel,
        grid=(T // tm, N // tn),
        in_specs=[
            pl.BlockSpec((tm, D), lambda i, n: (i, 0), pipeline_mode=pl.Buffered(1)),
            pl.BlockSpec((1, 8, D), lambda i, n: (i * tm // S, 0, 0)),
            pl.BlockSpec((D, tn), lambda i, n: (0, n)),
            pl.BlockSpec((1, tn), lambda i, n: (0, n)),
        ],
        out_specs=pl.BlockSpec((tm, tn), lambda i, n: (i, n)),
        out_shape=jax.ShapeDtypeStruct((T, N), BF16),
        scratch_shapes=[pltpu.VMEM((tm, D), BF16)],
        compiler_params=_params("arbitrary", "arbitrary"),
        name="qkv",
    )(h, mod_l, w, b_qkv.reshape(1, N))


def _ca_attn_kernel(q_ref, k0_ref, k1_ref, k2_ref, v0_ref, v1_ref, v2_ref, bias_ref, o_ref, *, scale):
    i = pl.program_id(2)
    tq = q_ref.shape[0]
    k_refs = (k0_ref, k1_ref, k2_ref)
    v_refs = (v0_ref, v1_ref, v2_ref)

    def attend(stream_start):
        for hd in range(ATTN_HEADS_PER_STEP):
            cols = slice(hd * CA_HEAD_DIM, (hd + 1) * CA_HEAD_DIM)
            q = q_ref[:, cols]
            s = []
            for p in range(CA_KBLOCKS):
                sp = lax.dot_general(q, k_refs[p][:, cols], _NT, preferred_element_type=F32) * scale
                sp = sp + bias_ref[hd, :, p * tq:(p + 1) * tq]
                first = CA_KBLOCKS - 1 - p
                if stream_start and first > 0:
                    sp = jnp.where(i >= first, sp, -jnp.inf)
                s.append(sp)
            m = functools.reduce(jnp.maximum, [jnp.max(sp, axis=-1, keepdims=True) for sp in s])
            acc = functools.reduce(jnp.add, [
                jnp.dot(jnp.exp2(sp - m).astype(BF16), _with_ones(v_refs[p][:, cols]), preferred_element_type=F32)
                for p, sp in enumerate(s)])
            o_ref[:, cols] = (acc[:, :CA_HEAD_DIM] / acc[:, CA_HEAD_DIM:]).astype(o_ref.dtype)

    @pl.when(i < CA_KBLOCKS - 1)
    def _():
        attend(True)

    @pl.when(i >= CA_KBLOCKS - 1)
    def _():
        attend(False)


def _ca_bias_kernel(w_ref, o_ref):
    tq, nk = o_ref.shape[1], o_ref.shape[2]
    W = w_ref.shape[2]
    x = jnp.broadcast_to(w_ref[0], (tq, W))
    t = pltpu.roll(x, W - (tq - 1), 1, stride=1, stride_axis=0)[:, :nk]
    qc = lax.broadcasted_iota(jnp.int32, (tq, nk), 0) // CHUNK + (nk - tq) // CHUNK
    kc = lax.broadcasted_iota(jnp.int32, (tq, nk), 1) // CHUNK
    o_ref[0] = jnp.where((kc <= qc) & (kc >= qc - LEFT_CHUNKS), t * LOG2E, -jnp.inf)


def _ca_bias_table(rel_bias):
    H = rel_bias.shape[0]
    tq = CA_QBLOCK
    nk = CA_KBLOCKS * tq
    W = tq + nk
    d = np.arange(W) - (tq - 1)
    rel_idx = np.clip((nk - tq) - d, -MAX_REL, MAX_REL) + MAX_REL
    w = rel_bias[:, rel_idx].astype(F32).reshape(H, 1, W)
    return pl.pallas_call(
        _ca_bias_kernel,
        grid=(H,),
        in_specs=[pl.BlockSpec((1, 1, W), lambda h: (h, 0, 0))],
        out_specs=pl.BlockSpec((1, tq, nk), lambda h: (h, 0, 0)),
        out_shape=jax.ShapeDtypeStruct((H, tq, nk), F32),
        compiler_params=_params("arbitrary"),
        name="ca_bias",
    )(w)


def _ca_attn(qkv, rel_bias, B, S):
    H, Dh = CA_HEADS, CA_HEAD_DIM
    hp = ATTN_HEADS_PER_STEP
    ng = H // hp
    tq = CA_QBLOCK
    nq = S // tq
    bias = _ca_bias_table(rel_bias)

    def kv_spec(which, p):
        back = CA_KBLOCKS - 1 - p
        return pl.BlockSpec((tq, hp * Dh), lambda b, g, i: (b * nq + jnp.maximum(i - back, 0), which * ng + g))

    return pl.pallas_call(
        functools.partial(_ca_attn_kernel, scale=Dh ** -0.5 * LOG2E),
        grid=(B, ng, nq),
        in_specs=[pl.BlockSpec((tq, hp * Dh), lambda b, g, i: (b * nq + i, g))]
        + [kv_spec(1, p) for p in range(CA_KBLOCKS)]
        + [kv_spec(2, p) for p in range(CA_KBLOCKS)]
        + [pl.BlockSpec((hp, tq, CA_KBLOCKS * tq), lambda b, g, i: (g, 0, 0))],
        out_specs=pl.BlockSpec((tq, hp * Dh), lambda b, g, i: (b * nq + i, g)),
        out_shape=jax.ShapeDtypeStruct((B * S, H * Dh), BF16),
        compiler_params=_params("arbitrary", "arbitrary", "arbitrary"),
        name="ca_attn",
    )(qkv, qkv, qkv, qkv, qkv, qkv, qkv, bias)


def _pack_bf16_halves(x):
    half = x.shape[1] // 2
    lo = pltpu.bitcast(x[:, :half].astype(BF16).astype(F32), jnp.uint32)
    hi = pltpu.bitcast(x[:, half:].astype(BF16).astype(F32), jnp.uint32)
    return lax.shift_right_logical(lo, jnp.uint32(16)) | (hi & jnp.uint32(0xFFFF0000))


def _unpack_bf16_halves(p):
    lo = pltpu.bitcast(lax.shift_left(p, jnp.uint32(16)), F32).astype(BF16)
    hi = pltpu.bitcast(p & jnp.uint32(0xFFFF0000), F32).astype(BF16)
    return jnp.concatenate([lo, hi], axis=1)


def _router_kernel(h_ref, mod_ref, rw_ref, rb_ref, u_ref, idx_ref, gate_ref, rank_ref, cnt_ref, carry_sc):
    t = pl.program_id(0)

    @pl.when(t == 0)
    def _():
        carry_sc[...] = jnp.zeros(carry_sc.shape, F32)

    u = _modulate(h_ref[...], mod_ref, 3)
    u_ref[...] = _pack_bf16_halves(u)
    tb = u.shape[0]
    logits = lax.dot_general(rw_ref[...], u, _NT, precision=lax.Precision.HIGHEST,
                             preferred_element_type=F32) + rb_ref[...]
    eid = lax.broadcasted_iota(jnp.int32, logits.shape, 0)
    m1 = jnp.max(logits, axis=0, keepdims=True)
    i1 = jnp.min(jnp.where(logits == m1, eid, N_EXPERTS), axis=0, keepdims=True)
    rest = jnp.where(eid == i1, -jnp.inf, logits)
    m2 = jnp.max(rest, axis=0, keepdims=True)
    i2 = jnp.min(jnp.where(rest == m2, eid, N_EXPERTS), axis=0, keepdims=True)
    e2 = jnp.exp(m2 - m1)
    den = 1.0 + e2
    idx_ref[...] = jnp.concatenate([i1, i2], axis=0)
    gate_ref[...] = jnp.concatenate([1.0 / den, e2 / den], axis=0)
    onehot = jnp.where((eid == i1) | (eid == i2), 1.0, 0.0)
    tri = jnp.where(lax.broadcasted_iota(jnp.int32, (tb, tb), 0) <= lax.broadcasted_iota(jnp.int32, (tb, tb), 1),
                    1.0, 0.0).astype(BF16)
    incl = jnp.dot(onehot.astype(BF16), tri, preferred_element_type=F32)
    excl = incl - onehot + carry_sc[:, 0:1]
    r1 = jnp.sum(jnp.where(eid == i1, excl, 0.0), axis=0, keepdims=True)
    r2 = jnp.sum(jnp.where(eid == i2, excl, 0.0), axis=0, keepdims=True)
    rank_ref[...] = jnp.concatenate([r1, r2], axis=0).astype(jnp.int32)
    carry_sc[...] = carry_sc[...] + jnp.sum(onehot, axis=1, keepdims=True)
    cnt_ref[...] = carry_sc[...].astype(jnp.int32)


def _router(h, mod_l, router_w, router_b, S):
    T, D = h.shape
    E = N_EXPERTS
    tb = 512
    return pl.pallas_call(
        _router_kernel,
        grid=(T // tb,),
        in_specs=[
            pl.BlockSpec((tb, D), lambda t: (t, 0)),
            pl.BlockSpec((1, 8, D), lambda t: (t * tb // S, 0, 0)),
            pl.BlockSpec((E, D), lambda t: (0, 0)),
            pl.BlockSpec((E, 1), lambda t: (0, 0)),
        ],
        out_specs=[
            pl.BlockSpec((tb, D // 2), lambda t: (t, 0)),
            pl.BlockSpec((TOP_K, tb), lambda t: (0, t)),
            pl.BlockSpec((TOP_K, tb), lambda t: (0, t)),
            pl.BlockSpec((TOP_K, tb), lambda t: (0, t)),
            pl.BlockSpec((E, LANES), lambda t: (0, 0)),
        ],
        out_shape=[
            jax.ShapeDtypeStruct((T, D // 2), jnp.uint32),
            jax.ShapeDtypeStruct((TOP_K, T), jnp.int32),
            jax.ShapeDtypeStruct((TOP_K, T), F32),
            jax.ShapeDtypeStruct((TOP_K, T), jnp.int32),
            jax.ShapeDtypeStruct((E, LANES), jnp.int32),
        ],
        scratch_shapes=[pltpu.VMEM((E, LANES), F32)],
        compiler_params=_params("arbitrary"),
        name="router",
    )(h, mod_l, router_w.T, router_b.reshape(E, 1))


SUBLANES = 8


def _row_gather_group(src_hbm, dst, sem, rows_ref, base, g):
    for k in range(SUBLANES):
        r = rows_ref[base + g * SUBLANES + k]
        pltpu.make_async_copy(src_hbm.at[pl.ds(r, 1), :], dst.at[g, pl.ds(k, 1), :], sem).start(priority=k % 2)


def _row_gather_start(src_hbm, dst, sem, rows_ref, base):
    def issue(g, carry):
        _row_gather_group(src_hbm, dst, sem, rows_ref, base, g)
        return carry
    lax.fori_loop(0, dst.shape[0], issue, 0)


def _row_gather_wait(dst, sem):
    pltpu.make_async_copy(dst, dst, sem).wait()


def _gathered_rows(buf):
    return buf.reshape(buf.shape[0] * SUBLANES, buf.shape[2])


def _dispatch_kernel(row_tok_ref, blk_ref, base_ref, nv_ref, u_hbm, o_ref, buf, sem):
    c = pl.program_id(0)
    nv = nv_ref[0]
    rb = MOE_ROW_BLOCK
    slot = c % 2

    def start(step, sl):
        _row_gather_start(u_hbm, buf.at[sl], sem.at[sl], row_tok_ref, base_ref[step])

    @pl.when((c == 0) & (nv > 0))
    def _():
        start(0, 0)

    @pl.when(c + 1 < nv)
    def _():
        start(c + 1, 1 - slot)

    @pl.when(c < nv)
    def _():
        _row_gather_wait(buf.at[slot], sem.at[slot])
        o_ref[...] = _unpack_bf16_halves(_gathered_rows(buf[slot]))

    @pl.when(c >= nv)
    def _():
        o_ref[...] = jnp.zeros(o_ref.shape, BF16)


def _dispatch(u, row_tok, blk_ids, blk_base, n_valid, n_rows):
    T, half = u.shape
    D = 2 * half
    rb = MOE_ROW_BLOCK
    return pl.pallas_call(
        _dispatch_kernel,
        grid_spec=pltpu.PrefetchScalarGridSpec(
            num_scalar_prefetch=4,
            grid=(blk_ids.shape[0],),
            in_specs=[pl.BlockSpec(memory_space=pl.ANY)],
            out_specs=pl.BlockSpec((rb, D), lambda c, rt, blk, base, nv: (blk[c], 0)),
            scratch_shapes=[pltpu.VMEM((2, rb // SUBLANES, SUBLANES, half), jnp.uint32),
                            pltpu.SemaphoreType.DMA((2,))],
        ),
        out_shape=jax.ShapeDtypeStruct((n_rows + rb, D), BF16),
        compiler_params=_params("arbitrary"),
        name="moe_dispatch",
    )(row_tok, blk_ids, blk_base, n_valid, u)


def _moe_ffn_kernel(se_ref, nblk_ref, x_ref, wg_ref, wu_ref, wd_ref, y_hbm, acc, wg_sc, wu_sc, wd_sc, sem):
    s = pl.program_id(0)
    f = pl.program_id(1)
    nb = nblk_ref[s]
    rb = MOE_ROW_BLOCK

    @pl.when(nb > 0)
    def _():
        @pl.when(f == 0)
        def _():
            acc[...] = jnp.zeros(acc.shape, F32)

        wg_sc[...] = wg_ref[...].astype(BF16)
        wu_sc[...] = wu_ref[...].astype(BF16)
        wd_sc[...] = wd_ref[...].astype(BF16)

        last_f = f == pl.num_programs(1) - 1

        def out_copy(start, size):
            return pltpu.make_async_copy(
                acc.at[pl.ds(pl.multiple_of(start, rb), size), :],
                y_hbm.at[pl.ds(pl.multiple_of(s * MOE_SLOT_ROWS + start, rb), size), :], sem.at[0])

        def rows_at(start, size):
            rows = pl.ds(pl.multiple_of(start, rb), size)
            x = x_ref[rows, :]
            g = jnp.dot(x, wg_sc[...], preferred_element_type=F32)
            up = jnp.dot(x, wu_sc[...], preferred_element_type=F32)
            hm = (_silu(g) * up).astype(BF16)
            acc[rows, :] += jnp.dot(hm, wd_sc[...], preferred_element_type=F32)

            @pl.when(last_f)
            def _():
                out_copy(start, size).start()

        def quad(r, carry):
            rows_at(r * (4 * rb), 4 * rb)
            return carry

        lax.fori_loop(0, nb // 4, quad, 0)

        @pl.when(nb % 4 >= 2)
        def _():
            rows_at((nb // 4) * (4 * rb), 2 * rb)

        @pl.when(nb % 2 == 1)
        def _():
            rows_at((nb - 1) * rb, rb)

        @pl.when(last_f)
        def _():
            def wait(r, carry):
                out_copy(r * rb, rb).wait()
                return carry

            lax.fori_loop(0, nb, wait, 0)


def _moe_ffn(xs, slot_e, nblk, n_used, w_gate, w_up, w_down, layer):
    D = xs.shape[1]
    F = w_gate.shape[3]
    R = MOE_SLOT_ROWS
    n_slots = slot_e.shape[0]
    tf = 256
    nf = F // tf

    def f_eff(s, f, nb):
        return jnp.where(nb[s] > 0, f, nf - 1)

    return pl.pallas_call(
        _moe_ffn_kernel,
        grid_spec=pltpu.PrefetchScalarGridSpec(
            num_scalar_prefetch=2,
            grid=(n_used, nf),
            in_specs=[
                pl.BlockSpec((R, D), lambda s, f, se, nb: (s, 0), pipeline_mode=pl.Buffered(1)),
                pl.BlockSpec((None, None, D, tf), lambda s, f, se, nb: (layer, se[s], 0, f_eff(s, f, nb))),
                pl.BlockSpec((None, None, D, tf), lambda s, f, se, nb: (layer, se[s], 0, f_eff(s, f, nb))),
                pl.BlockSpec((None, None, tf, D), lambda s, f, se, nb: (layer, se[s], f_eff(s, f, nb), 0)),
            ],
            out_specs=pl.BlockSpec(memory_space=pl.ANY),
            scratch_shapes=[pltpu.VMEM((R, D), F32), pltpu.VMEM((D, tf), BF16), pltpu.VMEM((D, tf), BF16),
                            pltpu.VMEM((tf, D), BF16), pltpu.SemaphoreType.DMA((1,))],
        ),
        out_shape=jax.ShapeDtypeStruct((n_slots * R, D), F32),
        compiler_params=_params("arbitrary", "arbitrary"),
        name="moe_ffn",
    )(slot_e, nblk, xs, w_gate, w_up, w_down)


def _combine_kernel(dest_ref, y_hbm, gate_ref, h_ref, mod_ref, g_ref, b_ref, out_ref, buf0, buf1, sem, *, n_tok):
    i = pl.program_id(0)
    tb = out_ref.shape[0]
    slot = i % 2

    def start(step, sl):
        _row_gather_start(y_hbm, buf0.at[sl], sem.at[0, sl], dest_ref, step * tb)
        _row_gather_start(y_hbm, buf1.at[sl], sem.at[1, sl], dest_ref, n_tok + step * tb)

    @pl.when(i == 0)
    def _():
        start(0, 0)

    @pl.when(i + 1 < pl.num_programs(0))
    def _():
        start(i + 1, 1 - slot)

    _row_gather_wait(buf0.at[slot], sem.at[0, slot])
    _row_gather_wait(buf1.at[slot], sem.at[1, slot])
    y = gate_ref[:, 0:1] * _gathered_rows(buf0[slot]) + gate_ref[:, 1:2] * _gathered_rows(buf1[slot])
    z = DEEPNORM_ALPHA * h_ref[...] + (1.0 + mod_ref[0, 5:6, :]) * y
    out_ref[...] = _layer_norm(z, g_ref[...], b_ref[...])


def _combine(y, dest, gates_tk, h, mod_l, ln_g, ln_b, S):
    T, D = h.shape
    tb = 256
    return pl.pallas_call(
        functools.partial(_combine_kernel, n_tok=T),
        grid_spec=pltpu.PrefetchScalarGridSpec(
            num_scalar_prefetch=1,
            grid=(T // tb,),
            in_specs=[
                pl.BlockSpec(memory_space=pl.ANY),
                pl.BlockSpec((tb, TOP_K), lambda i, d: (i, 0)),
                pl.BlockSpec((tb, D), lambda i, d: (i, 0)),
                pl.BlockSpec((1, 8, D), lambda i, d: (i * tb // S, 0, 0)),
                pl.BlockSpec((1, D), lambda i, d: (0, 0)),
                pl.BlockSpec((1, D), lambda i, d: (0, 0)),
            ],
            out_specs=pl.BlockSpec((tb, D), lambda i, d: (i, 0)),
            scratch_shapes=[pltpu.VMEM((2, tb // SUBLANES, SUBLANES, D), F32),
                            pltpu.VMEM((2, tb // SUBLANES, SUBLANES, D), F32),
                            pltpu.SemaphoreType.DMA((2, 2))],
        ),
        out_shape=jax.ShapeDtypeStruct((T, D), F32),
        compiler_params=_params("arbitrary"),
        name="moe_combine",
    )(dest, y, gates_tk, h, mod_l, ln_g.reshape(1, D), ln_b.reshape(1, D))


def _moe_layer(h, mod_l, router_w, router_b, w_gate, w_up, w_down, layer, ln_g, ln_b, S):
    T, D = h.shape
    E, R, rb = N_EXPERTS, MOE_SLOT_ROWS, MOE_ROW_BLOCK
    n_slots = (T * TOP_K + E * (R - 1)) // R
    n_rows = n_slots * R
    per_slot = R // rb
    max_blocks = T * TOP_K // rb + E
    u, idx, gates, rank, cnt = _router(h, mod_l, router_w, router_b, S)
    cnt = cnt[:, 0]
    slots_e = (cnt + R - 1) // R
    slot_end = jnp.cumsum(slots_e)
    slot_start = slot_end - slots_e
    row_start = slot_start * R
    idx_l = idx.reshape(-1, LANES)
    dest = rank.reshape(-1, LANES)
    for e in range(E):
        dest = dest + jnp.where(idx_l == e, row_start[e], 0)
    dest = dest.reshape(TOP_K, T)
    tok = jnp.tile(jnp.arange(T, dtype=jnp.int32), TOP_K)
    _, sorted_tok = lax.sort((dest.reshape(-1), tok), num_keys=1)
    sorted_tok = jnp.concatenate([sorted_tok, jnp.zeros((rb,), jnp.int32)])
    cstart = jnp.cumsum(cnt) - cnt
    sid = jnp.arange(n_slots, dtype=jnp.int32)
    n_used = slot_end[-1]
    last_used = jnp.maximum(n_used - 1, 0)
    se = jnp.minimum(jnp.searchsorted(slot_end, jnp.minimum(sid, last_used), side='right'), E - 1).astype(jnp.int32)
    rows_in = jnp.clip(cnt[se] - (sid - slot_start[se]) * R, 0, R)
    rows_in = jnp.where(sid < n_used, rows_in, 0)
    nblk = ((rows_in + rb - 1) // rb).astype(jnp.int32)
    blk = jnp.arange(n_rows // rb, dtype=jnp.int32)
    blk_valid = (blk % per_slot) < nblk[blk // per_slot]
    order = jnp.argsort(jnp.logical_not(blk_valid), stable=True).astype(jnp.int32)
    n_valid = jnp.sum(blk_valid.astype(jnp.int32))
    blk_ids = jnp.where(jnp.arange(max_blocks) < n_valid, order[:max_blocks], n_rows // rb).astype(jnp.int32)
    b_slot = jnp.minimum(blk_ids // per_slot, n_slots - 1)
    b_e = se[b_slot]
    blk_base = cstart[b_e] + (b_slot - slot_start[b_e]) * R + (blk_ids % per_slot) * rb
    blk_base = jnp.clip(blk_base, 0, T * TOP_K).astype(jnp.int32)
    xs = _dispatch(u, sorted_tok, blk_ids, blk_base, n_valid.reshape(1), n_rows)
    y = _moe_ffn(xs, se, nblk, n_used.astype(jnp.int32), w_gate, w_up, w_down, layer)
    return _combine(y, dest.reshape(-1).astype(jnp.int32), gates.T, h, mod_l, ln_g, ln_b, S)


def kernel(x, c, positions, ada_w, ada_b, ada_table, ln_g, ln_b, mla_w_dq, mla_q_norm, mla_w_uq, mla_w_dkv, mla_kv_norm, mla_w_ukv, mla_w_o, ca_w_qkv, ca_b_qkv, ca_rel_bias, ca_w_o, ffn_w_gate, ffn_w_up, ffn_w_down, moe_router_w, moe_router_b, moe_w_gate, moe_w_up, moe_w_down):
    B, S, D = x.shape
    T = B * S
    mod = _cond(c, ada_w, ada_b, ada_table)
    half = QK_ROPE // 2
    inv_freq = ROPE_THETA ** (-jnp.arange(half, dtype=F32) / half)
    freq_row = jnp.concatenate([inv_freq, inv_freq, jnp.zeros((LANES - QK_ROPE,), F32)]).reshape(1, LANES)
    pos_col = positions.reshape(T, 1).astype(jnp.int32)
    h = x.reshape(T, D)
    for i in range(DEPTH):
        j = i // 2
        mod_l = mod[i]
        if i % 2 == 0:
            q, k, v = _mla_proj(h, mod_l, pos_col, freq_row, mla_w_dq[j], mla_q_norm[j], mla_w_dkv[j],
                                mla_kv_norm[j], mla_w_uq[j], mla_w_ukv[j], S)
            o = _mla_attn(q, k, v, B, S)
            h = _proj_ln(o, mla_w_o, j, h, mod_l, ln_g[i, 0], ln_b[i, 0], S)
            h = _ffn(h, mod_l, ffn_w_gate, ffn_w_up, ffn_w_down, j, ln_g[i, 1], ln_b[i, 1], S)
        else:
            qkv = _qkv(h, mod_l, ca_w_qkv, j, ca_b_qkv[j], S)
            o = _ca_attn(qkv, ca_rel_bias[j], B, S)
            h = _proj_ln(o, ca_w_o, j, h, mod_l, ln_g[i, 0], ln_b[i, 0], S)
            h = _moe_layer(h, mod_l, moe_router_w[j], moe_router_b[j], moe_w_gate, moe_w_up,
                           moe_w_down, j, ln_g[i, 1], ln_b[i, 1], S)
    return h.reshape(B, S, D)
```

```python
import functools

import numpy as np
import jax
import jax.numpy as jnp
from jax import lax
from jax.experimental import pallas as pl
from jax.experimental.pallas import tpu as pltpu

F32 = jnp.float32
BF16 = jnp.bfloat16

CHUNK = 64
MLA_HEADS = 16
QK_NOPE = 128
QK_ROPE = 64
V_HEAD = 128
KV_LORA = 512
ROPE_THETA = 10000.0
CA_HEADS = 16
CA_HEAD_DIM = 128
LEFT_CHUNKS = 8
MAX_REL = 256
N_EXPERTS = 8
TOP_K = 2
N_MOD = 6
LN_EPS = 1e-5
RMS_EPS = 1e-6
DEPTH = 4
DEEPNORM_ALPHA = (2.0 * DEPTH) ** 0.25
LOG2E = 1.4426950408889634

LANES = 128
VMEM_LIMIT = 56 * 1024 * 1024

MLA_HEAD_PAD = 2 * LANES
MOE_SLOT_ROWS = 2560
MOE_ROW_BLOCK = 256
CA_QBLOCK = 256
CA_KBLOCKS = 3
ATTN_HEADS_PER_STEP = 16

_NT = (((1,), (1,)), ((), ()))


def _params(*sem):
    return pltpu.CompilerParams(dimension_semantics=sem, vmem_limit_bytes=VMEM_LIMIT)


def _silu(x):
    return x * jax.nn.sigmoid(x)


def _layer_norm(z, g, b):
    mu = jnp.mean(z, axis=-1, keepdims=True)
    zc = z - mu
    var = jnp.mean(zc * zc, axis=-1, keepdims=True)
    return zc * lax.rsqrt(var + LN_EPS) * g + b


def _modulate(h, mod_ref, k):
    return h * (1.0 + mod_ref[0, k + 1:k + 2, :]) + mod_ref[0, k:k + 1, :]


def _cond_kernel(c_ref, w_ref, b_ref, tab_ref, o_ref):
    s = _silu(c_ref[...]).astype(BF16)
    y = jnp.dot(s, w_ref[...].astype(BF16), preferred_element_type=F32) + b_ref[...]
    o_ref[...] = y[None, :, :] + tab_ref[...][:, None, :]


def _cond(c, ada_w, ada_b, ada_table):
    B, D = c.shape
    N = ada_w.shape[1]
    tn = 1024
    c8 = jnp.zeros((8, D), F32).at[:B].set(c)
    out = pl.pallas_call(
        _cond_kernel,
        grid=(N // tn,),
        in_specs=[
            pl.BlockSpec((8, D), lambda n: (0, 0)),
            pl.BlockSpec((D, tn), lambda n: (0, n)),
            pl.BlockSpec((1, tn), lambda n: (0, n)),
            pl.BlockSpec((DEPTH, tn), lambda n: (0, n)),
        ],
        out_specs=pl.BlockSpec((DEPTH, 8, tn), lambda n: (0, 0, n)),
        out_shape=jax.ShapeDtypeStruct((DEPTH, 8, N), F32),
        compiler_params=_params("arbitrary"),
        name="cond",
    )(c8, ada_w, ada_b.reshape(1, N), ada_table.reshape(DEPTH, N))
    mod = out[:, :B].reshape(DEPTH, B, N_MOD, D)
    return jnp.pad(mod, ((0, 0), (0, 0), (0, 8 - N_MOD), (0, 0)))


def _rope_lanes(x, cos, sin_signed):
    lane = lax.broadcasted_iota(jnp.int32, x.shape, 1)
    partner = jnp.where(lane < QK_ROPE // 2, pltpu.roll(x, LANES - QK_ROPE // 2, 1),
                        pltpu.roll(x, QK_ROPE // 2, 1))
    return x * cos + partner * sin_signed


def _mla_proj_kernel(h_ref, mod_ref, pos_ref, freq_ref, wdq_ref, qn_ref, wdkv_ref, kvn_ref, wkr_ref,
                     wuq_ref, wukv_ref, q_ref, k_ref, v_ref):
    u = _modulate(h_ref[...], mod_ref, 0).astype(BF16)
    ang = pos_ref[...].astype(F32) * freq_ref[...]
    cos = jnp.cos(ang)
    sin = jnp.sin(ang)
    lane = lax.broadcasted_iota(jnp.int32, ang.shape, 1)
    sin_signed = jnp.where(lane < QK_ROPE // 2, -sin, sin)

    def rms(x, g):
        return (x * lax.rsqrt(jnp.mean(x * x, axis=-1, keepdims=True) + RMS_EPS)) * g

    cq = rms(jnp.dot(u, wdq_ref[...], preferred_element_type=F32), qn_ref[...]).astype(BF16)
    ckv = rms(jnp.dot(u, wdkv_ref[...], preferred_element_type=F32), kvn_ref[...]).astype(BF16)
    kr = jnp.dot(u, wkr_ref[...], preferred_element_type=F32)
    kr = _rope_lanes(kr, cos, sin_signed).astype(BF16)
    for hd in range(MLA_HEADS):
        c0 = hd * MLA_HEAD_PAD
        qh = jnp.dot(cq, wuq_ref[:, c0:c0 + MLA_HEAD_PAD], preferred_element_type=F32)
        q_ref[:, c0:c0 + LANES] = qh[:, :LANES].astype(BF16)
        q_ref[:, c0 + LANES:c0 + MLA_HEAD_PAD] = _rope_lanes(qh[:, LANES:], cos, sin_signed).astype(BF16)
        kvh = jnp.dot(ckv, wukv_ref[:, c0:c0 + MLA_HEAD_PAD], preferred_element_type=F32)
        k_ref[:, c0:c0 + LANES] = kvh[:, :LANES].astype(BF16)
        k_ref[:, c0 + LANES:c0 + MLA_HEAD_PAD] = kr
        v_ref[:, hd * V_HEAD:(hd + 1) * V_HEAD] = kvh[:, LANES:].astype(BF16)


def _mla_proj(h, mod_l, pos_col, freq_row, w_dq, q_norm, w_dkv, kv_norm, w_uq, w_ukv, S):
    T, D = h.shape
    H = MLA_HEADS
    tm = 256
    q_lora = w_dq.shape[1]
    wdq = w_dq.astype(BF16)
    wdkv = w_dkv[:, :KV_LORA].astype(BF16)
    wkr = jnp.pad(w_dkv[:, KV_LORA:], ((0, 0), (0, LANES - QK_ROPE))).astype(BF16)
    wuq = jnp.pad(w_uq.reshape(q_lora, H, QK_NOPE + QK_ROPE),
                  ((0, 0), (0, 0), (0, MLA_HEAD_PAD - QK_NOPE - QK_ROPE))).reshape(q_lora, H * MLA_HEAD_PAD)
    wuq = wuq.astype(BF16)
    wukv = w_ukv.astype(BF16)
    const = lambda i: (0, 0)
    row = lambda i: (i, 0)
    return pl.pallas_call(
        _mla_proj_kernel,
        grid=(T // tm,),
        in_specs=[
            pl.BlockSpec((tm, D), row),
            pl.BlockSpec((1, 8, D), lambda i: (i * tm // S, 0, 0)),
            pl.BlockSpec((tm, 1), row),
            pl.BlockSpec((1, LANES), const),
            pl.BlockSpec(wdq.shape, const),
            pl.BlockSpec((1, q_lora), const),
            pl.BlockSpec(wdkv.shape, const),
            pl.BlockSpec((1, KV_LORA), const),
            pl.BlockSpec(wkr.shape, const),
            pl.BlockSpec(wuq.shape, const),
            pl.BlockSpec(wukv.shape, const),
        ],
        out_specs=[
            pl.BlockSpec((tm, H * MLA_HEAD_PAD), row),
            pl.BlockSpec((tm, H * MLA_HEAD_PAD), row),
            pl.BlockSpec((tm, H * V_HEAD), row),
        ],
        out_shape=[
            jax.ShapeDtypeStruct((T, H * MLA_HEAD_PAD), BF16),
            jax.ShapeDtypeStruct((T, H * MLA_HEAD_PAD), BF16),
            jax.ShapeDtypeStruct((T, H * V_HEAD), BF16),
        ],
        compiler_params=_params("arbitrary"),
        name="mla_proj",
    )(h, mod_l, pos_col, freq_row, wdq, q_norm.reshape(1, -1), wdkv, kv_norm.reshape(1, -1), wkr, wuq, wukv)


def _with_ones(v):
    return jnp.concatenate([v, jnp.ones_like(v)], axis=1)


def _mla_attn_kernel(pi_ref, pj_ref, q_ref, k_ref, v_ref, o_ref, m_sc, acc_sc, *, scale):
    p = pl.program_id(2)
    i = pi_ref[p]
    j = pj_ref[p]

    @pl.when(j == 0)
    def _():
        m_sc[...] = jnp.full(m_sc.shape, -jnp.inf, F32)
        acc_sc[...] = jnp.zeros(acc_sc.shape, F32)

    def accumulate(diagonal):
        for hd in range(ATTN_HEADS_PER_STEP):
            qk = slice(hd * MLA_HEAD_PAD, (hd + 1) * MLA_HEAD_PAD)
            st = lax.dot_general(k_ref[:, qk], q_ref[:, qk], _NT, preferred_element_type=F32)
            if diagonal:
                kc_t = lax.broadcasted_iota(jnp.int32, st.shape, 0) // CHUNK
                qc_t = lax.broadcasted_iota(jnp.int32, st.shape, 1) // CHUNK
                st = jnp.where(kc_t <= qc_t, st, -jnp.inf)
            row_max = jnp.transpose(jnp.max(st, axis=0, keepdims=True)) * scale
            s = lax.dot_general(q_ref[:, qk], k_ref[:, qk], _NT, preferred_element_type=F32) * scale
            if diagonal:
                qc = lax.broadcasted_iota(jnp.int32, s.shape, 0) // CHUNK
                kc = lax.broadcasted_iota(jnp.int32, s.shape, 1) // CHUNK
                s = jnp.where(kc <= qc, s, -jnp.inf)
            m_prev = m_sc[hd]
            m_new = jnp.maximum(m_prev, row_max)
            a = jnp.exp2(m_prev - m_new)
            e = jnp.exp2(s - m_new).astype(BF16)
            v = _with_ones(v_ref[:, hd * V_HEAD:(hd + 1) * V_HEAD])
            acc_sc[hd] = a * acc_sc[hd] + jnp.dot(e, v, preferred_element_type=F32)
            m_sc[hd] = m_new

    @pl.when(j < i)
    def _():
        accumulate(False)

    @pl.when(j == i)
    def _():
        accumulate(True)
        for hd in range(ATTN_HEADS_PER_STEP):
            acc = acc_sc[hd]
            o_ref[:, hd * V_HEAD:(hd + 1) * V_HEAD] = (acc[:, :V_HEAD] / acc[:, V_HEAD:]).astype(o_ref.dtype)


def _mla_attn(q, k, v, B, S):
    H = MLA_HEADS
    hp = ATTN_HEADS_PER_STEP
    tq = 512
    nq = S // tq
    scale = (QK_NOPE + QK_ROPE) ** -0.5 * LOG2E
    pairs = [(i, j) for i in range(nq) for j in range(i + 1)]
    pi = jnp.asarray([p[0] for p in pairs], jnp.int32)
    pj = jnp.asarray([p[1] for p in pairs], jnp.int32)
    return pl.pallas_call(
        functools.partial(_mla_attn_kernel, scale=scale),
        grid_spec=pltpu.PrefetchScalarGridSpec(
            num_scalar_prefetch=2,
            grid=(B, H // hp, len(pairs)),
            in_specs=[
                pl.BlockSpec((tq, hp * MLA_HEAD_PAD), lambda b, g, p, pi, pj: (b * nq + pi[p], g)),
                pl.BlockSpec((tq, hp * MLA_HEAD_PAD), lambda b, g, p, pi, pj: (b * nq + pj[p], g)),
                pl.BlockSpec((tq, hp * V_HEAD), lambda b, g, p, pi, pj: (b * nq + pj[p], g)),
            ],
            out_specs=pl.BlockSpec((tq, hp * V_HEAD), lambda b, g, p, pi, pj: (b * nq + pi[p], g)),
            scratch_shapes=[pltpu.VMEM((hp, tq, 1), F32), pltpu.VMEM((hp, tq, 2 * V_HEAD), F32)],
        ),
        out_shape=jax.ShapeDtypeStruct((B * S, H * V_HEAD), BF16),
        compiler_params=_params("arbitrary", "arbitrary", "arbitrary"),
        name="mla_attn",
    )(pi, pj, q, k, v)


def _proj_ln_kernel(o_ref, w_ref, h_ref, mod_ref, g_ref, b_ref, out_ref, w_sc, *, gate_row):
    @pl.when(pl.program_id(0) == 0)
    def _():
        w_sc[...] = w_ref[...].astype(BF16)

    half = out_ref.shape[0] // 2
    for part in range(2):
        rows = slice(part * half, (part + 1) * half)
        y = jnp.dot(o_ref[rows, :], w_sc[...], preferred_element_type=F32)
        z = DEEPNORM_ALPHA * h_ref[rows, :] + (1.0 + mod_ref[0, gate_row:gate_row + 1, :]) * y
        out_ref[rows, :] = _layer_norm(z, g_ref[...], b_ref[...])


def _proj_ln(o, w_o, layer, h, mod_l, ln_g, ln_b, S):
    T, D = h.shape
    tm = 512
    K = w_o.shape[1]
    row = lambda i: (i, 0)
    const = lambda i: (0, 0)
    return pl.pallas_call(
        functools.partial(_proj_ln_kernel, gate_row=2),
        grid=(T // tm,),
        in_specs=[
            pl.BlockSpec((tm, K), row),
            pl.BlockSpec((None, K, D), lambda i: (layer, 0, 0), pipeline_mode=pl.Buffered(1)),
            pl.BlockSpec((tm, D), row),
            pl.BlockSpec((1, 8, D), lambda i: (i * tm // S, 0, 0)),
            pl.BlockSpec((1, D), const),
            pl.BlockSpec((1, D), const),
        ],
        out_specs=pl.BlockSpec((tm, D), row),
        out_shape=jax.ShapeDtypeStruct((T, D), F32),
        scratch_shapes=[pltpu.VMEM((K, D), BF16)],
        compiler_params=_params("arbitrary"),
        name="proj_ln",
    )(o, w_o, h, mod_l, ln_g.reshape(1, D), ln_b.reshape(1, D))


def _ffn_kernel(h_ref, mod_ref, wg_ref, wu_ref, wd_ref, g_ref, b_ref, out_ref, u_sc):
    f = pl.program_id(1)

    @pl.when(f == 0)
    def _():
        u_sc[...] = _modulate(h_ref[...], mod_ref, 3).astype(BF16)
        out_ref[...] = jnp.zeros(out_ref.shape, F32)

    u = u_sc[...]
    g = jnp.dot(u, wg_ref[...].astype(BF16), preferred_element_type=F32)
    up = jnp.dot(u, wu_ref[...].astype(BF16), preferred_element_type=F32)
    hm = (_silu(g) * up).astype(BF16)
    out_ref[...] += jnp.dot(hm, wd_ref[...].astype(BF16), preferred_element_type=F32)

    @pl.when(f == pl.num_programs(1) - 1)
    def _():
        z = DEEPNORM_ALPHA * h_ref[...] + (1.0 + mod_ref[0, 5:6, :]) * out_ref[...]
        out_ref[...] = _layer_norm(z, g_ref[...], b_ref[...])


def _ffn(h, mod_l, w_gate, w_up, w_down, layer, ln_g, ln_b, S):
    T, D = h.shape
    F = w_gate.shape[2]
    tm = min(1024, T)
    tf = 256
    row = lambda i, f: (i, 0)
    const = lambda i, f: (0, 0)
    return pl.pallas_call(
        _ffn_kernel,
        grid=(T // tm, F // tf),
        in_specs=[
            pl.BlockSpec((tm, D), row, pipeline_mode=pl.Buffered(1)),
            pl.BlockSpec((1, 8, D), lambda i, f: (i * tm // S, 0, 0)),
            pl.BlockSpec((None, D, tf), lambda i, f: (layer, 0, f)),
            pl.BlockSpec((None, D, tf), lambda i, f: (layer, 0, f)),
            pl.BlockSpec((None, tf, D), lambda i, f: (layer, f, 0)),
            pl.BlockSpec((1, D), const),
            pl.BlockSpec((1, D), const),
        ],
        out_specs=pl.BlockSpec((tm, D), row),
        out_shape=jax.ShapeDtypeStruct((T, D), F32),
        scratch_shapes=[pltpu.VMEM((tm, D), BF16)],
        compiler_params=_params("arbitrary", "arbitrary"),
        name="ffn",
    )(h, mod_l, w_gate, w_up, w_down, ln_g.reshape(1, D), ln_b.reshape(1, D))


def _qkv_kernel(h_ref, mod_ref, w_ref, b_ref, o_ref, u_sc):
    @pl.when(pl.program_id(1) == 0)
    def _():
        u_sc[...] = _modulate(h_ref[...], mod_ref, 0).astype(BF16)

    y = jnp.dot(u_sc[...], w_ref[...], preferred_element_type=F32) + b_ref[...]
    o_ref[...] = y.astype(o_ref.dtype)


def _qkv(h, mod_l, w_qkv, layer, b_qkv, S):
    T, D = h.shape
    N = w_qkv.shape[2]
    tm = min(1024, T)
    tn = 1024
    w = w_qkv[layer].astype(BF16)
    return pl.pallas_call(
        _qkv_kernel,
        grid=(T // tm, N // tn),
        in_specs=[
            pl.BlockSpec((tm, D), lambda i, n: (i, 0), pipeline_mode=pl.Buffered(1)),
            pl.BlockSpec((1, 8, D), lambda i, n: (i * tm // S, 0, 0)),
            pl.BlockSpec((D, tn), lambda i, n: (0, n)),
            pl.BlockSpec((1, tn), lambda i, n: (0, n)),
        ],
        out_specs=pl.BlockSpec((tm, tn), lambda i, n: (i, n)),
        out_shape=jax.ShapeDtypeStruct((T, N), BF16),
        scratch_shapes=[pltpu.VMEM((tm, D), BF16)],
        compiler_params=_params("arbitrary", "arbitrary"),
        name="qkv",
    )(h, mod_l, w, b_qkv.reshape(1, N))


def _ca_attn_kernel(q_ref, k0_ref, k1_ref, k2_ref, v0_ref, v1_ref, v2_ref, bias_ref, o_ref, *, scale):
    i = pl.program_id(2)
    tq = q_ref.shape[0]
    k_refs = (k0_ref, k1_ref, k2_ref)
    v_refs = (v0_ref, v1_ref, v2_ref)

    def attend(stream_start):
        for hd in range(ATTN_HEADS_PER_STEP):
            cols = slice(hd * CA_HEAD_DIM, (hd + 1) * CA_HEAD_DIM)
            q = q_ref[:, cols]
            s = []
            for p in range(CA_KBLOCKS):
                sp = lax.dot_general(q, k_refs[p][:, cols], _NT, preferred_element_type=F32) * scale
                sp = sp + bias_ref[hd, :, p * tq:(p + 1) * tq]
                first = CA_KBLOCKS - 1 - p
                if stream_start and first > 0:
                    sp = jnp.where(i >= first, sp, -jnp.inf)
                s.append(sp)
            m = functools.reduce(jnp.maximum, [jnp.max(sp, axis=-1, keepdims=True) for sp in s])
            acc = functools.reduce(jnp.add, [
                jnp.dot(jnp.exp2(sp - m).astype(BF16), _with_ones(v_refs[p][:, cols]), preferred_element_type=F32)
                for p, sp in enumerate(s)])
            o_ref[:, cols] = (acc[:, :CA_HEAD_DIM] / acc[:, CA_HEAD_DIM:]).astype(o_ref.dtype)

    @pl.when(i < CA_KBLOCKS - 1)
    def _():
        attend(True)

    @pl.when(i >= CA_KBLOCKS - 1)
    def _():
        attend(False)


def _ca_bias_kernel(w_ref, o_ref):
    tq, nk = o_ref.shape[1], o_ref.shape[2]
    W = w_ref.shape[2]
    x = jnp.broadcast_to(w_ref[0], (tq, W))
    t = pltpu.roll(x, W - (tq - 1), 1, stride=1, stride_axis=0)[:, :nk]
    qc = lax.broadcasted_iota(jnp.int32, (tq, nk), 0) // CHUNK + (nk - tq) // CHUNK
    kc = lax.broadcasted_iota(jnp.int32, (tq, nk), 1) // CHUNK
    o_ref[0] = jnp.where((kc <= qc) & (kc >= qc - LEFT_CHUNKS), t * LOG2E, -jnp.inf)


def _ca_bias_table(rel_bias):
    H = rel_bias.shape[0]
    tq = CA_QBLOCK
    nk = CA_KBLOCKS * tq
    W = tq + nk
    d = np.arange(W) - (tq - 1)
    rel_idx = np.clip((nk - tq) - d, -MAX_REL, MAX_REL) + MAX_REL
    w = rel_bias[:, rel_idx].astype(F32).reshape(H, 1, W)
    return pl.pallas_call(
        _ca_bias_kernel,
        grid=(H,),
        in_specs=[pl.BlockSpec((1, 1, W), lambda h: (h, 0, 0))],
        out_specs=pl.BlockSpec((1, tq, nk), lambda h: (h, 0, 0)),
        out_shape=jax.ShapeDtypeStruct((H, tq, nk), F32),
        compiler_params=_params("arbitrary"),
        name="ca_bias",
    )(w)


def _ca_attn(qkv, rel_bias, B, S):
    H, Dh = CA_HEADS, CA_HEAD_DIM
    hp = ATTN_HEADS_PER_STEP
    ng = H // hp
    tq = CA_QBLOCK
    nq = S // tq
    bias = _ca_bias_table(rel_bias)

    def kv_spec(which, p):
        back = CA_KBLOCKS - 1 - p
        return pl.BlockSpec((tq, hp * Dh), lambda b, g, i: (b * nq + jnp.maximum(i - back, 0), which * ng + g))

    return pl.pallas_call(
        functools.partial(_ca_attn_kernel, scale=Dh ** -0.5 * LOG2E),
        grid=(B, ng, nq),
        in_specs=[pl.BlockSpec((tq, hp * Dh), lambda b, g, i: (b * nq + i, g))]
        + [kv_spec(1, p) for p in range(CA_KBLOCKS)]
        + [kv_spec(2, p) for p in range(CA_KBLOCKS)]
        + [pl.BlockSpec((hp, tq, CA_KBLOCKS * tq), lambda b, g, i: (g, 0, 0))],
        out_specs=pl.BlockSpec((tq, hp * Dh), lambda b, g, i: (b * nq + i, g)),
        out_shape=jax.ShapeDtypeStruct((B * S, H * Dh), BF16),
        compiler_params=_params("arbitrary", "arbitrary", "arbitrary"),
        name="ca_attn",
    )(qkv, qkv, qkv, qkv, qkv, qkv, qkv, bias)


def _pack_bf16_halves(x):
    half = x.shape[1] // 2
    lo = pltpu.bitcast(x[:, :half].astype(BF16).astype(F32), jnp.uint32)
    hi = pltpu.bitcast(x[:, half:].astype(BF16).astype(F32), jnp.uint32)
    return lax.shift_right_logical(lo, jnp.uint32(16)) | (hi & jnp.uint32(0xFFFF0000))


def _unpack_bf16_halves(p):
    lo = pltpu.bitcast(lax.shift_left(p, jnp.uint32(16)), F32).astype(BF16)
    hi = pltpu.bitcast(p & jnp.uint32(0xFFFF0000), F32).astype(BF16)
    return jnp.concatenate([lo, hi], axis=1)


def _router_kernel(h_ref, mod_ref, rw_ref, rb_ref, u_ref, idx_ref, gate_ref, rank_ref, cnt_ref, carry_sc):
    t = pl.program_id(0)

    @pl.when(t == 0)
    def _():
        carry_sc[...] = jnp.zeros(carry_sc.shape, F32)

    u = _modulate(h_ref[...], mod_ref, 3)
    u_ref[...] = _pack_bf16_halves(u)
    tb = u.shape[0]
    logits = lax.dot_general(rw_ref[...], u, _NT, precision=lax.Precision.HIGHEST,
                             preferred_element_type=F32) + rb_ref[...]
    eid = lax.broadcasted_iota(jnp.int32, logits.shape, 0)
    m1 = jnp.max(logits, axis=0, keepdims=True)
    i1 = jnp.min(jnp.where(logits == m1, eid, N_EXPERTS), axis=0, keepdims=True)
    rest = jnp.where(eid == i1, -jnp.inf, logits)
    m2 = jnp.max(rest, axis=0, keepdims=True)
    i2 = jnp.min(jnp.where(rest == m2, eid, N_EXPERTS), axis=0, keepdims=True)
    e2 = jnp.exp(m2 - m1)
    den = 1.0 + e2
    idx_ref[...] = jnp.concatenate([i1, i2], axis=0)
    gate_ref[...] = jnp.concatenate([1.0 / den, e2 / den], axis=0)
    onehot = jnp.where((eid == i1) | (eid == i2), 1.0, 0.0)
    tri = jnp.where(lax.broadcasted_iota(jnp.int32, (tb, tb), 0) <= lax.broadcasted_iota(jnp.int32, (tb, tb), 1),
                    1.0, 0.0).astype(BF16)
    incl = jnp.dot(onehot.astype(BF16), tri, preferred_element_type=F32)
    excl = incl - onehot + carry_sc[:, 0:1]
    r1 = jnp.sum(jnp.where(eid == i1, excl, 0.0), axis=0, keepdims=True)
    r2 = jnp.sum(jnp.where(eid == i2, excl, 0.0), axis=0, keepdims=True)
    rank_ref[...] = jnp.concatenate([r1, r2], axis=0).astype(jnp.int32)
    carry_sc[...] = carry_sc[...] + jnp.sum(onehot, axis=1, keepdims=True)
    cnt_ref[...] = carry_sc[...].astype(jnp.int32)


def _router(h, mod_l, router_w, router_b, S):
    T, D = h.shape
    E = N_EXPERTS
    tb = 512
    return pl.pallas_call(
        _router_kernel,
        grid=(T // tb,),
        in_specs=[
            pl.BlockSpec((tb, D), lambda t: (t, 0)),
            pl.BlockSpec((1, 8, D), lambda t: (t * tb // S, 0, 0)),
            pl.BlockSpec((E, D), lambda t: (0, 0)),
            pl.BlockSpec((E, 1), lambda t: (0, 0)),
        ],
        out_specs=[
            pl.BlockSpec((tb, D // 2), lambda t: (t, 0)),
            pl.BlockSpec((TOP_K, tb), lambda t: (0, t)),
            pl.BlockSpec((TOP_K, tb), lambda t: (0, t)),
            pl.BlockSpec((TOP_K, tb), lambda t: (0, t)),
            pl.BlockSpec((E, LANES), lambda t: (0, 0)),
        ],
        out_shape=[
            jax.ShapeDtypeStruct((T, D // 2), jnp.uint32),
            jax.ShapeDtypeStruct((TOP_K, T), jnp.int32),
            jax.ShapeDtypeStruct((TOP_K, T), F32),
            jax.ShapeDtypeStruct((TOP_K, T), jnp.int32),
            jax.ShapeDtypeStruct((E, LANES), jnp.int32),
        ],
        scratch_shapes=[pltpu.VMEM((E, LANES), F32)],
        compiler_params=_params("arbitrary"),
        name="router",
    )(h, mod_l, router_w.T, router_b.reshape(E, 1))


SUBLANES = 8


def _row_gather_group(src_hbm, dst, sem, rows_ref, base, g):
    for k in range(SUBLANES):
        r = rows_ref[base + g * SUBLANES + k]
        pltpu.make_async_copy(src_hbm.at[pl.ds(r, 1), :], dst.at[g, pl.ds(k, 1), :], sem).start(priority=k % 2)


def _row_gather_start(src_hbm, dst, sem, rows_ref, base):
    def issue(g, carry):
        _row_gather_group(src_hbm, dst, sem, rows_ref, base, g)
        return carry
    lax.fori_loop(0, dst.shape[0], issue, 0)


def _row_gather_wait(dst, sem):
    pltpu.make_async_copy(dst, dst, sem).wait()


def _gathered_rows(buf):
    return buf.reshape(buf.shape[0] * SUBLANES, buf.shape[2])


def _dispatch_kernel(row_tok_ref, blk_ref, base_ref, nv_ref, u_hbm, o_ref, buf, sem):
    c = pl.program_id(0)
    nv = nv_ref[0]
    rb = MOE_ROW_BLOCK
    slot = c % 2

    def start(step, sl):
        _row_gather_start(u_hbm, buf.at[sl], sem.at[sl], row_tok_ref, base_ref[step])

    @pl.when((c == 0) & (nv > 0))
    def _():
        start(0, 0)

    @pl.when(c + 1 < nv)
    def _():
        start(c + 1, 1 - slot)

    @pl.when(c < nv)
    def _():
        _row_gather_wait(buf.at[slot], sem.at[slot])
        o_ref[...] = _unpack_bf16_halves(_gathered_rows(buf[slot]))

    @pl.when(c >= nv)
    def _():
        o_ref[...] = jnp.zeros(o_ref.shape, BF16)


def _dispatch(u, row_tok, blk_ids, blk_base, n_valid, n_rows):
    T, half = u.shape
    D = 2 * half
    rb = MOE_ROW_BLOCK
    return pl.pallas_call(
        _dispatch_kernel,
        grid_spec=pltpu.PrefetchScalarGridSpec(
            num_scalar_prefetch=4,
            grid=(blk_ids.shape[0],),
            in_specs=[pl.BlockSpec(memory_space=pl.ANY)],
            out_specs=pl.BlockSpec((rb, D), lambda c, rt, blk, base, nv: (blk[c], 0)),
            scratch_shapes=[pltpu.VMEM((2, rb // SUBLANES, SUBLANES, half), jnp.uint32),
                            pltpu.SemaphoreType.DMA((2,))],
        ),
        out_shape=jax.ShapeDtypeStruct((n_rows + rb, D), BF16),
        compiler_params=_params("arbitrary"),
        name="moe_dispatch",
    )(row_tok, blk_ids, blk_base, n_valid, u)


def _moe_ffn_kernel(se_ref, nblk_ref, x_ref, wg_ref, wu_ref, wd_ref, y_hbm, acc, wg_sc, wu_sc, wd_sc, sem):
    s = pl.program_id(0)
    f = pl.program_id(1)
    nb = nblk_ref[s]
    rb = MOE_ROW_BLOCK

    @pl.when(nb > 0)
    def _():
        @pl.when(f == 0)
        def _():
            acc[...] = jnp.zeros(acc.shape, F32)

        wg_sc[...] = wg_ref[...].astype(BF16)
        wu_sc[...] = wu_ref[...].astype(BF16)
        wd_sc[...] = wd_ref[...].astype(BF16)

        last_f = f == pl.num_programs(1) - 1

        def out_copy(start, size):
            return pltpu.make_async_copy(
                acc.at[pl.ds(pl.multiple_of(start, rb), size), :],
                y_hbm.at[pl.ds(pl.multiple_of(s * MOE_SLOT_ROWS + start, rb), size), :], sem.at[0])

        def rows_at(start, size):
            rows = pl.ds(pl.multiple_of(start, rb), size)
            x = x_ref[rows, :]
            g = jnp.dot(x, wg_sc[...], preferred_element_type=F32)
            up = jnp.dot(x, wu_sc[...], preferred_element_type=F32)
            hm = (_silu(g) * up).astype(BF16)
            acc[rows, :] += jnp.dot(hm, wd_sc[...], preferred_element_type=F32)

            @pl.when(last_f)
            def _():
                out_copy(start, size).start()

        def quad(r, carry):
            rows_at(r * (4 * rb), 4 * rb)
            return carry

        lax.fori_loop(0, nb // 4, quad, 0)

        @pl.when(nb % 4 >= 2)
        def _():
            rows_at((nb // 4) * (4 * rb), 2 * rb)

        @pl.when(nb % 2 == 1)
        def _():
            rows_at((nb - 1) * rb, rb)

        @pl.when(last_f)
        def _():
            def wait(r, carry):
                out_copy(r * rb, rb).wait()
                return carry

            lax.fori_loop(0, nb, wait, 0)


def _moe_ffn(xs, slot_e, nblk, n_used, w_gate, w_up, w_down, layer):
    D = xs.shape[1]
    F = w_gate.shape[3]
    R = MOE_SLOT_ROWS
    n_slots = slot_e.shape[0]
    tf = 256
    nf = F // tf

    def f_eff(s, f, nb):
        return jnp.where(nb[s] > 0, f, nf - 1)

    return pl.pallas_call(
        _moe_ffn_kernel,
        grid_spec=pltpu.PrefetchScalarGridSpec(
            num_scalar_prefetch=2,
            grid=(n_used, nf),
            in_specs=[
                pl.BlockSpec((R, D), lambda s, f, se, nb: (s, 0), pipeline_mode=pl.Buffered(1)),
                pl.BlockSpec((None, None, D, tf), lambda s, f, se, nb: (layer, se[s], 0, f_eff(s, f, nb))),
                pl.BlockSpec((None, None, D, tf), lambda s, f, se, nb: (layer, se[s], 0, f_eff(s, f, nb))),
                pl.BlockSpec((None, None, tf, D), lambda s, f, se, nb: (layer, se[s], f_eff(s, f, nb), 0)),
            ],
            out_specs=pl.BlockSpec(memory_space=pl.ANY),
            scratch_shapes=[pltpu.VMEM((R, D), F32), pltpu.VMEM((D, tf), BF16), pltpu.VMEM((D, tf), BF16),
                            pltpu.VMEM((tf, D), BF16), pltpu.SemaphoreType.DMA((1,))],
        ),
        out_shape=jax.ShapeDtypeStruct((n_slots * R, D), F32),
        compiler_params=_params("arbitrary", "arbitrary"),
        name="moe_ffn",
    )(slot_e, nblk, xs, w_gate, w_up, w_down)


def _combine_kernel(dest_ref, y_hbm, gate_ref, h_ref, mod_ref, g_ref, b_ref, out_ref, buf0, buf1, sem, *, n_tok):
    i = pl.program_id(0)
    tb = out_ref.shape[0]
    slot = i % 2

    def start(step, sl):
        _row_gather_start(y_hbm, buf0.at[sl], sem.at[0, sl], dest_ref, step * tb)
        _row_gather_start(y_hbm, buf1.at[sl], sem.at[1, sl], dest_ref, n_tok + step * tb)

    @pl.when(i == 0)
    def _():
        start(0, 0)

    @pl.when(i + 1 < pl.num_programs(0))
    def _():
        start(i + 1, 1 - slot)

    _row_gather_wait(buf0.at[slot], sem.at[0, slot])
    _row_gather_wait(buf1.at[slot], sem.at[1, slot])
    y = gate_ref[:, 0:1] * _gathered_rows(buf0[slot]) + gate_ref[:, 1:2] * _gathered_rows(buf1[slot])
    z = DEEPNORM_ALPHA * h_ref[...] + (1.0 + mod_ref[0, 5:6, :]) * y
    out_ref[...] = _layer_norm(z, g_ref[...], b_ref[...])


def _combine(y, dest, gates_tk, h, mod_l, ln_g, ln_b, S):
    T, D = h.shape
    tb = 256
    return pl.pallas_call(
        functools.partial(_combine_kernel, n_tok=T),
        grid_spec=pltpu.PrefetchScalarGridSpec(
            num_scalar_prefetch=1,
            grid=(T // tb,),
            in_specs=[
                pl.BlockSpec(memory_space=pl.ANY),
                pl.BlockSpec((tb, TOP_K), lambda i, d: (i, 0)),
                pl.BlockSpec((tb, D), lambda i, d: (i, 0)),
                pl.BlockSpec((1, 8, D), lambda i, d: (i * tb // S, 0, 0)),
                pl.BlockSpec((1, D), lambda i, d: (0, 0)),
                pl.BlockSpec((1, D), lambda i, d: (0, 0)),
            ],
            out_specs=pl.BlockSpec((tb, D), lambda i, d: (i, 0)),
            scratch_shapes=[pltpu.VMEM((2, tb // SUBLANES, SUBLANES, D), F32),
                            pltpu.VMEM((2, tb // SUBLANES, SUBLANES, D), F32),
                            pltpu.SemaphoreType.DMA((2, 2))],
        ),
        out_shape=jax.ShapeDtypeStruct((T, D), F32),
        compiler_params=_params("arbitrary"),
        name="moe_combine",
    )(dest, y, gates_tk, h, mod_l, ln_g.reshape(1, D), ln_b.reshape(1, D))


def _moe_layer(h, mod_l, router_w, router_b, w_gate, w_up, w_down, layer, ln_g, ln_b, S):
    T, D = h.shape
    E, R, rb = N_EXPERTS, MOE_SLOT_ROWS, MOE_ROW_BLOCK
    n_slots = (T * TOP_K + E * (R - 1)) // R
    n_rows = n_slots * R
    per_slot = R // rb
    max_blocks = T * TOP_K // rb + E
    u, idx, gates, rank, cnt = _router(h, mod_l, router_w, router_b, S)
    cnt = cnt[:, 0]
    slots_e = (cnt + R - 1) // R
    slot_end = jnp.cumsum(slots_e)
    slot_start = slot_end - slots_e
    row_start = slot_start * R
    dest = rank
    for e in range(E):
        dest = dest + jnp.where(idx == e, row_start[e], 0)
    tok = jnp.tile(jnp.arange(T, dtype=jnp.int32), TOP_K)
    _, sorted_tok = lax.sort((dest.reshape(-1), tok), num_keys=1)
    sorted_tok = jnp.concatenate([sorted_tok, jnp.zeros((rb,), jnp.int32)])
    cstart = jnp.cumsum(cnt) - cnt
    sid = jnp.arange(n_slots, dtype=jnp.int32)
    n_used = slot_end[-1]
    last_used = jnp.maximum(n_used - 1, 0)
    se = jnp.minimum(jnp.searchsorted(slot_end, jnp.minimum(sid, last_used), side='right'), E - 1).astype(jnp.int32)
    rows_in = jnp.clip(cnt[se] - (sid - slot_start[se]) * R, 0, R)
    rows_in = jnp.where(sid < n_used, rows_in, 0)
    nblk = ((rows_in + rb - 1) // rb).astype(jnp.int32)
    blk = jnp.arange(n_rows // rb, dtype=jnp.int32)
    blk_valid = (blk % per_slot) < nblk[blk // per_slot]
    order = jnp.argsort(jnp.logical_not(blk_valid), stable=True).astype(jnp.int32)
    n_valid = jnp.sum(blk_valid.astype(jnp.int32))
    blk_ids = jnp.where(jnp.arange(max_blocks) < n_valid, order[:max_blocks], n_rows // rb).astype(jnp.int32)
    b_slot = jnp.minimum(blk_ids // per_slot, n_slots - 1)
    b_e = se[b_slot]
    blk_base = cstart[b_e] + (b_slot - slot_start[b_e]) * R + (blk_ids % per_slot) * rb
    blk_base = jnp.clip(blk_base, 0, T * TOP_K).astype(jnp.int32)
    xs = _dispatch(u, sorted_tok, blk_ids, blk_base, n_valid.reshape(1), n_rows)
    y = _moe_ffn(xs, se, nblk, n_used.astype(jnp.int32), w_gate, w_up, w_down, layer)
    return _combine(y, dest.reshape(-1).astype(jnp.int32), gates.T, h, mod_l, ln_g, ln_b, S)


def kernel(x, c, positions, ada_w, ada_b, ada_table, ln_g, ln_b, mla_w_dq, mla_q_norm, mla_w_uq, mla_w_dkv, mla_kv_norm, mla_w_ukv, mla_w_o, ca_w_qkv, ca_b_qkv, ca_rel_bias, ca_w_o, ffn_w_gate, ffn_w_up, ffn_w_down, moe_router_w, moe_router_b, moe_w_gate, moe_w_up, moe_w_down):
    B, S, D = x.shape
    T = B * S
    mod = _cond(c, ada_w, ada_b, ada_table)
    half = QK_ROPE // 2
    inv_freq = ROPE_THETA ** (-jnp.arange(half, dtype=F32) / half)
    freq_row = jnp.concatenate([inv_freq, inv_freq, jnp.zeros((LANES - QK_ROPE,), F32)]).reshape(1, LANES)
    pos_col = positions.reshape(T, 1).astype(jnp.int32)
    h = x.reshape(T, D)
    for i in range(DEPTH):
        j = i // 2
        mod_l = mod[i]
        if i % 2 == 0:
            q, k, v = _mla_proj(h, mod_l, pos_col, freq_row, mla_w_dq[j], mla_q_norm[j], mla_w_dkv[j],
                                mla_kv_norm[j], mla_w_uq[j], mla_w_ukv[j], S)
            o = _mla_attn(q, k, v, B, S)
            h = _proj_ln(o, mla_w_o, j, h, mod_l, ln_g[i, 0], ln_b[i, 0], S)
            h = _ffn(h, mod_l, ffn_w_gate, ffn_w_up, ffn_w_down, j, ln_g[i, 1], ln_b[i, 1], S)
        else:
            qkv = _qkv(h, mod_l, ca_w_qkv, j, ca_b_qkv[j], S)
            o = _ca_attn(qkv, ca_rel_bias[j], B, S)
            h = _proj_ln(o, ca_w_o, j, h, mod_l, ln_g[i, 0], ln_b[i, 0], S)
            h = _moe_layer(h, mod_l, moe_router_w[j], moe_router_b[j], moe_w_gate, moe_w_up,
                           moe_w_down, j, ln_g[i, 1], ln_b[i, 1], S)
    return h.reshape(B, S, D)
```
